```python
import math
import jax, jax.numpy as jnp
from jax import lax
import numpy as np

D_MODEL = 4096
BATCH = 4
SEQ = 4096
DEPTH = 1

CHUNK = 64
Q_BLOCK = 128
ATT_HEADS = 16
ATT_HEAD_DIM = 128
ATT_QK = ATT_HEADS * 2 * ATT_HEAD_DIM
ATT_V = ATT_HEADS * 2 * ATT_HEAD_DIM
ROT_DIM = ATT_HEAD_DIM // 4
ROPE_THETA = 500000.0
D_RNN = D_MODEL
RNN_HEADS = 16
RNN_BLOCK = D_RNN // RNN_HEADS
CONV_WIDTH = 4
LRU_C = 8.0
N_BRANCHES = 2
IN_SPLITS = (ATT_QK, 2 * ATT_QK, 2 * ATT_QK + ATT_V, 2 * ATT_QK + ATT_V + D_RNN, 2 * ATT_QK + ATT_V + 2 * D_RNN)
IN_COLS = 2 * ATT_QK + ATT_V + 2 * D_RNN + N_BRANCHES * D_MODEL
N_EXPERTS = 128
N_GROUPS = 8
TOPK_GROUPS = 4
TOP_K = 8
D_EXPERT = 384
D_SHARED = 384
ROUTED_SCALE = 2.5
EXPERT_ROWS = 128
LN_EPS = 1e-5
DEEPNORM_ALPHA = (2 * DEPTH) ** 0.25
DEEPNORM_BETA = (8 * DEPTH) ** -0.25

kernel_name = 'hybrid_diffattn_rglru_moe_block'


def layer_norm(x, g, b):
    xf = x.astype(jnp.float32)
    mu = jnp.mean(xf, -1, keepdims=True)
    var = jnp.mean(jnp.square(xf - mu), -1, keepdims=True)
    y = (xf - mu) * lax.rsqrt(var + LN_EPS) * g.astype(jnp.float32) + b.astype(jnp.float32)
    return y.astype(x.dtype)


def rope_tables(positions):
    inv_freq = jnp.power(ROPE_THETA, -jnp.arange(0, ROT_DIM, 2, dtype=jnp.float32) / ROT_DIM)
    ang = positions.astype(jnp.float32)[..., None] * inv_freq
    return jnp.cos(ang), jnp.sin(ang)


def apply_partial_rope(t, cos, sin):
    c = cos[:, :, None, None, :]
    s = sin[:, :, None, None, :]
    half = ROT_DIM // 2
    tf = t[..., :ROT_DIM].astype(jnp.float32)
    x1, x2 = tf[..., :half], tf[..., half:]
    rot = jnp.concatenate([x1 * c - x2 * s, x2 * c + x1 * s], axis=-1).astype(t.dtype)
    return jnp.concatenate([rot, t[..., ROT_DIM:]], axis=-1)


def diff_attention(q, k, v, cos, sin, lam, subln_g, lambda_init):
    B, S = q.shape[0], q.shape[1]
    q = apply_partial_rope(q.reshape(B, S, ATT_HEADS, 2, ATT_HEAD_DIM), cos, sin)
    k = apply_partial_rope(k.reshape(B, S, ATT_HEADS, 2, ATT_HEAD_DIM), cos, sin)
    q = q.transpose(0, 2, 3, 1, 4)
    k = k.transpose(0, 2, 3, 1, 4)
    v = v.reshape(B, S, ATT_HEADS, 2 * ATT_HEAD_DIM).transpose(0, 2, 1, 3)
    scale = ATT_HEAD_DIM ** -0.5
    outs = []
    for blk in range(S // Q_BLOCK):
        q0 = blk * Q_BLOCK
        q_end = q0 + Q_BLOCK
        s = jnp.einsum('bhmqd,bhmkd->bhmqk', q[:, :, :, q0:q_end], k[:, :, :, :q_end]).astype(jnp.float32) * scale
        mask = (jnp.arange(q_end) // CHUNK)[None, :] <= (jnp.arange(q0, q_end) // CHUNK)[:, None]
        p = jax.nn.softmax(jnp.where(mask, s, -jnp.inf), axis=-1)
        a = p[:, :, 0] - lam * p[:, :, 1]
        outs.append(jnp.einsum('bhqk,bhkv->bhqv', a.astype(v.dtype), v[:, :, :q_end]))
    o = jnp.concatenate(outs, axis=2).astype(jnp.float32)
    o = o * lax.rsqrt(jnp.mean(jnp.square(o), -1, keepdims=True) + LN_EPS) * subln_g.astype(jnp.float32)
    o = o * (1.0 - lambda_init)
    return o.transpose(0, 2, 1, 3).reshape(B, S, ATT_V).astype(q.dtype)


def rglru_branch(xr, yr, conv_w, conv_b, w_ra, b_ra, w_rx, b_rx, lru_lambda):
    B, S = xr.shape[0], xr.shape[1]
    xc = lax.conv_general_dilated(xr, conv_w[:, None, :], window_strides=(1,), padding=[(CONV_WIDTH - 1, 0)],
                                  dimension_numbers=('NWC', 'WIO', 'NWC'), feature_group_count=D_RNN) + conv_b
    xh = xc.reshape(B, S, RNN_HEADS, RNN_BLOCK)
    r = jax.nn.sigmoid((jnp.einsum('bshi,hij->bshj', xh, w_ra) + b_ra).astype(jnp.float32)).reshape(B, S, D_RNN)
    i = jax.nn.sigmoid((jnp.einsum('bshi,hij->bshj', xh, w_rx) + b_rx).astype(jnp.float32)).reshape(B, S, D_RNN)
    log_a = -LRU_C * r * jax.nn.softplus(-lru_lambda.astype(jnp.float32))
    a = jnp.exp(log_a)
    u = jnp.sqrt(-jnp.expm1(2.0 * log_a)) * i * xc.astype(jnp.float32)

    def step(h, au):
        a_t, u_t = au
        h = a_t * h + u_t
        return h, h

    _, hs = lax.scan(step, jnp.zeros((B, D_RNN), jnp.float32), (a.swapaxes(0, 1), u.swapaxes(0, 1)))
    return hs.swapaxes(0, 1).astype(xr.dtype) * jax.nn.gelu(yr)


def moe_ffn(x2, w_router, router_bias, wg, wu, wd, wsg, wsu, wsd):
    T = x2.shape[0]
    scores = jax.nn.sigmoid((x2 @ w_router).astype(jnp.float32))
    biased = scores + router_bias.astype(jnp.float32)
    grp = biased.reshape(T, N_GROUPS, N_EXPERTS // N_GROUPS)
    grp_score = lax.top_k(grp, 2)[0].sum(-1)
    _, gidx = lax.top_k(grp_score, TOPK_GROUPS)
    gmask = jax.nn.one_hot(gidx, N_GROUPS, dtype=jnp.float32).sum(-2) > 0
    emask = jnp.repeat(gmask, N_EXPERTS // N_GROUPS, axis=-1)
    _, eidx = lax.top_k(jnp.where(emask, biased, -jnp.inf), TOP_K)
    w = jnp.take_along_axis(scores, eidx, axis=-1)
    w = w / jnp.sum(w, -1, keepdims=True) * ROUTED_SCALE

    TK = T * TOP_K
    flat_e = eidx.reshape(-1).astype(jnp.int32)
    flat_t = jnp.repeat(jnp.arange(T, dtype=jnp.int32), TOP_K)
    flat_w = w.reshape(-1)
    order = jnp.argsort(flat_e)
    sorted_e = flat_e[order]
    counts = jnp.bincount(flat_e, length=N_EXPERTS).astype(jnp.int32)
    starts = jnp.cumsum(counts) - counts
    padded = (counts + EXPERT_ROWS - 1) // EXPERT_ROWS * EXPERT_ROWS
    pend = jnp.cumsum(padded)
    pstart = pend - padded
    dest = pstart[sorted_e] + jnp.arange(TK, dtype=jnp.int32) - starts[sorted_e]
    n_blocks = -(-TK // EXPERT_ROWS) + N_EXPERTS
    n_rows = n_blocks * EXPERT_ROWS
    row_tok = jnp.zeros((n_rows,), jnp.int32).at[dest].set(flat_t[order])
    row_w = jnp.zeros((n_rows,), jnp.float32).at[dest].set(flat_w[order])
    blk_e = jnp.minimum(jnp.searchsorted(pend, jnp.arange(n_blocks, dtype=jnp.int32) * EXPERT_ROWS, side='right'),
                        N_EXPERTS - 1).astype(jnp.int32)

    def body(acc, blk):
        e, tok, wt = blk
        xb = x2[tok]
        hb = jax.nn.silu(xb @ wg[e]) * (xb @ wu[e])
        yb = (hb @ wd[e]).astype(jnp.float32) * wt[:, None]
        return acc.at[tok].add(yb), None

    routed, _ = lax.scan(body, jnp.zeros(x2.shape, jnp.float32),
                         (blk_e, row_tok.reshape(n_blocks, EXPERT_ROWS), row_w.reshape(n_blocks, EXPERT_ROWS)))
    shared = (jax.nn.silu(x2 @ wsg) * (x2 @ wsu)) @ wsd
    return routed.astype(x2.dtype) + shared


def setup_inputs(seed: int = 0) -> dict:
    key = jax.random.key(seed)
    ks = iter(jax.random.split(key, 40))
    f32 = jnp.float32
    L = DEPTH

    def nrm(shape, scale):
        return jax.random.normal(next(ks), shape, f32) * scale

    x = jax.random.normal(next(ks), (BATCH, SEQ, D_MODEL), f32)
    offsets = jax.random.randint(next(ks), (BATCH, 1), 0, 4096, dtype=jnp.int32)
    positions = offsets + jnp.arange(SEQ, dtype=jnp.int32)[None, :]
    ln_in_g = 1.0 + nrm((D_MODEL,), 0.02)
    ln_in_b = nrm((D_MODEL,), 0.02)
    w_in = nrm((L, D_MODEL, IN_COLS), D_MODEL ** -0.5)
    b_gate = nrm((L, N_BRANCHES * D_MODEL), 0.02)
    lam_q1 = nrm((L, ATT_HEAD_DIM), 0.1)
    lam_k1 = nrm((L, ATT_HEAD_DIM), 0.1)
    lam_q2 = nrm((L, ATT_HEAD_DIM), 0.1)
    lam_k2 = nrm((L, ATT_HEAD_DIM), 0.1)
    subln_g = 1.0 + nrm((L, 2 * ATT_HEAD_DIM), 0.02)
    conv_w = nrm((L, CONV_WIDTH, D_RNN), CONV_WIDTH ** -0.5)
    conv_b = nrm((L, D_RNN), 0.02)
    w_rg_a = nrm((L, RNN_HEADS, RNN_BLOCK, RNN_BLOCK), RNN_BLOCK ** -0.5)
    b_rg_a = nrm((L, RNN_HEADS, RNN_BLOCK), 0.02)
    w_rg_x = nrm((L, RNN_HEADS, RNN_BLOCK, RNN_BLOCK), RNN_BLOCK ** -0.5)
    b_rg_x = nrm((L, RNN_HEADS, RNN_BLOCK), 0.02)
    u = jax.random.uniform(next(ks), (L, D_RNN), f32, 0.9, 0.999)
    p = u ** (1.0 / LRU_C)
    lru_lambda = jnp.log(p) - jnp.log1p(-p)
    w_proj_attn = nrm((L, ATT_V, D_MODEL), ATT_V ** -0.5)
    w_proj_rnn = nrm((L, D_RNN, D_MODEL), D_RNN ** -0.5)
    w_out = nrm((L, D_MODEL, D_MODEL), DEEPNORM_BETA * D_MODEL ** -0.5)
    ln1_g = 1.0 + nrm((L, D_MODEL), 0.02)
    ln1_b = nrm((L, D_MODEL), 0.02)
    w_router = nrm((L, D_MODEL, N_EXPERTS), D_MODEL ** -0.5)
    router_bias = nrm((L, N_EXPERTS), 0.01)
    w_exp_gate = nrm((L, N_EXPERTS, D_MODEL, D_EXPERT), D_MODEL ** -0.5)
    w_exp_up = nrm((L, N_EXPERTS, D_MODEL, D_EXPERT), D_MODEL ** -0.5)
    w_exp_down = nrm((L, N_EXPERTS, D_EXPERT, D_MODEL), DEEPNORM_BETA * D_EXPERT ** -0.5)
    w_sh_gate = nrm((L, D_MODEL, D_SHARED), D_MODEL ** -0.5)
    w_sh_up = nrm((L, D_MODEL, D_SHARED), D_MODEL ** -0.5)
    w_sh_down = nrm((L, D_SHARED, D_MODEL), DEEPNORM_BETA * D_SHARED ** -0.5)
    ln2_g = 1.0 + nrm((L, D_MODEL), 0.02)
    ln2_b = nrm((L, D_MODEL), 0.02)
    return {'x': x, 'positions': positions, 'ln_in_g': ln_in_g, 'ln_in_b': ln_in_b, 'w_in': w_in,
            'b_gate': b_gate, 'lam_q1': lam_q1, 'lam_k1': lam_k1, 'lam_q2': lam_q2, 'lam_k2': lam_k2,
            'subln_g': subln_g, 'conv_w': conv_w, 'conv_b': conv_b, 'w_rg_a': w_rg_a, 'b_rg_a': b_rg_a,
            'w_rg_x': w_rg_x, 'b_rg_x': b_rg_x, 'lru_lambda': lru_lambda, 'w_proj_attn': w_proj_attn,
            'w_proj_rnn': w_proj_rnn, 'w_out': w_out, 'ln1_g': ln1_g, 'ln1_b': ln1_b, 'w_router': w_router,
            'router_bias': router_bias, 'w_exp_gate': w_exp_gate, 'w_exp_up': w_exp_up,
            'w_exp_down': w_exp_down, 'w_sh_gate': w_sh_gate, 'w_sh_up': w_sh_up, 'w_sh_down': w_sh_down,
            'ln2_g': ln2_g, 'ln2_b': ln2_b}


def reference(x, positions, ln_in_g, ln_in_b, w_in, b_gate, lam_q1, lam_k1, lam_q2, lam_k2, subln_g,
              conv_w, conv_b, w_rg_a, b_rg_a, w_rg_x, b_rg_x, lru_lambda, w_proj_attn, w_proj_rnn, w_out,
              ln1_g, ln1_b, w_router, router_bias, w_exp_gate, w_exp_up, w_exp_down, w_sh_gate, w_sh_up,
              w_sh_down, ln2_g, ln2_b):
    B, S, D = x.shape
    cos, sin = rope_tables(positions)
    h = layer_norm(x, ln_in_g, ln_in_b)
    for l in range(DEPTH):
        lambda_init = 0.8 - 0.6 * math.exp(-0.3 * l)
        proj = h @ w_in[l]
        q, k, v, xr, yr, g = jnp.split(proj, list(IN_SPLITS), axis=-1)
        gates = jax.nn.sigmoid((g + b_gate[l]).astype(jnp.float32)).astype(h.dtype).reshape(B, S, N_BRANCHES, D)
        lam = (jnp.exp(jnp.sum(lam_q1[l].astype(jnp.float32) * lam_k1[l].astype(jnp.float32)))
               - jnp.exp(jnp.sum(lam_q2[l].astype(jnp.float32) * lam_k2[l].astype(jnp.float32))) + lambda_init)
        attn = diff_attention(q, k, v, cos, sin, lam, subln_g[l], lambda_init)
        rnn = rglru_branch(xr, yr, conv_w[l], conv_b[l], w_rg_a[l], b_rg_a[l], w_rg_x[l], b_rg_x[l], lru_lambda[l])
        mixed = gates[:, :, 0] * (attn @ w_proj_attn[l]) + gates[:, :, 1] * (rnn @ w_proj_rnn[l])
        h = layer_norm(DEEPNORM_ALPHA * h + mixed @ w_out[l], ln1_g[l], ln1_b[l])
        ffn = moe_ffn(h.reshape(B * S, D), w_router[l], router_bias[l], w_exp_gate[l], w_exp_up[l],
                      w_exp_down[l], w_sh_gate[l], w_sh_up[l], w_sh_down[l]).reshape(B, S, D)
        h = layer_norm(DEEPNORM_ALPHA * h + ffn, ln2_g[l], ln2_b[l])
    return h
```

```python
import functools
import math

import jax
import jax.numpy as jnp
from jax import lax
from jax.experimental import pallas as pl
from jax.experimental.pallas import tpu as pltpu

F32 = jnp.float32
BF16 = jnp.bfloat16
I32 = jnp.int32

CHUNK = 64
ROPE_THETA = 500000.0
ROT_FRACTION = 4
LRU_C = 8.0
CONV_WIDTH = 4
N_GROUPS = 8
TOPK_GROUPS = 4
TOP_K = 8
ROUTED_SCALE = 2.5
LN_EPS = 1e-5
LOG2E = 1.4426950408889634

LANES = 128
SUBLANES = 8
VMEM_LIMIT_BYTES = 56 * 1024 * 1024
EXPERT_ROWS = 128


def _tile(n, pref):
    t = min(n, pref)
    while n % t:
        t //= 2
    return t


def _params(*sem):
    return pltpu.CompilerParams(dimension_semantics=sem, vmem_limit_bytes=VMEM_LIMIT_BYTES)


def _ln_rows(x, g, b):
    mu = jnp.mean(x, -1, keepdims=True)
    xc = x - mu
    var = jnp.mean(xc * xc, -1, keepdims=True)
    return xc * lax.rsqrt(var + LN_EPS) * g + b


def _silu(x):
    return x * jax.nn.sigmoid(x)


def _ln_in_kernel(x_ref, g_ref, b_ref, yf_ref, yb_ref):
    y = _ln_rows(x_ref[...], g_ref[...], b_ref[...])
    yf_ref[...] = y
    yb_ref[...] = y.astype(BF16)


def _ln_in(x2, g, b):
    T, D = x2.shape
    tm = _tile(T, 256)
    row = pl.BlockSpec((tm, D), lambda i: (i, 0))
    vec = pl.BlockSpec((1, D), lambda i: (0, 0))
    return pl.pallas_call(
        _ln_in_kernel, grid=(T // tm,),
        in_specs=[row, vec, vec], out_specs=[row, row],
        out_shape=[jax.ShapeDtypeStruct((T, D), F32), jax.ShapeDtypeStruct((T, D), BF16)],
        compiler_params=_params("parallel"), name="ln_in",
    )(x2, g.reshape(1, D), b.reshape(1, D))


def _in_proj_kernel(a_ref, w_ref, cos_ref, sin_ref, o_ref, *, n_q, n_qk, q_scale, half):
    j = pl.program_id(1)
    acc = jnp.dot(a_ref[...], w_ref[...], preferred_element_type=F32)
    tn = acc.shape[1]

    @pl.when(j >= n_qk)
    def _():
        o_ref[...] = acc.astype(BF16)

    @pl.when(j < n_qk)
    def _():
        reps = tn // LANES
        cosf = jnp.tile(cos_ref[...], (1, reps))
        sinf = jnp.tile(sin_ref[...], (1, reps))
        lane = lax.broadcasted_iota(I32, acc.shape, 1) % LANES
        partner = jnp.where(lane < half, pltpu.roll(acc, tn - half, 1), pltpu.roll(acc, half, 1))
        r = acc * cosf + partner * sinf
        r = r * jnp.where(j < n_q, q_scale, 1.0)
        o_ref[...] = r.astype(BF16)


def _in_proj(h0b, w, cosf, sinf, *, att_qk, q_scale, half):
    T, K = h0b.shape
    N = w.shape[1]
    tm = _tile(T, 1024)
    tn = _tile(att_qk, 1024)
    assert N % tn == 0
    kern = functools.partial(_in_proj_kernel, n_q=att_qk // tn, n_qk=2 * att_qk // tn,
                             q_scale=q_scale, half=half)
    return pl.pallas_call(
        kern, grid=(T // tm, N // tn),
        in_specs=[pl.BlockSpec((tm, K), lambda i, j: (i, 0)),
                  pl.BlockSpec((K, tn), lambda i, j: (0, j)),
                  pl.BlockSpec((tm, LANES), lambda i, j: (i, 0)),
                  pl.BlockSpec((tm, LANES), lambda i, j: (i, 0))],
        out_specs=pl.BlockSpec((tm, tn), lambda i, j: (i, j)),
        out_shape=jax.ShapeDtypeStruct((T, N), BF16),
        compiler_params=_params("parallel", "arbitrary"), name="in_proj",
    )(h0b, w, cosf, sinf)


def _attn_kernel(lam_ref, q_ref, k_ref, v_ref, g_ref, o_ref, *, dh, tk, out_scale):
    i = pl.program_id(2)
    tq = q_ref.shape[0]
    qs = [q_ref[:, m * dh:(m + 1) * dh] for m in range(2)]

    def update(carry, s, vj):
        mx, l, acc = carry
        mn = jnp.maximum(mx, jnp.max(s, -1, keepdims=True))
        p = jnp.exp2(s - mn)
        corr = jnp.exp2(mx - mn)
        l = corr * l + jnp.sum(p, -1, keepdims=True)
        acc = corr * acc + jnp.dot(p.astype(BF16), vj, preferred_element_type=F32)
        return mn, l, acc

    def scores(m, kj):
        return lax.dot_general(qs[m], kj, (((1,), (1,)), ((), ())), preferred_element_type=F32)

    def body(j, carry):
        r0 = pl.multiple_of(j * tk, tk)
        vj = v_ref[pl.ds(r0, tk), :]
        return tuple(update(carry[m], scores(m, k_ref[pl.ds(r0, tk), m * dh:(m + 1) * dh]), vj)
                     for m in range(2))

    init = tuple((jnp.full((tq, 1), -jnp.inf, F32), jnp.zeros((tq, 1), F32),
                  jnp.zeros((tq, v_ref.shape[1]), F32)) for _ in range(2))
    carry = lax.fori_loop(0, i * (tq // tk), body, init)

    d0 = pl.multiple_of(i * tq, tq)
    vd = v_ref[pl.ds(d0, tq), :]
    qc = lax.broadcasted_iota(I32, (tq, tq), 0) // CHUNK
    kc = lax.broadcasted_iota(I32, (tq, tq), 1) // CHUNK
    outs = []
    for m in range(2):
        s = scores(m, k_ref[pl.ds(d0, tq), m * dh:(m + 1) * dh])
        s = jnp.where(kc <= qc, s, -jnp.inf)
        _, l, acc = update(carry[m], s, vd)
        outs.append(acc / l)
    o = outs[0] - lam_ref[0] * outs[1]
    o = o * lax.rsqrt(jnp.mean(o * o, -1, keepdims=True) + LN_EPS) * g_ref[...]
    o_ref[...] = (o * out_scale).astype(BF16)


def _attention(proj, lam, subln_g, *, B, S, heads, dh, att_qk, lambda_init):
    T = B * S
    wv = 2 * dh
    tq = _tile(S, 256)
    tk = tq
    nq = S // tq
    kern = functools.partial(_attn_kernel, dh=dh, tk=tk, out_scale=1.0 - lambda_init)
    return pl.pallas_call(
        kern, grid=(B, heads, nq),
        in_specs=[pl.BlockSpec(memory_space=pltpu.SMEM),
                  pl.BlockSpec((tq, wv), lambda b, h, i: (b * nq + i, h)),
                  pl.BlockSpec((S, wv), lambda b, h, i: (b, att_qk // wv + h)),
                  pl.BlockSpec((S, wv), lambda b, h, i: (b, 2 * att_qk // wv + h)),
                  pl.BlockSpec((1, wv), lambda b, h, i: (0, 0))],
        out_specs=pl.BlockSpec((tq, wv), lambda b, h, i: (b * nq + i, h)),
        out_shape=jax.ShapeDtypeStruct((T, heads * wv), BF16),
        compiler_params=_params("parallel", "parallel", "arbitrary"), name="attention",
    )(lam, proj, proj, proj, subln_g.reshape(1, wv))


def _gelu_tanh(x):
    return 0.5 * x * (1.0 + jnp.tanh(0.7978845608028654 * (x + 0.044715 * (x * x * x))))


def _rglru_kernel(xr_ref, yr_ref, cw_ref, cb_ref, wa_ref, ba_ref, wx_ref, bx_ref, lam_ref, o_ref,
                  a_s, u_s, xprev_s, h_s, *, n_heads, blk, scan_cols):
    @pl.when(pl.program_id(1) == 0)
    def _():
        xprev_s[...] = jnp.zeros_like(xprev_s)
        h_s[...] = jnp.zeros_like(h_s)

    tt, D = xr_ref.shape
    z = -lam_ref[...]
    sp = jnp.maximum(z, 0.0) + jnp.log1p(jnp.exp(-jnp.abs(z)))
    for hh in range(n_heads):
        sl = slice(hh * blk, (hh + 1) * blk)
        xe = jnp.concatenate([xprev_s[:, sl], xr_ref[:, sl].astype(F32)], axis=0)
        xc = cb_ref[:, sl]
        for j in range(CONV_WIDTH):
            o = SUBLANES - (CONV_WIDTH - 1) + j
            xc = xc + cw_ref[j:j + 1, sl] * xe[o:o + tt]
        xb = xc.astype(BF16)
        r = jax.nn.sigmoid(jnp.dot(xb, wa_ref[hh], preferred_element_type=F32) + ba_ref[:, sl])
        ig = jax.nn.sigmoid(jnp.dot(xb, wx_ref[hh], preferred_element_type=F32) + bx_ref[:, sl])
        log_a = (-LRU_C * r) * sp[:, sl]
        a = jnp.exp(log_a)
        a_s[:, sl] = a
        u_s[:, sl] = jnp.sqrt(-jnp.tanh(log_a) * (a * a + 1.0)) * ig * xc
    xprev_s[...] = xr_ref[tt - SUBLANES:tt, :].astype(F32)

    row = lax.broadcasted_iota(I32, (SUBLANES, scan_cols), 0)

    def scan_body(g, h):
        r0 = pl.multiple_of(g * SUBLANES, SUBLANES)
        outs = []
        for c in range(D // scan_cols):
            cs = slice(c * scan_cols, (c + 1) * scan_cols)
            a = a_s[pl.ds(r0, SUBLANES), cs]
            u = u_s[pl.ds(r0, SUBLANES), cs]
            for s in (1, 2, 4):
                valid = row >= s
                u = jnp.where(valid, a * pltpu.roll(u, s, 0) + u, u)
                a = jnp.where(valid, a * pltpu.roll(a, s, 0), a)
            hg = u + a * h[:, cs]
            u_s[pl.ds(r0, SUBLANES), cs] = hg
            outs.append(hg[SUBLANES - 1:SUBLANES, :])
        return jnp.concatenate(outs, axis=1)

    h_s[...] = lax.fori_loop(0, tt // SUBLANES, scan_body, h_s[...])
    o_ref[...] = (u_s[...] * _gelu_tanh(yr_ref[...].astype(F32))).astype(BF16)


def _rglru(proj, conv_w, conv_b, w_ra, b_ra, w_rx, b_rx, lru_lambda, *, B, S, off_xr):
    T = B * S
    n_heads, blk, _ = w_ra.shape
    D = n_heads * blk
    tt = _tile(S, 256)
    nt = S // tt
    assert off_xr % D == 0
    cx = off_xr // D
    vec = pl.BlockSpec((1, D), lambda b, t: (0, 0))
    wspec = pl.BlockSpec((n_heads, blk, blk), lambda b, t: (0, 0, 0))
    kern = functools.partial(_rglru_kernel, n_heads=n_heads, blk=blk, scan_cols=_tile(D, 512))
    return pl.pallas_call(
        kern, grid=(B, nt),
        in_specs=[pl.BlockSpec((tt, D), lambda b, t: (b * nt + t, cx)),
                  pl.BlockSpec((tt, D), lambda b, t: (b * nt + t, cx + 1)),
                  pl.BlockSpec((CONV_WIDTH, D), lambda b, t: (0, 0)), vec,
                  wspec, vec, wspec, vec, vec],
        out_specs=pl.BlockSpec((tt, D), lambda b, t: (b * nt + t, 0)),
        out_shape=jax.ShapeDtypeStruct((T, D), BF16),
        scratch_shapes=[pltpu.VMEM((tt, D), F32), pltpu.VMEM((tt, D), F32),
                        pltpu.VMEM((SUBLANES, D), F32), pltpu.VMEM((1, D), F32)],
        compiler_params=_params("parallel", "arbitrary"), name="rglru",
    )(proj, proj, conv_w, conv_b.reshape(1, D), w_ra.astype(BF16), b_ra.reshape(1, D),
      w_rx.astype(BF16), b_rx.reshape(1, D), lru_lambda.reshape(1, D))


def _mix_kernel(at_ref, rn_ref, wa_ref, wr_ref, g0_ref, g1_ref, b0_ref, b1_ref, o_ref):
    pa = jnp.dot(at_ref[...], wa_ref[...], preferred_element_type=F32)
    pr = jnp.dot(rn_ref[...], wr_ref[...], preferred_element_type=F32)
    s0 = jax.nn.sigmoid(g0_ref[...].astype(F32) + b0_ref[...])
    s1 = jax.nn.sigmoid(g1_ref[...].astype(F32) + b1_ref[...])
    o_ref[...] = (s0 * pa + s1 * pr).astype(BF16)


def _mix(attn, rnn, wa, wr, proj, b_gate, *, off_g):
    T, Ka = attn.shape
    Kr = rnn.shape[1]
    D = wa.shape[1]
    tm = _tile(T, 512)
    tn = _tile(D, 512)
    assert off_g % tn == 0
    cg = off_g // tn
    nn = D // tn
    return pl.pallas_call(
        _mix_kernel, grid=(T // tm, nn),
        in_specs=[pl.BlockSpec((tm, Ka), lambda i, j: (i, 0)),
                  pl.BlockSpec((tm, Kr), lambda i, j: (i, 0)),
                  pl.BlockSpec((Ka, tn), lambda i, j: (0, j)),
                  pl.BlockSpec((Kr, tn), lambda i, j: (0, j)),
                  pl.BlockSpec((tm, tn), lambda i, j: (i, cg + j)),
                  pl.BlockSpec((tm, tn), lambda i, j: (i, cg + nn + j)),
                  pl.BlockSpec((1, tn), lambda i, j: (0, j)),
                  pl.BlockSpec((1, tn), lambda i, j: (0, nn + j))],
        out_specs=pl.BlockSpec((tm, tn), lambda i, j: (i, j)),
        out_shape=jax.ShapeDtypeStruct((T, D), BF16),
        compiler_params=_params("parallel", "arbitrary"), name="mix",
    )(attn, rnn, wa, wr, proj, proj, b_gate.reshape(1, 2 * D), b_gate.reshape(1, 2 * D))


def _out_ln_kernel(a_ref, w_ref, h0_ref, g_ref, b_ref, hf_ref, hb_ref, acc_s, *, alpha):
    j = pl.program_id(1)
    n_n, _, tn = acc_s.shape
    acc_s[j] = alpha * h0_ref[...] + jnp.dot(a_ref[...], w_ref[...], preferred_element_type=F32)

    @pl.when(j == n_n - 1)
    def _():
        D = n_n * tn
        mu = sum(jnp.sum(acc_s[c], -1, keepdims=True) for c in range(n_n)) / D
        var = sum(jnp.sum(jnp.square(acc_s[c] - mu), -1, keepdims=True) for c in range(n_n)) / D
        rstd = lax.rsqrt(var + LN_EPS)
        for c in range(n_n):
            cs = slice(c * tn, (c + 1) * tn)
            y = (acc_s[c] - mu) * rstd * g_ref[:, cs] + b_ref[:, cs]
            hf_ref[:, cs] = y
            hb_ref[:, cs] = y.astype(BF16)


def _out_ln(mixed, w, h0f, g, b, *, alpha):
    T, K = mixed.shape
    D = w.shape[1]
    tm = _tile(T, 256)
    tn = _tile(D, 512)
    vec = pl.BlockSpec((1, D), lambda i, j: (0, 0))
    row = pl.BlockSpec((tm, D), lambda i, j: (i, 0))
    return pl.pallas_call(
        functools.partial(_out_ln_kernel, alpha=alpha), grid=(T // tm, D // tn),
        in_specs=[pl.BlockSpec((tm, K), lambda i, j: (i, 0)),
                  pl.BlockSpec((K, tn), lambda i, j: (0, j)),
                  pl.BlockSpec((tm, tn), lambda i, j: (i, j)), vec, vec],
        out_specs=[row, row],
        out_shape=[jax.ShapeDtypeStruct((T, D), F32), jax.ShapeDtypeStruct((T, D), BF16)],
        scratch_shapes=[pltpu.VMEM((D // tn, tm, tn), F32)],
        compiler_params=_params("parallel", "arbitrary"), name="out_ln",
    )(mixed, w, h0f, g.reshape(1, D), b.reshape(1, D))


def _seg_allreduce(x, lane, width, op):
    n = x.shape[1]
    s = 1
    while s < width:
        partner = jnp.where((lane & s) == 0, pltpu.roll(x, n - s, 1), pltpu.roll(x, s, 1))
        x = op(x, partner)
        s *= 2
    return x


def _router_kernel(h_ref, w_ref, bias_ref, cw_ref, sel_ref):
    h = h_ref[...]
    w = w_ref[...]
    h_hi = h.astype(BF16)
    h_lo = (h - h_hi.astype(F32)).astype(BF16)
    w_hi = w.astype(BF16)
    w_lo = (w - w_hi.astype(F32)).astype(BF16)
    logits = (jnp.dot(h_hi, w_hi, preferred_element_type=F32)
              + (jnp.dot(h_hi, w_lo, preferred_element_type=F32)
                 + jnp.dot(h_lo, w_hi, preferred_element_type=F32)))
    scores = jax.nn.sigmoid(logits)
    biased = scores + bias_ref[...]
    E = biased.shape[1]
    gw = E // N_GROUPS
    lane = lax.broadcasted_iota(I32, biased.shape, 1)
    neg = -jnp.inf

    m1 = _seg_allreduce(biased, lane, gw, jnp.maximum)
    is_m1 = biased == m1
    cnt = _seg_allreduce(is_m1.astype(F32), lane, gw, jnp.add)
    m2 = jnp.where(cnt >= 2.0, m1, _seg_allreduce(jnp.where(is_m1, neg, biased), lane, gw, jnp.maximum))
    gs = m1 + m2

    gi = lane // gw
    rank = jnp.zeros(biased.shape, I32)
    for k in range(1, N_GROUPS):
        other = pltpu.roll(gs, k * gw, 1)
        ogi = (gi - k) % N_GROUPS
        beats = (other > gs) | ((other == gs) & (ogi < gi))
        rank = rank + beats.astype(I32)
    cur = jnp.where(rank < TOPK_GROUPS, biased, neg)

    sel = jnp.zeros(biased.shape, jnp.bool_)
    for _ in range(TOP_K):
        mx = jnp.max(cur, -1, keepdims=True)
        first = jnp.min(jnp.where(cur == mx, lane, E), -1, keepdims=True)
        pick = lane == first
        sel = sel | pick
        cur = jnp.where(pick, neg, cur)
    wsel = jnp.where(sel, scores, 0.0)
    cw_ref[...] = wsel / jnp.sum(wsel, -1, keepdims=True) * ROUTED_SCALE
    sel_ref[...] = sel.astype(F32)


def _router(h1f, w_router, router_bias):
    T, D = h1f.shape
    E = w_router.shape[1]
    assert E == LANES and E % N_GROUPS == 0
    tm = _tile(T, 512)
    row = pl.BlockSpec((tm, E), lambda i: (i, 0))
    return pl.pallas_call(
        _router_kernel, grid=(T // tm,),
        in_specs=[pl.BlockSpec((tm, D), lambda i: (i, 0)),
                  pl.BlockSpec((D, E), lambda i: (0, 0)),
                  pl.BlockSpec((1, E), lambda i: (0, 0))],
        out_specs=[row, row],
        out_shape=[jax.ShapeDtypeStruct((T, E), F32), jax.ShapeDtypeStruct((T, E), F32)],
        compiler_params=_params("parallel"), name="router",
    )(h1f, w_router, router_bias.reshape(1, E))


def _row_copy(src_hbm, src_row, buf, slot, r, sem):
    return pltpu.make_async_copy(src_hbm.at[pl.ds(src_row, 1), :], buf.at[slot, pl.ds(r, 1), :], sem.at[slot])


def _experts_kernel(blk_e_ref, n_used_ref, tok_cur_ref, tok_nxt_ref, h_hbm, rw_ref, wg_ref, wu_ref, wd_ref,
                    y_ref, xbuf, sem):
    i = pl.program_id(0)
    n_used = n_used_ref[0]
    R = xbuf.shape[1]
    slot = i % 2

    def gather(tok_ref, s):
        def issue(r, c):
            _row_copy(h_hbm, tok_ref[0, 0, r], xbuf, s, r, sem).start()
            return c
        lax.fori_loop(0, R, issue, 0)

    @pl.when(jnp.logical_and(i == 0, n_used > 0))
    def _():
        gather(tok_cur_ref, 0)

    @pl.when(i + 1 < n_used)
    def _():
        gather(tok_nxt_ref, 1 - slot)

    @pl.when(i < n_used)
    def _():
        def wait(r, c):
            _row_copy(h_hbm, 0, xbuf, slot, r, sem).wait()
            return c
        lax.fori_loop(0, R, wait, 0)
        x = xbuf[slot].astype(BF16)
        g = jnp.dot(x, wg_ref[0], preferred_element_type=F32)
        u = jnp.dot(x, wu_ref[0], preferred_element_type=F32)
        hb = (_silu(g) * u).astype(BF16)
        y_ref[...] = jnp.dot(hb, wd_ref[0], preferred_element_type=F32) * rw_ref[...]

    @pl.when(i >= n_used)
    def _():
        y_ref[...] = jnp.zeros_like(y_ref)


def _experts(h1f, blk_e, n_used, row_tok, row_w, wg, wu, wd):
    T, D = h1f.shape
    E, _, F = wg.shape
    n_blocks = blk_e.shape[0]
    R = EXPERT_ROWS
    tok3 = row_tok.reshape(n_blocks, 1, R)
    grid_spec = pltpu.PrefetchScalarGridSpec(
        num_scalar_prefetch=2, grid=(n_blocks,),
        in_specs=[pl.BlockSpec((1, 1, R), lambda i, be, nu: (i, 0, 0), memory_space=pltpu.SMEM),
                  pl.BlockSpec((1, 1, R), lambda i, be, nu: (jnp.minimum(i + 1, n_blocks - 1), 0, 0),
                               memory_space=pltpu.SMEM),
                  pl.BlockSpec(memory_space=pl.ANY),
                  pl.BlockSpec((R, 1), lambda i, be, nu: (i, 0)),
                  pl.BlockSpec((1, D, F), lambda i, be, nu: (be[i], 0, 0)),
                  pl.BlockSpec((1, D, F), lambda i, be, nu: (be[i], 0, 0)),
                  pl.BlockSpec((1, F, D), lambda i, be, nu: (be[i], 0, 0))],
        out_specs=pl.BlockSpec((R, D), lambda i, be, nu: (i, 0)),
        scratch_shapes=[pltpu.VMEM((2, R, D), F32), pltpu.SemaphoreType.DMA((2,))])
    return pl.pallas_call(
        _experts_kernel, grid_spec=grid_spec,
        out_shape=jax.ShapeDtypeStruct((n_blocks * R, D), F32),
        compiler_params=_params("arbitrary"), name="experts",
    )(blk_e, n_used, tok3, tok3, h1f, row_w.reshape(n_blocks * R, 1), wg, wu, wd)


def _final_kernel(d8_ref, ys_hbm, hf_ref, hb_ref, wsg_ref, wsu_ref, wsd_ref, g_ref, b_ref, o_ref, gbuf, sem,
                  *, alpha):
    tm = hf_ref.shape[0]
    n = tm * TOP_K

    def issue(r, c):
        pltpu.make_async_copy(ys_hbm.at[pl.ds(d8_ref[0, 0, r], 1), :], gbuf.at[pl.ds(r, 1), :], sem.at[0]).start()
        return c
    lax.fori_loop(0, n, issue, 0)

    x = hb_ref[...]
    hs = (_silu(jnp.dot(x, wsg_ref[...], preferred_element_type=F32))
          * jnp.dot(x, wsu_ref[...], preferred_element_type=F32)).astype(BF16)
    acc = alpha * hf_ref[...] + jnp.dot(hs, wsd_ref[...], preferred_element_type=F32)

    def wait(r, c):
        pltpu.make_async_copy(ys_hbm.at[pl.ds(0, 1), :], gbuf.at[pl.ds(r, 1), :], sem.at[0]).wait()
        return c
    lax.fori_loop(0, n, wait, 0)

    routed = gbuf[0:tm, :]
    for k in range(1, TOP_K):
        routed = routed + gbuf[k * tm:(k + 1) * tm, :]
    o_ref[...] = _ln_rows(acc + routed, g_ref[...], b_ref[...])


def _final(dest8, ys, h1f, h1b, wsg, wsu, wsd, g, b, *, alpha):
    T, D = h1f.shape
    F = wsg.shape[1]
    tm = _tile(T, 128)
    nt = T // tm
    d8 = dest8.reshape(nt, tm, TOP_K).transpose(0, 2, 1).reshape(nt, 1, TOP_K * tm)
    row = pl.BlockSpec((tm, D), lambda i: (i, 0))
    vec = pl.BlockSpec((1, D), lambda i: (0, 0))
    return pl.pallas_call(
        functools.partial(_final_kernel, alpha=alpha), grid=(nt,),
        in_specs=[pl.BlockSpec((1, 1, TOP_K * tm), lambda i: (i, 0, 0), memory_space=pltpu.SMEM),
                  pl.BlockSpec(memory_space=pl.ANY), row, row,
                  pl.BlockSpec((D, F), lambda i: (0, 0)), pl.BlockSpec((D, F), lambda i: (0, 0)),
                  pl.BlockSpec((F, D), lambda i: (0, 0)), vec, vec],
        out_specs=row,
        out_shape=jax.ShapeDtypeStruct((T, D), F32),
        scratch_shapes=[pltpu.VMEM((TOP_K * tm, D), F32), pltpu.SemaphoreType.DMA((1,))],
        compiler_params=_params("arbitrary"), name="final",
    )(d8, ys, h1f, h1b, wsg, wsu, wsd, g.reshape(1, D), b.reshape(1, D))


def _rope_tables(positions, dh):
    rot = dh // ROT_FRACTION
    half = rot // 2
    inv_freq = jnp.power(ROPE_THETA, -jnp.arange(0, rot, 2, dtype=F32) / rot)
    ang = positions.astype(F32).reshape(-1, 1) * inv_freq
    cos, sin = jnp.cos(ang), jnp.sin(ang)
    T = ang.shape[0]
    cosf = jnp.concatenate([cos, cos, jnp.ones((T, dh - rot), F32)], axis=1)
    sinf = jnp.concatenate([-sin, sin, jnp.zeros((T, dh - rot), F32)], axis=1)
    return cosf, sinf, half


def _dispatch_plan(cw, sel):
    T, E = sel.shape
    R = EXPERT_ROWS
    n_blocks = -(-(T * TOP_K) // R) + E
    sel_i = sel.astype(I32)
    counts = jnp.sum(sel_i, axis=0)
    pos = jnp.cumsum(sel_i, axis=0) - sel_i
    padded = (counts + R - 1) // R * R
    pend = jnp.cumsum(padded)
    dest = (pend - padded)[None, :] + pos
    _, idx8 = lax.top_k(sel, TOP_K)
    dest8 = jnp.take_along_axis(dest, idx8, axis=1).astype(I32)
    w8 = jnp.take_along_axis(cw, idx8, axis=1)
    flat = dest8.reshape(-1)
    row_tok = jnp.zeros((n_blocks * R,), I32).at[flat].set(jnp.repeat(jnp.arange(T, dtype=I32), TOP_K))
    row_w = jnp.zeros((n_blocks * R,), F32).at[flat].set(w8.reshape(-1))
    blk_e = jnp.minimum(jnp.searchsorted(pend, jnp.arange(n_blocks, dtype=I32) * R, side='right'),
                        E - 1).astype(I32)
    n_used = (pend[-1:] // R).astype(I32)
    return blk_e, n_used, row_tok, row_w, dest8


def kernel(x, positions, ln_in_g, ln_in_b, w_in, b_gate, lam_q1, lam_k1, lam_q2, lam_k2, subln_g, conv_w, conv_b, w_rg_a, b_rg_a, w_rg_x, b_rg_x, lru_lambda, w_proj_attn, w_proj_rnn, w_out, ln1_g, ln1_b, w_router, router_bias, w_exp_gate, w_exp_up, w_exp_down, w_sh_gate, w_sh_up, w_sh_down, ln2_g, ln2_b):
    B, S, D = x.shape
    T = B * S
    depth = w_in.shape[0]
    alpha = (2 * depth) ** 0.25
    dh = lam_q1.shape[-1]
    att_v = w_proj_attn.shape[1]
    d_rnn = conv_w.shape[-1]
    att_qk = (w_in.shape[2] - att_v - 2 * d_rnn - 2 * D) // 2
    heads = att_v // (2 * dh)
    assert dh == LANES and att_qk == att_v
    cosf, sinf, half = _rope_tables(positions, dh)

    h0f, h0b = _ln_in(x.reshape(T, D), ln_in_g, ln_in_b)
    hf, hb = h0f, h0b
    for l in range(depth):
        lambda_init = 0.8 - 0.6 * math.exp(-0.3 * l)
        lam = (jnp.exp(jnp.sum(lam_q1[l] * lam_k1[l])) - jnp.exp(jnp.sum(lam_q2[l] * lam_k2[l]))
               + lambda_init).reshape(1).astype(F32)
        proj = _in_proj(hb, w_in[l].astype(BF16), cosf, sinf, att_qk=att_qk,
                        q_scale=dh ** -0.5 * LOG2E, half=half)
        attn = _attention(proj, lam, subln_g[l], B=B, S=S, heads=heads, dh=dh, att_qk=att_qk,
                          lambda_init=lambda_init)
        rnn = _rglru(proj, conv_w[l], conv_b[l], w_rg_a[l], b_rg_a[l], w_rg_x[l], b_rg_x[l], lru_lambda[l],
                     B=B, S=S, off_xr=2 * att_qk + att_v)
        mixed = _mix(attn, rnn, w_proj_attn[l].astype(BF16), w_proj_rnn[l].astype(BF16), proj, b_gate[l],
                     off_g=2 * att_qk + att_v + 2 * d_rnn)
        h1f, h1b = _out_ln(mixed, w_out[l].astype(BF16), hf, ln1_g[l], ln1_b[l], alpha=alpha)
        cw, sel = _router(h1f, w_router[l], router_bias[l])
        blk_e, n_used, row_tok, row_w, dest8 = _dispatch_plan(cw, sel)
        ys = _experts(h1f, blk_e, n_used, row_tok, row_w, w_exp_gate[l].astype(BF16),
                      w_exp_up[l].astype(BF16), w_exp_down[l].astype(BF16))
        out = _final(dest8, ys, h1f, h1b, w_sh_gate[l].astype(BF16), w_sh_up[l].astype(BF16),
                     w_sh_down[l].astype(BF16), ln2_g[l], ln2_b[l], alpha=alpha)
        hf = out
        hb = out.astype(BF16)
    return hf.reshape(B, S, D)
```

```python
import functools
import math

import jax
import jax.numpy as jnp
from jax import lax
from jax.experimental import pallas as pl
from jax.experimental.pallas import tpu as pltpu

F32 = jnp.float32
BF16 = jnp.bfloat16
I32 = jnp.int32
U32 = jnp.uint32

CHUNK = 64
ROPE_THETA = 500000.0
ROT_FRACTION = 4
LRU_C = 8.0
CONV_WIDTH = 4
N_GROUPS = 8
TOPK_GROUPS = 4
TOP_K = 8
ROUTED_SCALE = 2.5
LN_EPS = 1e-5
LOG2E = 1.4426950408889634

LANES = 128
SUBLANES = 8
VMEM_LIMIT_BYTES = 56 * 1024 * 1024
EXPERT_ROWS = 256
DMA_UNROLL = 8


def _tile(n, pref):
    t = min(n, pref)
    while n % t:
        t //= 2
    return t


def _params(*sem):
    return pltpu.CompilerParams(dimension_semantics=sem, vmem_limit_bytes=VMEM_LIMIT_BYTES)


def _ln_rows(x, g, b):
    mu = jnp.mean(x, -1, keepdims=True)
    xc = x - mu
    var = jnp.mean(xc * xc, -1, keepdims=True)
    return xc * lax.rsqrt(var + LN_EPS) * g + b


def _silu(x):
    return x * jax.nn.sigmoid(x)


def _pack_halves(y):
    half = y.shape[1] // 2
    bits = lax.bitcast_convert_type(y.astype(BF16).astype(F32), U32)
    return (bits[:, :half] >> 16) | (bits[:, half:] & jnp.uint32(0xFFFF0000))


def _unpack_halves(p):
    lo = lax.bitcast_convert_type(p << 16, F32)
    hi = lax.bitcast_convert_type(p & jnp.uint32(0xFFFF0000), F32)
    return lo, hi


def _ln_in_kernel(x_ref, g_ref, b_ref, yf_ref, yb_ref):
    y = _ln_rows(x_ref[...], g_ref[...], b_ref[...])
    yf_ref[...] = y
    yb_ref[...] = y.astype(BF16)


def _ln_in(x2, g, b):
    T, D = x2.shape
    tm = _tile(T, 256)
    row = pl.BlockSpec((tm, D), lambda i: (i, 0))
    vec = pl.BlockSpec((1, D), lambda i: (0, 0))
    return pl.pallas_call(
        _ln_in_kernel, grid=(T // tm,),
        in_specs=[row, vec, vec], out_specs=[row, row],
        out_shape=[jax.ShapeDtypeStruct((T, D), F32), jax.ShapeDtypeStruct((T, D), BF16)],
        compiler_params=_params("parallel"), name="ln_in",
    )(x2, g.reshape(1, D), b.reshape(1, D))


def _in_proj_kernel(a_ref, w_ref, cos_ref, sin_ref, o_ref, *, n_q, n_qk, q_scale, half):
    j = pl.program_id(1)
    acc = jnp.dot(a_ref[...], w_ref[...], preferred_element_type=F32)
    tn = acc.shape[1]

    @pl.when(j >= n_qk)
    def _():
        o_ref[...] = acc.astype(BF16)

    @pl.when(j < n_qk)
    def _():
        reps = tn // LANES
        cosf = jnp.tile(cos_ref[...], (1, reps))
        sinf = jnp.tile(sin_ref[...], (1, reps))
        lane = lax.broadcasted_iota(I32, acc.shape, 1) % LANES
        partner = jnp.where(lane < half, pltpu.roll(acc, tn - half, 1), pltpu.roll(acc, half, 1))
        r = acc * cosf + partner * sinf
        r = r * jnp.where(j < n_q, q_scale, 1.0)
        o_ref[...] = r.astype(BF16)


def _in_proj(h0b, w, cosf, sinf, *, att_qk, q_scale, half):
    T, K = h0b.shape
    N = w.shape[1]
    tm = _tile(T, 1024)
    tn = _tile(att_qk, 1024)
    assert N % tn == 0
    kern = functools.partial(_in_proj_kernel, n_q=att_qk // tn, n_qk=2 * att_qk // tn,
                             q_scale=q_scale, half=half)
    return pl.pallas_call(
        kern, grid=(T // tm, N // tn),
        in_specs=[pl.BlockSpec((tm, K), lambda i, j: (i, 0)),
                  pl.BlockSpec((K, tn), lambda i, j: (0, j)),
                  pl.BlockSpec((tm, LANES), lambda i, j: (i, 0)),
                  pl.BlockSpec((tm, LANES), lambda i, j: (i, 0))],
        out_specs=pl.BlockSpec((tm, tn), lambda i, j: (i, j)),
        out_shape=jax.ShapeDtypeStruct((T, N), BF16),
        compiler_params=_params("parallel", "arbitrary"), name="in_proj",
    )(h0b, w, cosf, sinf)


def _attn_kernel(lam_ref, q_ref, k_ref, v_ref, g_ref, o_ref, qt_s, vt_s, st_s, acc_s, *, dh, tk, out_scale):
    i = pl.program_id(2)
    tq = q_ref.shape[0]
    S = v_ref.shape[0]

    @pl.when(i == 0)
    def _():
        for c in range(S // tk):
            vt_s[c] = v_ref[c * tk:(c + 1) * tk, :].astype(F32).T.astype(BF16)

    qt_s[...] = q_ref[...].astype(F32).T.astype(BF16)

    def scores(c, m):
        r0 = pl.multiple_of(c * tk, tk)
        return jnp.dot(k_ref[pl.ds(r0, tk), m * dh:(m + 1) * dh], qt_s[m * dh:(m + 1) * dh, :],
                       preferred_element_type=F32)

    def phase1(c, mx):
        out = []
        for m in range(2):
            st = scores(c, m)
            st_s[c, m] = st
            out.append(jnp.maximum(mx[m], jnp.max(st, 0, keepdims=True)))
        return tuple(out)

    mx = lax.fori_loop(0, i, phase1, tuple(jnp.full((1, tq), -jnp.inf, F32) for _ in range(2)))
    visible = ((lax.broadcasted_iota(I32, (tk, tq), 0) // CHUNK)
               <= (lax.broadcasted_iota(I32, (tk, tq), 1) // CHUNK))
    mxs = []
    for m in range(2):
        st = jnp.where(visible, scores(i, m), -jnp.inf)
        st_s[i, m] = st
        mxs.append(jnp.maximum(mx[m], jnp.max(st, 0, keepdims=True)))

    acc_s[...] = jnp.zeros_like(acc_s)

    def phase2(c, l):
        vt = vt_s[c]
        out = []
        for m in range(2):
            p = jnp.exp2(st_s[c, m] - mxs[m])
            acc_s[m] += jnp.dot(vt, p.astype(BF16), preferred_element_type=F32)
            out.append(l[m] + jnp.sum(p, 0, keepdims=True))
        return tuple(out)

    l0, l1 = lax.fori_loop(0, i + 1, phase2, tuple(jnp.zeros((1, tq), F32) for _ in range(2)))
    ot = acc_s[0] * (1.0 / l0) - lam_ref[0] * (acc_s[1] * (1.0 / l1))
    ot = ot * lax.rsqrt(jnp.mean(ot * ot, 0, keepdims=True) + LN_EPS)
    o_ref[...] = (ot.T * (g_ref[...] * out_scale)).astype(BF16)


def _attention(proj, lam, subln_g, *, B, S, heads, dh, att_qk, lambda_init):
    T = B * S
    wv = 2 * dh
    tq = _tile(S, 512)
    tk = tq
    nq = S // tq
    kern = functools.partial(_attn_kernel, dh=dh, tk=tk, out_scale=1.0 - lambda_init)
    scratch = [pltpu.VMEM((wv, tq), BF16), pltpu.VMEM((S // tk, wv, tk), BF16),
               pltpu.VMEM((S // tk, 2, tk, tq), F32), pltpu.VMEM((2, wv, tq), F32)]
    return pl.pallas_call(
        kern, grid=(B, heads, nq),
        in_specs=[pl.BlockSpec(memory_space=pltpu.SMEM),
                  pl.BlockSpec((tq, wv), lambda b, h, i: (b * nq + i, h)),
                  pl.BlockSpec((S, wv), lambda b, h, i: (b, att_qk // wv + h)),
                  pl.BlockSpec((S, wv), lambda b, h, i: (b, 2 * att_qk // wv + h)),
                  pl.BlockSpec((1, wv), lambda b, h, i: (0, 0))],
        out_specs=pl.BlockSpec((tq, wv), lambda b, h, i: (b * nq + i, h)),
        out_shape=jax.ShapeDtypeStruct((T, heads * wv), BF16),
        scratch_shapes=scratch,
        compiler_params=_params("parallel", "parallel", "arbitrary"), name="attention",
    )(lam, proj, proj, proj, subln_g.reshape(1, wv))


def _gelu_tanh(x):
    return 0.5 * x * (1.0 + jnp.tanh(0.7978845608028654 * (x + 0.044715 * (x * x * x))))


def _rglru_kernel(xr_ref, yr_ref, cw_ref, cb_ref, wa_ref, ba_ref, wx_ref, bx_ref, lam_ref, o_ref,
                  a_s, u_s, xprev_s, h_s, *, n_heads, blk, scan_cols):
    @pl.when(pl.program_id(1) == 0)
    def _():
        xprev_s[...] = jnp.zeros_like(xprev_s)
        h_s[...] = jnp.zeros_like(h_s)

    tt, D = xr_ref.shape
    z = -lam_ref[...]
    sp = jnp.maximum(z, 0.0) + jnp.log1p(jnp.exp(-jnp.abs(z)))
    for hh in range(n_heads):
        sl = slice(hh * blk, (hh + 1) * blk)
        xe = jnp.concatenate([xprev_s[:, sl], xr_ref[:, sl].astype(F32)], axis=0)
        xc = cb_ref[:, sl]
        for j in range(CONV_WIDTH):
            o = SUBLANES - (CONV_WIDTH - 1) + j
            xc = xc + cw_ref[j:j + 1, sl] * xe[o:o + tt]
        xb = xc.astype(BF16)
        r = jax.nn.sigmoid(jnp.dot(xb, wa_ref[hh], preferred_element_type=F32) + ba_ref[:, sl])
        ig = jax.nn.sigmoid(jnp.dot(xb, wx_ref[hh], preferred_element_type=F32) + bx_ref[:, sl])
        log_a = (-LRU_C * r) * sp[:, sl]
        a = jnp.exp(log_a)
        a_s[:, sl] = a
        u_s[:, sl] = jnp.sqrt(-jnp.tanh(log_a) * (a * a + 1.0)) * ig * xc
    xprev_s[...] = xr_ref[tt - SUBLANES:tt, :].astype(F32)

    row = lax.broadcasted_iota(I32, (SUBLANES, scan_cols), 0)

    def scan_body(g, h):
        r0 = pl.multiple_of(g * SUBLANES, SUBLANES)
        outs = []
        for c in range(D // scan_cols):
            cs = slice(c * scan_cols, (c + 1) * scan_cols)
            a = a_s[pl.ds(r0, SUBLANES), cs]
            u = u_s[pl.ds(r0, SUBLANES), cs]
            for s in (1, 2, 4):
                valid = row >= s
                u = jnp.where(valid, a * pltpu.roll(u, s, 0) + u, u)
                a = jnp.where(valid, a * pltpu.roll(a, s, 0), a)
            hg = u + a * h[:, cs]
            u_s[pl.ds(r0, SUBLANES), cs] = hg
            outs.append(hg[SUBLANES - 1:SUBLANES, :])
        return jnp.concatenate(outs, axis=1)

    h_s[...] = lax.fori_loop(0, tt // SUBLANES, scan_body, h_s[...])
    o_ref[...] = (u_s[...] * _gelu_tanh(yr_ref[...].astype(F32))).astype(BF16)


def _rglru(proj, conv_w, conv_b, w_ra, b_ra, w_rx, b_rx, lru_lambda, *, B, S, off_xr):
    T = B * S
    n_heads, blk, _ = w_ra.shape
    D = n_heads * blk
    tt = _tile(S, 256)
    nt = S // tt
    assert off_xr % D == 0
    cx = off_xr // D
    vec = pl.BlockSpec((1, D), lambda b, t: (0, 0))
    wspec = pl.BlockSpec((n_heads, blk, blk), lambda b, t: (0, 0, 0))
    kern = functools.partial(_rglru_kernel, n_heads=n_heads, blk=blk, scan_cols=_tile(D, 512))
    return pl.pallas_call(
        kern, grid=(B, nt),
        in_specs=[pl.BlockSpec((tt, D), lambda b, t: (b * nt + t, cx)),
                  pl.BlockSpec((tt, D), lambda b, t: (b * nt + t, cx + 1)),
                  pl.BlockSpec((CONV_WIDTH, D), lambda b, t: (0, 0)), vec,
                  wspec, vec, wspec, vec, vec],
        out_specs=pl.BlockSpec((tt, D), lambda b, t: (b * nt + t, 0)),
        out_shape=jax.ShapeDtypeStruct((T, D), BF16),
        scratch_shapes=[pltpu.VMEM((tt, D), F32), pltpu.VMEM((tt, D), F32),
                        pltpu.VMEM((SUBLANES, D), F32), pltpu.VMEM((1, D), F32)],
        compiler_params=_params("parallel", "arbitrary"), name="rglru",
    )(proj, proj, conv_w, conv_b.reshape(1, D), w_ra.astype(BF16), b_ra.reshape(1, D),
      w_rx.astype(BF16), b_rx.reshape(1, D), lru_lambda.reshape(1, D))


def _mix_kernel(at_ref, rn_ref, wa_ref, wr_ref, g0_ref, g1_ref, b0_ref, b1_ref, o_ref):
    pa = jnp.dot(at_ref[...], wa_ref[...], preferred_element_type=F32)
    pr = jnp.dot(rn_ref[...], wr_ref[...], preferred_element_type=F32)
    s0 = jax.nn.sigmoid(g0_ref[...].astype(F32) + b0_ref[...])
    s1 = jax.nn.sigmoid(g1_ref[...].astype(F32) + b1_ref[...])
    o_ref[...] = (s0 * pa + s1 * pr).astype(BF16)


def _mix(attn, rnn, wa, wr, proj, b_gate, *, off_g):
    T, Ka = attn.shape
    Kr = rnn.shape[1]
    D = wa.shape[1]
    tm = _tile(T, 512)
    tn = _tile(D, 512)
    assert off_g % tn == 0
    cg = off_g // tn
    nn = D // tn
    return pl.pallas_call(
        _mix_kernel, grid=(T // tm, nn),
        in_specs=[pl.BlockSpec((tm, Ka), lambda i, j: (i, 0)),
                  pl.BlockSpec((tm, Kr), lambda i, j: (i, 0)),
                  pl.BlockSpec((Ka, tn), lambda i, j: (0, j)),
                  pl.BlockSpec((Kr, tn), lambda i, j: (0, j)),
                  pl.BlockSpec((tm, tn), lambda i, j: (i, cg + j)),
                  pl.BlockSpec((tm, tn), lambda i, j: (i, cg + nn + j)),
                  pl.BlockSpec((1, tn), lambda i, j: (0, j)),
                  pl.BlockSpec((1, tn), lambda i, j: (0, nn + j))],
        out_specs=pl.BlockSpec((tm, tn), lambda i, j: (i, j)),
        out_shape=jax.ShapeDtypeStruct((T, D), BF16),
        compiler_params=_params("parallel", "arbitrary"), name="mix",
    )(attn, rnn, wa, wr, proj, proj, b_gate.reshape(1, 2 * D), b_gate.reshape(1, 2 * D))


def _out_ln_kernel(a_ref, w_ref, h0_ref, g_ref, b_ref, hf_ref, hp_ref, acc_s, *, alpha):
    j = pl.program_id(1)
    n_n, _, tn = acc_s.shape
    acc_s[j] = alpha * h0_ref[...] + jnp.dot(a_ref[...], w_ref[...], preferred_element_type=F32)

    @pl.when(j == n_n - 1)
    def _():
        D = n_n * tn
        mu = sum(jnp.sum(acc_s[c], -1, keepdims=True) for c in range(n_n)) / D
        var = sum(jnp.sum(jnp.square(acc_s[c] - mu), -1, keepdims=True) for c in range(n_n)) / D
        rstd = lax.rsqrt(var + LN_EPS)
        hn = n_n // 2
        for c in range(hn):
            ys = []
            for cc in (c, c + hn):
                cs = slice(cc * tn, (cc + 1) * tn)
                y = (acc_s[cc] - mu) * rstd * g_ref[:, cs] + b_ref[:, cs]
                hf_ref[:, cs] = y
                ys.append(y)
            hp_ref[:, c * tn:(c + 1) * tn] = _pack_halves(jnp.concatenate(ys, axis=1))


def _out_ln(mixed, w, h0f, g, b, *, alpha):
    T, K = mixed.shape
    D = w.shape[1]
    tm = _tile(T, 512)
    tn = _tile(D // 2, 512)
    vec = pl.BlockSpec((1, D), lambda i, j: (0, 0))
    return pl.pallas_call(
        functools.partial(_out_ln_kernel, alpha=alpha), grid=(T // tm, D // tn),
        in_specs=[pl.BlockSpec((tm, K), lambda i, j: (i, 0), pipeline_mode=pl.Buffered(1)),
                  pl.BlockSpec((K, tn), lambda i, j: (0, j)),
                  pl.BlockSpec((tm, tn), lambda i, j: (i, j)), vec, vec],
        out_specs=[pl.BlockSpec((tm, D), lambda i, j: (i, 0)),
                   pl.BlockSpec((tm, D // 2), lambda i, j: (i, 0))],
        out_shape=[jax.ShapeDtypeStruct((T, D), F32), jax.ShapeDtypeStruct((T, D // 2), U32)],
        scratch_shapes=[pltpu.VMEM((D // tn, tm, tn), F32)],
        compiler_params=_params("parallel", "arbitrary"), name="out_ln",
    )(mixed, w, h0f, g.reshape(1, D), b.reshape(1, D))


def _seg_allreduce(x, lane, width, op):
    n = x.shape[1]
    s = 1
    while s < width:
        partner = jnp.where((lane & s) == 0, pltpu.roll(x, n - s, 1), pltpu.roll(x, s, 1))
        x = op(x, partner)
        s *= 2
    return x


def _router_kernel(h_ref, w_ref, bias_ref, cw_ref, sel_ref, idx_ref):
    h = h_ref[...]
    w = w_ref[...]
    h_hi = h.astype(BF16)
    h_lo = (h - h_hi.astype(F32)).astype(BF16)
    w_hi = w.astype(BF16)
    w_lo = (w - w_hi.astype(F32)).astype(BF16)
    logits = (jnp.dot(h_hi, w_hi, preferred_element_type=F32)
              + (jnp.dot(h_hi, w_lo, preferred_element_type=F32)
                 + jnp.dot(h_lo, w_hi, preferred_element_type=F32)))
    scores = jax.nn.sigmoid(logits)
    biased = scores + bias_ref[...]
    E = biased.shape[1]
    gw = E // N_GROUPS
    lane = lax.broadcasted_iota(I32, biased.shape, 1)
    neg = -jnp.inf

    m1 = _seg_allreduce(biased, lane, gw, jnp.maximum)
    is_m1 = biased == m1
    cnt = _seg_allreduce(is_m1.astype(F32), lane, gw, jnp.add)
    m2 = jnp.where(cnt >= 2.0, m1, _seg_allreduce(jnp.where(is_m1, neg, biased), lane, gw, jnp.maximum))
    gs = m1 + m2

    gi = lane // gw
    rank = jnp.zeros(biased.shape, I32)
    for k in range(1, N_GROUPS):
        other = pltpu.roll(gs, k * gw, 1)
        ogi = (gi - k) % N_GROUPS
        beats = (other > gs) | ((other == gs) & (ogi < gi))
        rank = rank + beats.astype(I32)
    cur = jnp.where(rank < TOPK_GROUPS, biased, neg)

    sel = jnp.zeros(biased.shape, jnp.bool_)
    idx = jnp.zeros(biased.shape, I32)
    for k in range(TOP_K):
        mx = jnp.max(cur, -1, keepdims=True)
        first = jnp.min(jnp.where(cur == mx, lane, E), -1, keepdims=True)
        pick = lane == first
        sel = sel | pick
        idx = jnp.where(lane == k, first, idx)
        cur = jnp.where(pick, neg, cur)
    wsel = jnp.where(sel, scores, 0.0)
    cw_ref[...] = wsel / jnp.sum(wsel, -1, keepdims=True) * ROUTED_SCALE
    sel_ref[...] = sel.astype(I32)
    idx_ref[...] = idx


def _router(h1f, w_router, router_bias):
    T, D = h1f.shape
    E = w_router.shape[1]
    assert E == LANES and E % N_GROUPS == 0
    tm = _tile(T, 512)
    row = pl.BlockSpec((tm, E), lambda i: (i, 0))
    return pl.pallas_call(
        _router_kernel, grid=(T // tm,),
        in_specs=[pl.BlockSpec((tm, D), lambda i: (i, 0)),
                  pl.BlockSpec((D, E), lambda i: (0, 0)),
                  pl.BlockSpec((1, E), lambda i: (0, 0))],
        out_specs=[row, row, row],
        out_shape=[jax.ShapeDtypeStruct((T, E), F32), jax.ShapeDtypeStruct((T, E), I32),
                   jax.ShapeDtypeStruct((T, E), I32)],
        compiler_params=_params("parallel"), name="router",
    )(h1f, w_router, router_bias.reshape(1, E))


def _dispatch_kernel(fill_start_ref, fill_n_ref, d8_ref, hp_ref, xs_hbm, zero_s, sem, zsem):
    tm = hp_ref.shape[0]
    E = fill_n_ref.shape[0]

    def row_copy(r, k):
        return pltpu.make_async_copy(hp_ref.at[pl.ds(r, 1), :],
                                     xs_hbm.at[pl.ds(d8_ref[0, 0, r * TOP_K + k], 1), :], sem.at[0])

    def issue(r, c):
        for k in range(TOP_K):
            row_copy(r, k).start()
        return c
    lax.fori_loop(0, tm, issue, 0)

    @pl.when(pl.program_id(0) == 0)
    def _():
        zero_s[...] = jnp.zeros_like(zero_s)

        def zcopy(row):
            return pltpu.make_async_copy(zero_s, xs_hbm.at[pl.ds(row, 1), :], zsem.at[0])

        def per_expert(e, c):
            s0 = fill_start_ref[e]
            n = fill_n_ref[e]

            def zi(r, c2):
                zcopy(s0 + r).start()
                return c2
            lax.fori_loop(0, n, zi, 0)

            def zw(r, c2):
                zcopy(s0 + r).wait()
                return c2
            lax.fori_loop(0, n, zw, 0)
            return c
        lax.fori_loop(0, E, per_expert, 0)

    def wait(r, c):
        for k in range(TOP_K):
            row_copy(r, k).wait()
        return c
    lax.fori_loop(0, tm, wait, 0)


def _dispatch(h1p, dest8, fill_start, fill_n, n_rows):
    T, W = h1p.shape
    tm = _tile(T, 512)
    nt = T // tm
    d8 = dest8.reshape(nt, 1, tm * TOP_K)
    grid_spec = pltpu.PrefetchScalarGridSpec(
        num_scalar_prefetch=2, grid=(nt,),
        in_specs=[pl.BlockSpec((1, 1, tm * TOP_K), lambda i, fs, fn: (i, 0, 0), memory_space=pltpu.SMEM),
                  pl.BlockSpec((tm, W), lambda i, fs, fn: (i, 0))],
        out_specs=pl.BlockSpec(memory_space=pl.ANY),
        scratch_shapes=[pltpu.VMEM((1, W), U32), pltpu.SemaphoreType.DMA((1,)), pltpu.SemaphoreType.DMA((1,))])
    return pl.pallas_call(
        _dispatch_kernel, grid_spec=grid_spec,
        out_shape=jax.ShapeDtypeStruct((n_rows, W), U32),
        compiler_params=_params("arbitrary"), name="dispatch",
    )(fill_start, fill_n, d8, h1p)


def _experts_kernel(blk_e_ref, n_used_ref, xs_ref, wg_ref, wu_ref, wd_ref, ys_ref):
    i = pl.program_id(0)

    @pl.when(i < n_used_ref[0])
    def _():
        lo, hi = _unpack_halves(xs_ref[...])
        x = jnp.concatenate([lo.astype(BF16), hi.astype(BF16)], axis=1)
        g = jnp.dot(x, wg_ref[0], preferred_element_type=F32)
        u = jnp.dot(x, wu_ref[0], preferred_element_type=F32)
        hb = (_silu(g) * u).astype(BF16)
        ys_ref[...] = _pack_halves(jnp.dot(hb, wd_ref[0], preferred_element_type=F32))

    @pl.when(i >= n_used_ref[0])
    def _():
        ys_ref[...] = jnp.zeros_like(ys_ref)


def _experts(xs, blk_e, n_used, wg, wu, wd):
    n_rows, W = xs.shape
    E, D, F = wg.shape
    R = EXPERT_ROWS
    n_blocks = n_rows // R
    xrow = lambda i, be, nu: (jnp.minimum(i, jnp.maximum(nu[0] - 1, 0)), 0)
    grid_spec = pltpu.PrefetchScalarGridSpec(
        num_scalar_prefetch=2, grid=(n_blocks,),
        in_specs=[pl.BlockSpec((R, W), xrow),
                  pl.BlockSpec((1, D, F), lambda i, be, nu: (be[i], 0, 0)),
                  pl.BlockSpec((1, D, F), lambda i, be, nu: (be[i], 0, 0)),
                  pl.BlockSpec((1, F, D), lambda i, be, nu: (be[i], 0, 0))],
        out_specs=pl.BlockSpec((R, W), lambda i, be, nu: (i, 0)))
    return pl.pallas_call(
        _experts_kernel, grid_spec=grid_spec,
        out_shape=jax.ShapeDtypeStruct((n_rows, W), U32),
        compiler_params=_params("arbitrary"), name="experts",
    )(blk_e, n_used, xs, wg, wu, wd)


def _final_kernel(d8c_ref, d8n_ref, ys_hbm, hf_ref, hp_ref, w8_ref, wsg_ref, wsu_ref, wsd_ref, g_ref, b_ref,
                  o_ref, gbuf, sem, *, alpha):
    i = pl.program_id(0)
    nt = pl.num_programs(0)
    tm = hf_ref.shape[0]
    n = tm * TOP_K
    slot = i % 2

    def gather(d8_ref, s):
        def issue(r, c):
            pltpu.make_async_copy(ys_hbm.at[pl.ds(d8_ref[0, 0, r], 1), :],
                                  gbuf.at[s, pl.ds(r, 1), :], sem.at[s]).start()
            return c
        lax.fori_loop(0, n, issue, 0, unroll=DMA_UNROLL)

    @pl.when(i == 0)
    def _():
        gather(d8c_ref, 0)

    for s in range(2):
        @pl.when(jnp.logical_and(i + 1 < nt, slot == 1 - s))
        def _():
            gather(d8n_ref, s)

    lo, hi = _unpack_halves(hp_ref[...])
    x = jnp.concatenate([lo.astype(BF16), hi.astype(BF16)], axis=1)
    hs = (_silu(jnp.dot(x, wsg_ref[...], preferred_element_type=F32))
          * jnp.dot(x, wsu_ref[...], preferred_element_type=F32)).astype(BF16)
    acc = alpha * hf_ref[...] + jnp.dot(hs, wsd_ref[...], preferred_element_type=F32)

    def wait(r, c):
        pltpu.make_async_copy(ys_hbm.at[pl.ds(0, 1), :], gbuf.at[slot, pl.ds(r, 1), :], sem.at[slot]).wait()
        return c
    lax.fori_loop(0, n, wait, 0, unroll=DMA_UNROLL)

    r_lo = r_hi = None
    for k in range(TOP_K):
        lo, hi = _unpack_halves(gbuf[slot, k * tm:(k + 1) * tm, :])
        wk = w8_ref[:, k:k + 1]
        r_lo = lo * wk if r_lo is None else r_lo + lo * wk
        r_hi = hi * wk if r_hi is None else r_hi + hi * wk
    o_ref[...] = _ln_rows(acc + jnp.concatenate([r_lo, r_hi], axis=1), g_ref[...], b_ref[...])


def _final(dest8, w8, ys, h1f, h1p, wsg, wsu, wsd, g, b, *, alpha):
    T, D = h1f.shape
    W = ys.shape[1]
    F = wsg.shape[1]
    tm = _tile(T, 128)
    nt = T // tm
    d8 = dest8.reshape(nt, tm, TOP_K).transpose(0, 2, 1).reshape(nt, 1, TOP_K * tm)
    row = lambda w: pl.BlockSpec((tm, w), lambda i: (i, 0))
    vec = pl.BlockSpec((1, D), lambda i: (0, 0))
    kern = functools.partial(_final_kernel, alpha=alpha)
    return pl.pallas_call(
        kern, grid=(nt,),
        in_specs=[pl.BlockSpec((1, 1, TOP_K * tm), lambda i: (i, 0, 0), memory_space=pltpu.SMEM),
                  pl.BlockSpec((1, 1, TOP_K * tm), lambda i: (jnp.minimum(i + 1, nt - 1), 0, 0),
                               memory_space=pltpu.SMEM),
                  pl.BlockSpec(memory_space=pl.ANY), row(D), row(W), row(TOP_K),
                  pl.BlockSpec((D, F), lambda i: (0, 0)), pl.BlockSpec((D, F), lambda i: (0, 0)),
                  pl.BlockSpec((F, D), lambda i: (0, 0)), vec, vec],
        out_specs=row(D),
        out_shape=jax.ShapeDtypeStruct((T, D), F32),
        scratch_shapes=[pltpu.VMEM((2, TOP_K * tm, W), U32), pltpu.SemaphoreType.DMA((2,))],
        compiler_params=_params("arbitrary"), name="final",
    )(d8, d8, ys, h1f, h1p, w8, wsg, wsu, wsd, g.reshape(1, D), b.reshape(1, D))


def _rope_tables(positions, dh):
    rot = dh // ROT_FRACTION
    half = rot // 2
    inv_freq = jnp.power(ROPE_THETA, -jnp.arange(0, rot, 2, dtype=F32) / rot)
    ang = positions.astype(F32).reshape(-1, 1) * inv_freq
    cos, sin = jnp.cos(ang), jnp.sin(ang)
    T = ang.shape[0]
    cosf = jnp.concatenate([cos, cos, jnp.ones((T, dh - rot), F32)], axis=1)
    sinf = jnp.concatenate([-sin, sin, jnp.zeros((T, dh - rot), F32)], axis=1)
    return cosf, sinf, half


def _dispatch_plan(cw, sel, idx):
    T, E = sel.shape
    R = EXPERT_ROWS
    n_blocks = -(-(T * TOP_K) // R) + E
    counts = jnp.sum(sel, axis=0)
    pos = jnp.cumsum(sel, axis=0) - sel
    padded = (counts + R - 1) // R * R
    pend = jnp.cumsum(padded)
    pstart = pend - padded
    idx8 = idx[:, :TOP_K]
    dest8 = (jnp.take_along_axis(pos, idx8, axis=1) + pstart[idx8]).astype(I32)
    w8 = jnp.take_along_axis(cw, idx8, axis=1)
    blk_e = jnp.minimum(jnp.searchsorted(pend, jnp.arange(n_blocks, dtype=I32) * R, side='right'),
                        E - 1).astype(I32)
    n_used = (pend[-1:] // R).astype(I32)
    fill_start = (pstart + counts).astype(I32)
    fill_n = (padded - counts).astype(I32)
    return blk_e, n_used, dest8, w8, fill_start, fill_n, n_blocks * R


def kernel(x, positions, ln_in_g, ln_in_b, w_in, b_gate, lam_q1, lam_k1, lam_q2, lam_k2, subln_g, conv_w, conv_b, w_rg_a, b_rg_a, w_rg_x, b_rg_x, lru_lambda, w_proj_attn, w_proj_rnn, w_out, ln1_g, ln1_b, w_router, router_bias, w_exp_gate, w_exp_up, w_exp_down, w_sh_gate, w_sh_up, w_sh_down, ln2_g, ln2_b):
    B, S, D = x.shape
    T = B * S
    depth = w_in.shape[0]
    alpha = (2 * depth) ** 0.25
    dh = lam_q1.shape[-1]
    att_v = w_proj_attn.shape[1]
    d_rnn = conv_w.shape[-1]
    att_qk = (w_in.shape[2] - att_v - 2 * d_rnn - 2 * D) // 2
    heads = att_v // (2 * dh)
    assert dh == LANES and att_qk == att_v
    cosf, sinf, half = _rope_tables(positions, dh)

    hf, hb = _ln_in(x.reshape(T, D), ln_in_g, ln_in_b)
    for l in range(depth):
        lambda_init = 0.8 - 0.6 * math.exp(-0.3 * l)
        lam = (jnp.exp(jnp.sum(lam_q1[l] * lam_k1[l])) - jnp.exp(jnp.sum(lam_q2[l] * lam_k2[l]))
               + lambda_init).reshape(1).astype(F32)
        proj = _in_proj(hb, w_in[l].astype(BF16), cosf, sinf, att_qk=att_qk,
                        q_scale=dh ** -0.5 * LOG2E, half=half)
        attn = _attention(proj, lam, subln_g[l], B=B, S=S, heads=heads, dh=dh, att_qk=att_qk,
                          lambda_init=lambda_init)
        rnn = _rglru(proj, conv_w[l], conv_b[l], w_rg_a[l], b_rg_a[l], w_rg_x[l], b_rg_x[l], lru_lambda[l],
                     B=B, S=S, off_xr=2 * att_qk + att_v)
        mixed = _mix(attn, rnn, w_proj_attn[l].astype(BF16), w_proj_rnn[l].astype(BF16), proj, b_gate[l],
                     off_g=2 * att_qk + att_v + 2 * d_rnn)
        h1f, h1p = _out_ln(mixed, w_out[l].astype(BF16), hf, ln1_g[l], ln1_b[l], alpha=alpha)
        cw, sel, idx = _router(h1f, w_router[l], router_bias[l])
        blk_e, n_used, dest8, w8, fill_start, fill_n, n_rows = _dispatch_plan(cw, sel, idx)
        xs = _dispatch(h1p, dest8, fill_start, fill_n, n_rows)
        ys = _experts(xs, blk_e, n_used, w_exp_gate[l].astype(BF16), w_exp_up[l].astype(BF16),
                      w_exp_down[l].astype(BF16))
        hf = _final(dest8, w8, ys, h1f, h1p, w_sh_gate[l].astype(BF16), w_sh_up[l].astype(BF16),
                    w_sh_down[l].astype(BF16), ln2_g[l], ln2_b[l], alpha=alpha)
        if l + 1 < depth:
            hb = hf.astype(BF16)
    return hf.reshape(B, S, D)
```

```python
import functools
import math

import jax
import jax.numpy as jnp
from jax import lax
from jax.experimental import pallas as pl
from jax.experimental.pallas import tpu as pltpu

F32 = jnp.float32
BF16 = jnp.bfloat16
I32 = jnp.int32
U32 = jnp.uint32

CHUNK = 64
ROPE_THETA = 500000.0
ROT_FRACTION = 4
LRU_C = 8.0
CONV_WIDTH = 4
N_GROUPS = 8
TOPK_GROUPS = 4
TOP_K = 8
ROUTED_SCALE = 2.5
LN_EPS = 1e-5
LOG2E = 1.4426950408889634

LANES = 128
SUBLANES = 8
VMEM_LIMIT_BYTES = 56 * 1024 * 1024
EXPERT_ROWS = 256
DMA_UNROLL = 8


def _tile(n, pref):
    t = min(n, pref)
    while n % t:
        t //= 2
    return t


def _params(*sem):
    return pltpu.CompilerParams(dimension_semantics=sem, vmem_limit_bytes=VMEM_LIMIT_BYTES)


def _ln_rows(x, g, b):
    mu = jnp.mean(x, -1, keepdims=True)
    xc = x - mu
    var = jnp.mean(xc * xc, -1, keepdims=True)
    return xc * lax.rsqrt(var + LN_EPS) * g + b


def _silu(x):
    return x * jax.nn.sigmoid(x)


def _pack_halves(y):
    half = y.shape[1] // 2
    bits = lax.bitcast_convert_type(y.astype(BF16).astype(F32), U32)
    return (bits[:, :half] >> 16) | (bits[:, half:] & jnp.uint32(0xFFFF0000))


def _unpack_halves(p):
    lo = lax.bitcast_convert_type(p << 16, F32)
    hi = lax.bitcast_convert_type(p & jnp.uint32(0xFFFF0000), F32)
    return lo, hi


def _ln_in_kernel(x_ref, g_ref, b_ref, yf_ref, yb_ref):
    y = _ln_rows(x_ref[...], g_ref[...], b_ref[...])
    yf_ref[...] = y
    yb_ref[...] = y.astype(BF16)


def _ln_in(x2, g, b):
    T, D = x2.shape
    tm = _tile(T, 256)
    row = pl.BlockSpec((tm, D), lambda i: (i, 0))
    vec = pl.BlockSpec((1, D), lambda i: (0, 0))
    return pl.pallas_call(
        _ln_in_kernel, grid=(T // tm,),
        in_specs=[row, vec, vec], out_specs=[row, row],
        out_shape=[jax.ShapeDtypeStruct((T, D), F32), jax.ShapeDtypeStruct((T, D), BF16)],
        compiler_params=_params("parallel"), name="ln_in",
    )(x2, g.reshape(1, D), b.reshape(1, D))


def _in_proj_kernel(a_ref, w_ref, cos_ref, sin_ref, o_ref, *, n_q, n_qk, q_scale, half, n_chunks):
    j = pl.program_id(1)
    tm, tn = o_ref.shape
    cn = tn // n_chunks
    scale = jnp.where(j < n_q, q_scale, 1.0)
    cosf = jnp.where(j < n_qk, cos_ref[...], 1.0) * scale
    sinf = jnp.where(j < n_qk, sin_ref[...], 0.0) * scale
    low = lax.broadcasted_iota(I32, (tm, LANES), 1) < half
    a = a_ref[...]
    for c in range(n_chunks):
        acc = jnp.dot(a, w_ref[:, c * cn:(c + 1) * cn], preferred_element_type=F32)
        for g in range(cn // LANES):
            x = acc[:, g * LANES:(g + 1) * LANES]
            partner = jnp.where(low, pltpu.roll(x, LANES - half, 1), pltpu.roll(x, half, 1))
            col = c * cn + g * LANES
            o_ref[:, col:col + LANES] = (x * cosf + partner * sinf).astype(BF16)


def _in_proj(h0b, w, cosf, sinf, *, att_qk, q_scale, half):
    T, K = h0b.shape
    N = w.shape[1]
    tm = _tile(T, 1024)
    tn = _tile(att_qk, 1024)
    assert N % tn == 0
    kern = functools.partial(_in_proj_kernel, n_q=att_qk // tn, n_qk=2 * att_qk // tn,
                             q_scale=q_scale, half=half, n_chunks=max(1, tn // 512))
    return pl.pallas_call(
        kern, grid=(T // tm, N // tn),
        in_specs=[pl.BlockSpec((tm, K), lambda i, j: (i, 0)),
                  pl.BlockSpec((K, tn), lambda i, j: (0, j)),
                  pl.BlockSpec((tm, LANES), lambda i, j: (i, 0)),
                  pl.BlockSpec((tm, LANES), lambda i, j: (i, 0))],
        out_specs=pl.BlockSpec((tm, tn), lambda i, j: (i, j)),
        out_shape=jax.ShapeDtypeStruct((T, N), BF16),
        compiler_params=_params("parallel", "arbitrary"), name="in_proj",
    )(h0b, w, cosf, sinf)


def _attn_kernel(lam_ref, q_ref, k_ref, v_ref, g_ref, o_ref, qt_s, vt_s, st_s, acc_s, *, dh, tk, out_scale):
    i = pl.program_id(2)
    tq = q_ref.shape[0]
    S = v_ref.shape[0]

    @pl.when(i == 0)
    def _():
        for c in range(S // tk):
            vt_s[c] = v_ref[c * tk:(c + 1) * tk, :].astype(F32).T.astype(BF16)

    qt_s[...] = q_ref[...].astype(F32).T.astype(BF16)

    def scores(c, m):
        r0 = pl.multiple_of(c * tk, tk)
        return jnp.dot(k_ref[pl.ds(r0, tk), m * dh:(m + 1) * dh], qt_s[m * dh:(m + 1) * dh, :],
                       preferred_element_type=F32)

    def phase1(c, mx):
        out = []
        for m in range(2):
            st = scores(c, m)
            st_s[c, m] = st
            out.append(jnp.maximum(mx[m], jnp.max(st, 0, keepdims=True)))
        return tuple(out)

    mx = lax.fori_loop(0, i, phase1, tuple(jnp.full((1, tq), -jnp.inf, F32) for _ in range(2)))
    visible = ((lax.broadcasted_iota(I32, (tk, tq), 0) // CHUNK)
               <= (lax.broadcasted_iota(I32, (tk, tq), 1) // CHUNK))
    mxs = []
    for m in range(2):
        st = jnp.where(visible, scores(i, m), -jnp.inf)
        st_s[i, m] = st
        mxs.append(jnp.maximum(mx[m], jnp.max(st, 0, keepdims=True)))

    acc_s[...] = jnp.zeros_like(acc_s)

    def phase2(c, l):
        vt = vt_s[c]
        out = []
        for m in range(2):
            p = jnp.exp2(st_s[c, m] - mxs[m])
            acc_s[m] += jnp.dot(vt, p.astype(BF16), preferred_element_type=F32)
            out.append(l[m] + jnp.sum(p, 0, keepdims=True))
        return tuple(out)

    l0, l1 = lax.fori_loop(0, i + 1, phase2, tuple(jnp.zeros((1, tq), F32) for _ in range(2)))
    ot = acc_s[0] * (1.0 / l0) - lam_ref[0] * (acc_s[1] * (1.0 / l1))
    ot = ot * lax.rsqrt(jnp.mean(ot * ot, 0, keepdims=True) + LN_EPS)
    o_ref[...] = (ot.T * (g_ref[...] * out_scale)).astype(BF16)


def _attention(proj, lam, subln_g, *, B, S, heads, dh, att_qk, lambda_init):
    T = B * S
    wv = 2 * dh
    tq = _tile(S, 512)
    tk = tq
    nq = S // tq
    kern = functools.partial(_attn_kernel, dh=dh, tk=tk, out_scale=1.0 - lambda_init)
    scratch = [pltpu.VMEM((wv, tq), BF16), pltpu.VMEM((S // tk, wv, tk), BF16),
               pltpu.VMEM((S // tk, 2, tk, tq), F32), pltpu.VMEM((2, wv, tq), F32)]
    return pl.pallas_call(
        kern, grid=(B, heads, nq),
        in_specs=[pl.BlockSpec(memory_space=pltpu.SMEM),
                  pl.BlockSpec((tq, wv), lambda b, h, i: (b * nq + i, h)),
                  pl.BlockSpec((S, wv), lambda b, h, i: (b, att_qk // wv + h)),
                  pl.BlockSpec((S, wv), lambda b, h, i: (b, 2 * att_qk // wv + h)),
                  pl.BlockSpec((1, wv), lambda b, h, i: (0, 0))],
        out_specs=pl.BlockSpec((tq, wv), lambda b, h, i: (b * nq + i, h)),
        out_shape=jax.ShapeDtypeStruct((T, heads * wv), BF16),
        scratch_shapes=scratch,
        compiler_params=_params("parallel", "parallel", "arbitrary"), name="attention",
    )(lam, proj, proj, proj, subln_g.reshape(1, wv))


def _gelu_tanh(x):
    return 0.5 * x * (1.0 + jnp.tanh(0.7978845608028654 * (x + 0.044715 * (x * x * x))))


def _rglru_kernel(xr_ref, yr_ref, cw_ref, cb_ref, wa_ref, ba_ref, wx_ref, bx_ref, lam_ref, o_ref,
                  a_s, u_s, xprev_s, h_s, *, n_heads, blk, scan_cols):
    @pl.when(pl.program_id(1) == 0)
    def _():
        xprev_s[...] = jnp.zeros_like(xprev_s)
        h_s[...] = jnp.zeros_like(h_s)

    tt, D = xr_ref.shape
    z = -lam_ref[...]
    sp = jnp.maximum(z, 0.0) + jnp.log1p(jnp.exp(-jnp.abs(z)))
    for hh in range(n_heads):
        sl = slice(hh * blk, (hh + 1) * blk)
        xe = jnp.concatenate([xprev_s[:, sl], xr_ref[:, sl].astype(F32)], axis=0)
        xc = cb_ref[:, sl]
        for j in range(CONV_WIDTH):
            o = SUBLANES - (CONV_WIDTH - 1) + j
            xc = xc + cw_ref[j:j + 1, sl] * xe[o:o + tt]
        xb = xc.astype(BF16)
        r = jax.nn.sigmoid(jnp.dot(xb, wa_ref[hh], preferred_element_type=F32) + ba_ref[:, sl])
        ig = jax.nn.sigmoid(jnp.dot(xb, wx_ref[hh], preferred_element_type=F32) + bx_ref[:, sl])
        log_a = (-LRU_C * r) * sp[:, sl]
        a = jnp.exp(log_a)
        a_s[:, sl] = a
        u_s[:, sl] = jnp.sqrt(-jnp.tanh(log_a) * (a * a + 1.0)) * ig * xc
    xprev_s[...] = xr_ref[tt - SUBLANES:tt, :].astype(F32)

    row = lax.broadcasted_iota(I32, (SUBLANES, scan_cols), 0)

    def scan_body(g, h):
        r0 = pl.multiple_of(g * SUBLANES, SUBLANES)
        outs = []
        for c in range(D // scan_cols):
            cs = slice(c * scan_cols, (c + 1) * scan_cols)
            a = a_s[pl.ds(r0, SUBLANES), cs]
            u = u_s[pl.ds(r0, SUBLANES), cs]
            for s in (1, 2, 4):
                valid = row >= s
                u = jnp.where(valid, a * pltpu.roll(u, s, 0) + u, u)
                a = jnp.where(valid, a * pltpu.roll(a, s, 0), a)
            hg = u + a * h[:, cs]
            u_s[pl.ds(r0, SUBLANES), cs] = hg
            outs.append(hg[SUBLANES - 1:SUBLANES, :])
        return jnp.concatenate(outs, axis=1)

    h_s[...] = lax.fori_loop(0, tt // SUBLANES, scan_body, h_s[...])
    o_ref[...] = (u_s[...] * _gelu_tanh(yr_ref[...].astype(F32))).astype(BF16)


def _rglru(proj, conv_w, conv_b, w_ra, b_ra, w_rx, b_rx, lru_lambda, *, B, S, off_xr):
    T = B * S
    n_heads, blk, _ = w_ra.shape
    D = n_heads * blk
    tt = _tile(S, 256)
    nt = S // tt
    assert off_xr % D == 0
    cx = off_xr // D
    vec = pl.BlockSpec((1, D), lambda b, t: (0, 0))
    wspec = pl.BlockSpec((n_heads, blk, blk), lambda b, t: (0, 0, 0))
    kern = functools.partial(_rglru_kernel, n_heads=n_heads, blk=blk, scan_cols=_tile(D, 512))
    return pl.pallas_call(
        kern, grid=(B, nt),
        in_specs=[pl.BlockSpec((tt, D), lambda b, t: (b * nt + t, cx)),
                  pl.BlockSpec((tt, D), lambda b, t: (b * nt + t, cx + 1)),
                  pl.BlockSpec((CONV_WIDTH, D), lambda b, t: (0, 0)), vec,
                  wspec, vec, wspec, vec, vec],
        out_specs=pl.BlockSpec((tt, D), lambda b, t: (b * nt + t, 0)),
        out_shape=jax.ShapeDtypeStruct((T, D), BF16),
        scratch_shapes=[pltpu.VMEM((tt, D), F32), pltpu.VMEM((tt, D), F32),
                        pltpu.VMEM((SUBLANES, D), F32), pltpu.VMEM((1, D), F32)],
        compiler_params=_params("parallel", "arbitrary"), name="rglru",
    )(proj, proj, conv_w, conv_b.reshape(1, D), w_ra.astype(BF16), b_ra.reshape(1, D),
      w_rx.astype(BF16), b_rx.reshape(1, D), lru_lambda.reshape(1, D))


def _mix_kernel(at_ref, rn_ref, wa_ref, wr_ref, g0_ref, g1_ref, b0_ref, b1_ref, o_ref):
    pa = jnp.dot(at_ref[...], wa_ref[...], preferred_element_type=F32)
    pr = jnp.dot(rn_ref[...], wr_ref[...], preferred_element_type=F32)
    s0 = jax.nn.sigmoid(g0_ref[...].astype(F32) + b0_ref[...])
    s1 = jax.nn.sigmoid(g1_ref[...].astype(F32) + b1_ref[...])
    o_ref[...] = (s0 * pa + s1 * pr).astype(BF16)


def _mix(attn, rnn, wa, wr, proj, b_gate, *, off_g):
    T, Ka = attn.shape
    Kr = rnn.shape[1]
    D = wa.shape[1]
    tm = _tile(T, 512)
    tn = _tile(D, 512)
    assert off_g % tn == 0
    cg = off_g // tn
    nn = D // tn
    return pl.pallas_call(
        _mix_kernel, grid=(T // tm, nn),
        in_specs=[pl.BlockSpec((tm, Ka), lambda i, j: (i, 0)),
                  pl.BlockSpec((tm, Kr), lambda i, j: (i, 0)),
                  pl.BlockSpec((Ka, tn), lambda i, j: (0, j)),
                  pl.BlockSpec((Kr, tn), lambda i, j: (0, j)),
                  pl.BlockSpec((tm, tn), lambda i, j: (i, cg + j)),
                  pl.BlockSpec((tm, tn), lambda i, j: (i, cg + nn + j)),
                  pl.BlockSpec((1, tn), lambda i, j: (0, j)),
                  pl.BlockSpec((1, tn), lambda i, j: (0, nn + j))],
        out_specs=pl.BlockSpec((tm, tn), lambda i, j: (i, j)),
        out_shape=jax.ShapeDtypeStruct((T, D), BF16),
        compiler_params=_params("parallel", "arbitrary"), name="mix",
    )(attn, rnn, wa, wr, proj, proj, b_gate.reshape(1, 2 * D), b_gate.reshape(1, 2 * D))


def _out_ln_kernel(a_ref, w_ref, h0_ref, g_ref, b_ref, hf_ref, hp_ref, acc_s, *, alpha):
    j = pl.program_id(1)
    n_n, _, tn = acc_s.shape
    acc_s[j] = alpha * h0_ref[...] + jnp.dot(a_ref[...], w_ref[...], preferred_element_type=F32)

    @pl.when(j == n_n - 1)
    def _():
        D = n_n * tn
        mu = sum(jnp.sum(acc_s[c], -1, keepdims=True) for c in range(n_n)) / D
        var = sum(jnp.sum(jnp.square(acc_s[c] - mu), -1, keepdims=True) for c in range(n_n)) / D
        rstd = lax.rsqrt(var + LN_EPS)
        hn = n_n // 2
        for c in range(hn):
            ys = []
            for cc in (c, c + hn):
                cs = slice(cc * tn, (cc + 1) * tn)
                y = (acc_s[cc] - mu) * rstd * g_ref[:, cs] + b_ref[:, cs]
                hf_ref[:, cs] = y
                ys.append(y)
            hp_ref[:, c * tn:(c + 1) * tn] = _pack_halves(jnp.concatenate(ys, axis=1))


def _out_ln(mixed, w, h0f, g, b, *, alpha):
    T, K = mixed.shape
    D = w.shape[1]
    tm = _tile(T, 512)
    tn = _tile(D // 2, 512)
    vec = pl.BlockSpec((1, D), lambda i, j: (0, 0))
    return pl.pallas_call(
        functools.partial(_out_ln_kernel, alpha=alpha), grid=(T // tm, D // tn),
        in_specs=[pl.BlockSpec((tm, K), lambda i, j: (i, 0), pipeline_mode=pl.Buffered(1)),
                  pl.BlockSpec((K, tn), lambda i, j: (0, j)),
                  pl.BlockSpec((tm, tn), lambda i, j: (i, j)), vec, vec],
        out_specs=[pl.BlockSpec((tm, D), lambda i, j: (i, 0)),
                   pl.BlockSpec((tm, D // 2), lambda i, j: (i, 0))],
        out_shape=[jax.ShapeDtypeStruct((T, D), F32), jax.ShapeDtypeStruct((T, D // 2), U32)],
        scratch_shapes=[pltpu.VMEM((D // tn, tm, tn), F32)],
        compiler_params=_params("parallel", "arbitrary"), name="out_ln",
    )(mixed, w, h0f, g.reshape(1, D), b.reshape(1, D))


def _seg_allreduce(x, lane, width, op):
    n = x.shape[1]
    s = 1
    while s < width:
        partner = jnp.where((lane & s) == 0, pltpu.roll(x, n - s, 1), pltpu.roll(x, s, 1))
        x = op(x, partner)
        s *= 2
    return x


def _router_kernel(h_ref, w_ref, bias_ref, w8_ref, idx_ref, pos_ref, cnt_ref, tri_s, run_s):
    @pl.when(pl.program_id(0) == 0)
    def _():
        tm = tri_s.shape[0]
        tri_s[...] = jnp.where(lax.broadcasted_iota(I32, (tm, tm), 1) < lax.broadcasted_iota(I32, (tm, tm), 0),
                               1.0, 0.0).astype(BF16)
        run_s[...] = jnp.zeros_like(run_s)

    h = h_ref[...]
    w = w_ref[...]
    h_hi = h.astype(BF16)
    h_lo = (h - h_hi.astype(F32)).astype(BF16)
    w_hi = w.astype(BF16)
    w_lo = (w - w_hi.astype(F32)).astype(BF16)
    logits = (jnp.dot(h_hi, w_hi, preferred_element_type=F32)
              + (jnp.dot(h_hi, w_lo, preferred_element_type=F32)
                 + jnp.dot(h_lo, w_hi, preferred_element_type=F32)))
    scores = jax.nn.sigmoid(logits)
    biased = scores + bias_ref[...]
    E = biased.shape[1]
    gw = E // N_GROUPS
    lane = lax.broadcasted_iota(I32, biased.shape, 1)
    neg = -jnp.inf

    m1 = _seg_allreduce(biased, lane, gw, jnp.maximum)
    is_m1 = biased == m1
    cnt = _seg_allreduce(is_m1.astype(F32), lane, gw, jnp.add)
    m2 = jnp.where(cnt >= 2.0, m1, _seg_allreduce(jnp.where(is_m1, neg, biased), lane, gw, jnp.maximum))
    gs = m1 + m2

    gi = lane // gw
    rank = jnp.zeros(biased.shape, I32)
    for k in range(1, N_GROUPS):
        other = pltpu.roll(gs, k * gw, 1)
        ogi = (gi - k) % N_GROUPS
        beats = (other > gs) | ((other == gs) & (ogi < gi))
        rank = rank + beats.astype(I32)
    cur = jnp.where(rank < TOPK_GROUPS, biased, neg)

    sel = jnp.zeros(biased.shape, jnp.bool_)
    firsts = []
    for k in range(TOP_K):
        mx = jnp.max(cur, -1, keepdims=True)
        first = jnp.min(jnp.where(cur == mx, lane, E), -1, keepdims=True)
        pick = lane == first
        sel = sel | pick
        firsts.append(first)
        cur = jnp.where(pick, neg, cur)

    sel_f = jnp.where(sel, 1.0, 0.0)
    rank_in_e = run_s[...] + jnp.dot(tri_s[...], sel_f.astype(BF16), preferred_element_type=F32)
    run_s[...] = run_s[...] + jnp.sum(sel_f, 0, keepdims=True)
    cnt_ref[...] = run_s[...]

    wsum = jnp.sum(jnp.where(sel, scores, 0.0), -1, keepdims=True)
    w8 = jnp.zeros(biased.shape, F32)
    idx = jnp.zeros(biased.shape, I32)
    pos = jnp.zeros(biased.shape, F32)
    for k in range(TOP_K):
        pick = lane == firsts[k]
        wk = jnp.sum(jnp.where(pick, scores, 0.0), -1, keepdims=True)
        pk = jnp.sum(jnp.where(pick, rank_in_e, 0.0), -1, keepdims=True)
        w8 = jnp.where(lane == k, wk / wsum * ROUTED_SCALE, w8)
        idx = jnp.where(lane == k, firsts[k], idx)
        pos = jnp.where(lane == k, pk, pos)
    w8_ref[...] = w8
    idx_ref[...] = idx
    pos_ref[...] = pos.astype(I32)


def _router(h1f, w_router, router_bias):
    T, D = h1f.shape
    E = w_router.shape[1]
    assert E == LANES and E % N_GROUPS == 0
    tm = _tile(T, 512)
    row = pl.BlockSpec((tm, E), lambda i: (i, 0))
    return pl.pallas_call(
        _router_kernel, grid=(T // tm,),
        in_specs=[pl.BlockSpec((tm, D), lambda i: (i, 0)),
                  pl.BlockSpec((D, E), lambda i: (0, 0)),
                  pl.BlockSpec((1, E), lambda i: (0, 0))],
        out_specs=[row, row, row, pl.BlockSpec((1, E), lambda i: (0, 0))],
        out_shape=[jax.ShapeDtypeStruct((T, E), F32), jax.ShapeDtypeStruct((T, E), I32),
                   jax.ShapeDtypeStruct((T, E), I32), jax.ShapeDtypeStruct((1, E), F32)],
        scratch_shapes=[pltpu.VMEM((tm, tm), BF16), pltpu.VMEM((1, E), F32)],
        compiler_params=_params("arbitrary"), name="router",
    )(h1f, w_router, router_bias.reshape(1, E))


def _dispatch_kernel(fill_start_ref, fill_n_ref, d8_ref, hp_ref, xs_hbm, zero_s, sem, zsem):
    tm = hp_ref.shape[0]
    E = fill_n_ref.shape[0]

    def row_copy(r, k):
        return pltpu.make_async_copy(hp_ref.at[pl.ds(r, 1), :],
                                     xs_hbm.at[pl.ds(d8_ref[0, 0, r * TOP_K + k], 1), :], sem.at[0])

    def issue(r, c):
        for k in range(TOP_K):
            row_copy(r, k).start()
        return c
    lax.fori_loop(0, tm, issue, 0)

    @pl.when(pl.program_id(0) == 0)
    def _():
        zero_s[...] = jnp.zeros_like(zero_s)

        def zcopy(row):
            return pltpu.make_async_copy(zero_s, xs_hbm.at[pl.ds(row, 1), :], zsem.at[0])

        def per_expert(e, c):
            s0 = fill_start_ref[e]
            n = fill_n_ref[e]

            def zi(r, c2):
                zcopy(s0 + r).start()
                return c2
            lax.fori_loop(0, n, zi, 0)

            def zw(r, c2):
                zcopy(s0 + r).wait()
                return c2
            lax.fori_loop(0, n, zw, 0)
            return c
        lax.fori_loop(0, E, per_expert, 0)

    def wait(r, c):
        for k in range(TOP_K):
            row_copy(r, k).wait()
        return c
    lax.fori_loop(0, tm, wait, 0)


def _dispatch(h1p, dest8, fill_start, fill_n, n_rows):
    T, W = h1p.shape
    tm = _tile(T, 512)
    nt = T // tm
    d8 = dest8.reshape(nt, 1, tm * TOP_K)
    grid_spec = pltpu.PrefetchScalarGridSpec(
        num_scalar_prefetch=2, grid=(nt,),
        in_specs=[pl.BlockSpec((1, 1, tm * TOP_K), lambda i, fs, fn: (i, 0, 0), memory_space=pltpu.SMEM),
                  pl.BlockSpec((tm, W), lambda i, fs, fn: (i, 0))],
        out_specs=pl.BlockSpec(memory_space=pl.ANY),
        scratch_shapes=[pltpu.VMEM((1, W), U32), pltpu.SemaphoreType.DMA((1,)), pltpu.SemaphoreType.DMA((1,))])
    return pl.pallas_call(
        _dispatch_kernel, grid_spec=grid_spec,
        out_shape=jax.ShapeDtypeStruct((n_rows, W), U32),
        compiler_params=_params("arbitrary"), name="dispatch",
    )(fill_start, fill_n, d8, h1p)


def _expert_changed(blk_e_ref, i):
    return jnp.logical_or(i == 0, blk_e_ref[i] != blk_e_ref[jnp.maximum(i - 1, 0)])


def _experts_up_kernel(blk_e_ref, n_used_ref, xs_ref, wg_ref, wu_ref, hb_ref, wgu_s):
    i = pl.program_id(0)
    F = wg_ref.shape[2]

    @pl.when(i < n_used_ref[0])
    def _():
        @pl.when(_expert_changed(blk_e_ref, i))
        def _():
            wgu_s[:, :F] = wg_ref[0].astype(BF16)
            wgu_s[:, F:] = wu_ref[0].astype(BF16)

        lo, hi = _unpack_halves(xs_ref[...])
        hw = lo.shape[1]
        gu = (jnp.dot(lo.astype(BF16), wgu_s[:hw, :], preferred_element_type=F32)
              + jnp.dot(hi.astype(BF16), wgu_s[hw:, :], preferred_element_type=F32))
        hb_ref[...] = (_silu(gu[:, :F]) * gu[:, F:]).astype(BF16)

    @pl.when(i >= n_used_ref[0])
    def _():
        hb_ref[...] = jnp.zeros_like(hb_ref)


def _experts_down_kernel(blk_e_ref, n_used_ref, hb_ref, wd_ref, ys_ref, wd_s):
    i = pl.program_id(0)

    @pl.when(i < n_used_ref[0])
    def _():
        @pl.when(_expert_changed(blk_e_ref, i))
        def _():
            wd_s[...] = wd_ref[0].astype(BF16)

        ys_ref[...] = _pack_halves(jnp.dot(hb_ref[...], wd_s[...], preferred_element_type=F32))

    @pl.when(i >= n_used_ref[0])
    def _():
        ys_ref[...] = jnp.zeros_like(ys_ref)


def _experts(xs, blk_e, n_used, wg, wu, wd):
    n_rows, W = xs.shape
    E, D, F = wg.shape
    R = EXPERT_ROWS
    n_blocks = n_rows // R
    xrow = lambda i, be, nu: (jnp.minimum(i, jnp.maximum(nu[0] - 1, 0)), 0)
    wspec = lambda shape: pl.BlockSpec((1,) + shape, lambda i, be, nu: (be[i], 0, 0))
    hb = pl.pallas_call(
        _experts_up_kernel,
        grid_spec=pltpu.PrefetchScalarGridSpec(
            num_scalar_prefetch=2, grid=(n_blocks,),
            in_specs=[pl.BlockSpec((R, W), xrow), wspec((D, F)), wspec((D, F))],
            out_specs=pl.BlockSpec((R, F), lambda i, be, nu: (i, 0)),
            scratch_shapes=[pltpu.VMEM((D, 2 * F), BF16)]),
        out_shape=jax.ShapeDtypeStruct((n_rows, F), BF16),
        compiler_params=_params("arbitrary"), name="experts_up",
    )(blk_e, n_used, xs, wg, wu)
    return pl.pallas_call(
        _experts_down_kernel,
        grid_spec=pltpu.PrefetchScalarGridSpec(
            num_scalar_prefetch=2, grid=(n_blocks,),
            in_specs=[pl.BlockSpec((R, F), lambda i, be, nu: (i, 0)), wspec((F, D))],
            out_specs=pl.BlockSpec((R, W), lambda i, be, nu: (i, 0)),
            scratch_shapes=[pltpu.VMEM((F, D), BF16)]),
        out_shape=jax.ShapeDtypeStruct((n_rows, W), U32),
        compiler_params=_params("arbitrary"), name="experts_down",
    )(blk_e, n_used, hb, wd)


def _final_kernel(d8c_ref, d8n_ref, ys_hbm, hf_ref, hp_ref, w8_ref, wsg_ref, wsu_ref, wsd_ref, g_ref, b_ref,
                  o_ref, gbuf, sem, *, alpha):
    i = pl.program_id(0)
    nt = pl.num_programs(0)
    tm = hf_ref.shape[0]
    n = tm * TOP_K
    slot = i % 2

    def gather(d8_ref, s):
        def issue(g, c):
            for sub in range(SUBLANES):
                pltpu.make_async_copy(ys_hbm.at[pl.ds(d8_ref[0, 0, g * SUBLANES + sub], 1), :],
                                      gbuf.at[s, g, pl.ds(sub, 1), :], sem.at[s]).start()
            return c
        lax.fori_loop(0, n // SUBLANES, issue, 0)

    @pl.when(i == 0)
    def _():
        gather(d8c_ref, 0)

    for s in range(2):
        @pl.when(jnp.logical_and(i + 1 < nt, slot == 1 - s))
        def _():
            gather(d8n_ref, s)

    lo, hi = _unpack_halves(hp_ref[...])
    x = jnp.concatenate([lo.astype(BF16), hi.astype(BF16)], axis=1)
    hs = (_silu(jnp.dot(x, wsg_ref[...], preferred_element_type=F32))
          * jnp.dot(x, wsu_ref[...], preferred_element_type=F32)).astype(BF16)
    acc = alpha * hf_ref[...] + jnp.dot(hs, wsd_ref[...], preferred_element_type=F32)

    def wait(g, c):
        for sub in range(SUBLANES):
            pltpu.make_async_copy(ys_hbm.at[pl.ds(0, 1), :], gbuf.at[slot, g, pl.ds(sub, 1), :],
                                  sem.at[slot]).wait()
        return c
    lax.fori_loop(0, n // SUBLANES, wait, 0)

    r_lo = r_hi = None
    gk = tm // SUBLANES
    for k in range(TOP_K):
        lo, hi = _unpack_halves(gbuf[slot, k * gk:(k + 1) * gk].reshape(tm, gbuf.shape[-1]))
        wk = w8_ref[:, k:k + 1]
        r_lo = lo * wk if r_lo is None else r_lo + lo * wk
        r_hi = hi * wk if r_hi is None else r_hi + hi * wk
    o_ref[...] = _ln_rows(acc + jnp.concatenate([r_lo, r_hi], axis=1), g_ref[...], b_ref[...])


def _final(dest8, w8, ys, h1f, h1p, wsg, wsu, wsd, g, b, *, alpha):
    T, D = h1f.shape
    W = ys.shape[1]
    F = wsg.shape[1]
    tm = _tile(T, 128)
    nt = T // tm
    d8 = dest8.reshape(nt, tm, TOP_K).transpose(0, 2, 1).reshape(nt, 1, TOP_K * tm)
    row = lambda w: pl.BlockSpec((tm, w), lambda i: (i, 0))
    vec = pl.BlockSpec((1, D), lambda i: (0, 0))
    kern = functools.partial(_final_kernel, alpha=alpha)
    return pl.pallas_call(
        kern, grid=(nt,),
        in_specs=[pl.BlockSpec((1, 1, TOP_K * tm), lambda i: (i, 0, 0), memory_space=pltpu.SMEM),
                  pl.BlockSpec((1, 1, TOP_K * tm), lambda i: (jnp.minimum(i + 1, nt - 1), 0, 0),
                               memory_space=pltpu.SMEM),
                  pl.BlockSpec(memory_space=pl.ANY), row(D), row(W), row(TOP_K),
                  pl.BlockSpec((D, F), lambda i: (0, 0)), pl.BlockSpec((D, F), lambda i: (0, 0)),
                  pl.BlockSpec((F, D), lambda i: (0, 0)), vec, vec],
        out_specs=row(D),
        out_shape=jax.ShapeDtypeStruct((T, D), F32),
        scratch_shapes=[pltpu.VMEM((2, TOP_K * tm // SUBLANES, SUBLANES, W), U32),
                        pltpu.SemaphoreType.DMA((2,))],
        compiler_params=_params("arbitrary"), name="final",
    )(d8, d8, ys, h1f, h1p, w8, wsg, wsu, wsd, g.reshape(1, D), b.reshape(1, D))


def _rope_tables(positions, dh):
    rot = dh // ROT_FRACTION
    half = rot // 2
    inv_freq = jnp.power(ROPE_THETA, -jnp.arange(0, rot, 2, dtype=F32) / rot)
    ang = positions.astype(F32).reshape(-1, 1) * inv_freq
    cos, sin = jnp.cos(ang), jnp.sin(ang)
    T = ang.shape[0]
    cosf = jnp.concatenate([cos, cos, jnp.ones((T, dh - rot), F32)], axis=1)
    sinf = jnp.concatenate([-sin, sin, jnp.zeros((T, dh - rot), F32)], axis=1)
    return cosf, sinf, half


def _dispatch_plan(idx, pos, cnt):
    T, E = idx.shape
    R = EXPERT_ROWS
    n_blocks = -(-(T * TOP_K) // R) + E
    counts = cnt.reshape(E).astype(I32)
    padded = (counts + R - 1) // R * R
    pend = jnp.cumsum(padded)
    pstart = pend - padded
    idx8 = idx[:, :TOP_K]
    onehot = idx8[:, :, None] == jnp.arange(E, dtype=I32)
    dest8 = jnp.sum(jnp.where(onehot, pstart, 0), axis=-1) + pos[:, :TOP_K]
    blk_e = jnp.minimum(jnp.searchsorted(pend, jnp.arange(n_blocks, dtype=I32) * R, side='right'),
                        E - 1).astype(I32)
    n_used = (pend[-1:] // R).astype(I32)
    fill_start = (pstart + counts).astype(I32)
    fill_n = (padded - counts).astype(I32)
    return blk_e, n_used, dest8.astype(I32), fill_start, fill_n, n_blocks * R


def kernel(x, positions, ln_in_g, ln_in_b, w_in, b_gate, lam_q1, lam_k1, lam_q2, lam_k2, subln_g, conv_w, conv_b, w_rg_a, b_rg_a, w_rg_x, b_rg_x, lru_lambda, w_proj_attn, w_proj_rnn, w_out, ln1_g, ln1_b, w_router, router_bias, w_exp_gate, w_exp_up, w_exp_down, w_sh_gate, w_sh_up, w_sh_down, ln2_g, ln2_b):
    B, S, D = x.shape
    T = B * S
    depth = w_in.shape[0]
    alpha = (2 * depth) ** 0.25
    dh = lam_q1.shape[-1]
    att_v = w_proj_attn.shape[1]
    d_rnn = conv_w.shape[-1]
    att_qk = (w_in.shape[2] - att_v - 2 * d_rnn - 2 * D) // 2
    heads = att_v // (2 * dh)
    assert dh == LANES and att_qk == att_v
    cosf, sinf, half = _rope_tables(positions, dh)

    hf, hb = _ln_in(x.reshape(T, D), ln_in_g, ln_in_b)
    for l in range(depth):
        lambda_init = 0.8 - 0.6 * math.exp(-0.3 * l)
        lam = (jnp.exp(jnp.sum(lam_q1[l] * lam_k1[l])) - jnp.exp(jnp.sum(lam_q2[l] * lam_k2[l]))
               + lambda_init).reshape(1).astype(F32)
        proj = _in_proj(hb, w_in[l].astype(BF16), cosf, sinf, att_qk=att_qk,
                        q_scale=dh ** -0.5 * LOG2E, half=half)
        attn = _attention(proj, lam, subln_g[l], B=B, S=S, heads=heads, dh=dh, att_qk=att_qk,
                          lambda_init=lambda_init)
        rnn = _rglru(proj, conv_w[l], conv_b[l], w_rg_a[l], b_rg_a[l], w_rg_x[l], b_rg_x[l], lru_lambda[l],
                     B=B, S=S, off_xr=2 * att_qk + att_v)
        mixed = _mix(attn, rnn, w_proj_attn[l].astype(BF16), w_proj_rnn[l].astype(BF16), proj, b_gate[l],
                     off_g=2 * att_qk + att_v + 2 * d_rnn)
        h1f, h1p = _out_ln(mixed, w_out[l].astype(BF16), hf, ln1_g[l], ln1_b[l], alpha=alpha)
        w8, idx, pos, cnt = _router(h1f, w_router[l], router_bias[l])
        blk_e, n_used, dest8, fill_start, fill_n, n_rows = _dispatch_plan(idx, pos, cnt)
        xs = _dispatch(h1p, dest8, fill_start, fill_n, n_rows)
        ys = _experts(xs, blk_e, n_used, w_exp_gate[l], w_exp_up[l], w_exp_down[l])
        hf = _final(dest8, w8[:, :TOP_K], ys, h1f, h1p, w_sh_gate[l].astype(BF16), w_sh_up[l].astype(BF16),
                    w_sh_down[l].astype(BF16), ln2_g[l], ln2_b[l], alpha=alpha)
        if l + 1 < depth:
            hb = hf.astype(BF16)
    return hf.reshape(B, S, D)
```

```python
import functools
import math

import jax
import jax.numpy as jnp
from jax import lax
from jax.experimental import pallas as pl
from jax.experimental.pallas import tpu as pltpu

F32 = jnp.float32
BF16 = jnp.bfloat16
I32 = jnp.int32
U32 = jnp.uint32

CHUNK = 64
ROPE_THETA = 500000.0
ROT_FRACTION = 4
LRU_C = 8.0
CONV_WIDTH = 4
N_GROUPS = 8
TOPK_GROUPS = 4
TOP_K = 8
ROUTED_SCALE = 2.5
LN_EPS = 1e-5
LOG2E = 1.4426950408889634

LANES = 128
SUBLANES = 8
VMEM_LIMIT_BYTES = 56 * 1024 * 1024
EXPERT_ROWS = 256
DMA_UNROLL = 8


def _tile(n, pref):
    t = min(n, pref)
    while n % t:
        t //= 2
    return t


def _params(*sem):
    return pltpu.CompilerParams(dimension_semantics=sem, vmem_limit_bytes=VMEM_LIMIT_BYTES)


def _ln_rows(x, g, b):
    mu = jnp.mean(x, -1, keepdims=True)
    xc = x - mu
    var = jnp.mean(xc * xc, -1, keepdims=True)
    return xc * lax.rsqrt(var + LN_EPS) * g + b


def _silu(x):
    return x * jax.nn.sigmoid(x)


def _pack_halves(y):
    half = y.shape[1] // 2
    bits = lax.bitcast_convert_type(y.astype(BF16).astype(F32), U32)
    return (bits[:, :half] >> 16) | (bits[:, half:] & jnp.uint32(0xFFFF0000))


def _unpack_halves(p):
    lo = lax.bitcast_convert_type(p << 16, F32)
    hi = lax.bitcast_convert_type(p & jnp.uint32(0xFFFF0000), F32)
    return lo, hi


def _ln_in_kernel(x_ref, g_ref, b_ref, yf_ref, yb_ref):
    y = _ln_rows(x_ref[...], g_ref[...], b_ref[...])
    yf_ref[...] = y
    yb_ref[...] = y.astype(BF16)


def _ln_in(x2, g, b):
    T, D = x2.shape
    tm = _tile(T, 256)
    row = pl.BlockSpec((tm, D), lambda i: (i, 0))
    vec = pl.BlockSpec((1, D), lambda i: (0, 0))
    return pl.pallas_call(
        _ln_in_kernel, grid=(T // tm,),
        in_specs=[row, vec, vec], out_specs=[row, row],
        out_shape=[jax.ShapeDtypeStruct((T, D), F32), jax.ShapeDtypeStruct((T, D), BF16)],
        compiler_params=_params("parallel"), name="ln_in",
    )(x2, g.reshape(1, D), b.reshape(1, D))


def _in_proj_kernel(a_ref, w_ref, cos_ref, sin_ref, o_ref, *, n_q, n_qk, q_scale, half, n_chunks):
    j = pl.program_id(1)
    tm, tn = o_ref.shape
    cn = tn // n_chunks
    scale = jnp.where(j < n_q, q_scale, 1.0)
    cosf = jnp.where(j < n_qk, cos_ref[...], 1.0) * scale
    sinf = jnp.where(j < n_qk, sin_ref[...], 0.0) * scale
    low = lax.broadcasted_iota(I32, (tm, LANES), 1) < half
    a = a_ref[...]
    for c in range(n_chunks):
        acc = jnp.dot(a, w_ref[:, c * cn:(c + 1) * cn], preferred_element_type=F32)
        for g in range(cn // LANES):
            x = acc[:, g * LANES:(g + 1) * LANES]
            partner = jnp.where(low, pltpu.roll(x, LANES - half, 1), pltpu.roll(x, half, 1))
            col = c * cn + g * LANES
            o_ref[:, col:col + LANES] = (x * cosf + partner * sinf).astype(BF16)


def _in_proj(h0b, w, cosf, sinf, *, att_qk, q_scale, half):
    T, K = h0b.shape
    N = w.shape[1]
    tm = _tile(T, 1024)
    tn = _tile(att_qk, 1024)
    assert N % tn == 0
    kern = functools.partial(_in_proj_kernel, n_q=att_qk // tn, n_qk=2 * att_qk // tn,
                             q_scale=q_scale, half=half, n_chunks=max(1, tn // 512))
    return pl.pallas_call(
        kern, grid=(T // tm, N // tn),
        in_specs=[pl.BlockSpec((tm, K), lambda i, j: (i, 0)),
                  pl.BlockSpec((K, tn), lambda i, j: (0, j)),
                  pl.BlockSpec((tm, LANES), lambda i, j: (i, 0)),
                  pl.BlockSpec((tm, LANES), lambda i, j: (i, 0))],
        out_specs=pl.BlockSpec((tm, tn), lambda i, j: (i, j)),
        out_shape=jax.ShapeDtypeStruct((T, N), BF16),
        compiler_params=_params("parallel", "arbitrary"), name="in_proj",
    )(h0b, w, cosf, sinf)


def _attn_kernel(lam_ref, q_ref, k_ref, v_ref, g_ref, o_ref, qt_s, vt_s, st_s, acc_s, *, dh, tk, out_scale):
    i = pl.program_id(2)
    tq = q_ref.shape[0]
    S = v_ref.shape[0]

    @pl.when(i == 0)
    def _():
        for c in range(S // tk):
            vt_s[c] = v_ref[c * tk:(c + 1) * tk, :].astype(F32).T.astype(BF16)

    qt_s[...] = q_ref[...].astype(F32).T.astype(BF16)

    def scores(c, m):
        r0 = pl.multiple_of(c * tk, tk)
        return jnp.dot(k_ref[pl.ds(r0, tk), m * dh:(m + 1) * dh], qt_s[m * dh:(m + 1) * dh, :],
                       preferred_element_type=F32)

    def phase1(c, mx):
        out = []
        for m in range(2):
            st = scores(c, m)
            st_s[c, m] = st
            out.append(jnp.maximum(mx[m], jnp.max(st, 0, keepdims=True)))
        return tuple(out)

    mx = lax.fori_loop(0, i, phase1, tuple(jnp.full((1, tq), -jnp.inf, F32) for _ in range(2)))
    visible = ((lax.broadcasted_iota(I32, (tk, tq), 0) // CHUNK)
               <= (lax.broadcasted_iota(I32, (tk, tq), 1) // CHUNK))
    mxs = []
    for m in range(2):
        st = jnp.where(visible, scores(i, m), -jnp.inf)
        st_s[i, m] = st
        mxs.append(jnp.maximum(mx[m], jnp.max(st, 0, keepdims=True)))

    acc_s[...] = jnp.zeros_like(acc_s)

    def phase2(c, l):
        vt = vt_s[c]
        out = []
        for m in range(2):
            p = jnp.exp2(st_s[c, m] - mxs[m])
            acc_s[m] += jnp.dot(vt, p.astype(BF16), preferred_element_type=F32)
            out.append(l[m] + jnp.sum(p, 0, keepdims=True))
        return tuple(out)

    l0, l1 = lax.fori_loop(0, i + 1, phase2, tuple(jnp.zeros((1, tq), F32) for _ in range(2)))
    ot = acc_s[0] * (1.0 / l0) - lam_ref[0] * (acc_s[1] * (1.0 / l1))
    ot = ot * lax.rsqrt(jnp.mean(ot * ot, 0, keepdims=True) + LN_EPS)
    o_ref[...] = (ot.T * (g_ref[...] * out_scale)).astype(BF16)


def _attention(proj, lam, subln_g, *, B, S, heads, dh, att_qk, lambda_init):
    T = B * S
    wv = 2 * dh
    tq = _tile(S, 512)
    tk = tq
    nq = S // tq
    kern = functools.partial(_attn_kernel, dh=dh, tk=tk, out_scale=1.0 - lambda_init)
    scratch = [pltpu.VMEM((wv, tq), BF16), pltpu.VMEM((S // tk, wv, tk), BF16),
               pltpu.VMEM((S // tk, 2, tk, tq), F32), pltpu.VMEM((2, wv, tq), F32)]
    return pl.pallas_call(
        kern, grid=(B, heads, nq),
        in_specs=[pl.BlockSpec(memory_space=pltpu.SMEM),
                  pl.BlockSpec((tq, wv), lambda b, h, i: (b * nq + i, h)),
                  pl.BlockSpec((S, wv), lambda b, h, i: (b, att_qk // wv + h)),
                  pl.BlockSpec((S, wv), lambda b, h, i: (b, 2 * att_qk // wv + h)),
                  pl.BlockSpec((1, wv), lambda b, h, i: (0, 0))],
        out_specs=pl.BlockSpec((tq, wv), lambda b, h, i: (b * nq + i, h)),
        out_shape=jax.ShapeDtypeStruct((T, heads * wv), BF16),
        scratch_shapes=scratch,
        compiler_params=_params("parallel", "parallel", "arbitrary"), name="attention",
    )(lam, proj, proj, proj, subln_g.reshape(1, wv))


def _gelu_tanh(x):
    return 0.5 * x * (1.0 + jnp.tanh(0.7978845608028654 * (x + 0.044715 * (x * x * x))))


def _rglru_kernel(xr_ref, yr_ref, cw_ref, cb_ref, wa_ref, ba_ref, wx_ref, bx_ref, lam_ref, o_ref,
                  a_s, u_s, xprev_s, h_s, *, n_heads, blk, scan_cols):
    @pl.when(pl.program_id(1) == 0)
    def _():
        xprev_s[...] = jnp.zeros_like(xprev_s)
        h_s[...] = jnp.zeros_like(h_s)

    tt, D = xr_ref.shape
    z = -lam_ref[...]
    sp = jnp.maximum(z, 0.0) + jnp.log1p(jnp.exp(-jnp.abs(z)))
    for hh in range(n_heads):
        sl = slice(hh * blk, (hh + 1) * blk)
        xe = jnp.concatenate([xprev_s[:, sl], xr_ref[:, sl].astype(F32)], axis=0)
        xc = cb_ref[:, sl]
        for j in range(CONV_WIDTH):
            o = SUBLANES - (CONV_WIDTH - 1) + j
            xc = xc + cw_ref[j:j + 1, sl] * xe[o:o + tt]
        xb = xc.astype(BF16)
        r = jax.nn.sigmoid(jnp.dot(xb, wa_ref[hh], preferred_element_type=F32) + ba_ref[:, sl])
        ig = jax.nn.sigmoid(jnp.dot(xb, wx_ref[hh], preferred_element_type=F32) + bx_ref[:, sl])
        log_a = (-LRU_C * r) * sp[:, sl]
        a = jnp.exp(log_a)
        a_s[:, sl] = a
        u_s[:, sl] = jnp.sqrt(-jnp.tanh(log_a) * (a * a + 1.0)) * ig * xc
    xprev_s[...] = xr_ref[tt - SUBLANES:tt, :].astype(F32)

    row = lax.broadcasted_iota(I32, (SUBLANES, scan_cols), 0)

    def scan_body(g, h):
        r0 = pl.multiple_of(g * SUBLANES, SUBLANES)
        outs = []
        for c in range(D // scan_cols):
            cs = slice(c * scan_cols, (c + 1) * scan_cols)
            a = a_s[pl.ds(r0, SUBLANES), cs]
            u = u_s[pl.ds(r0, SUBLANES), cs]
            for s in (1, 2, 4):
                valid = row >= s
                u = jnp.where(valid, a * pltpu.roll(u, s, 0) + u, u)
                a = jnp.where(valid, a * pltpu.roll(a, s, 0), a)
            hg = u + a * h[:, cs]
            u_s[pl.ds(r0, SUBLANES), cs] = hg
            outs.append(hg[SUBLANES - 1:SUBLANES, :])
        return jnp.concatenate(outs, axis=1)

    h_s[...] = lax.fori_loop(0, tt // SUBLANES, scan_body, h_s[...])
    o_ref[...] = (u_s[...] * _gelu_tanh(yr_ref[...].astype(F32))).astype(BF16)


def _rglru(proj, conv_w, conv_b, w_ra, b_ra, w_rx, b_rx, lru_lambda, *, B, S, off_xr):
    T = B * S
    n_heads, blk, _ = w_ra.shape
    D = n_heads * blk
    tt = _tile(S, 256)
    nt = S // tt
    assert off_xr % D == 0
    cx = off_xr // D
    vec = pl.BlockSpec((1, D), lambda b, t: (0, 0))
    wspec = pl.BlockSpec((n_heads, blk, blk), lambda b, t: (0, 0, 0))
    kern = functools.partial(_rglru_kernel, n_heads=n_heads, blk=blk, scan_cols=_tile(D, 512))
    return pl.pallas_call(
        kern, grid=(B, nt),
        in_specs=[pl.BlockSpec((tt, D), lambda b, t: (b * nt + t, cx)),
                  pl.BlockSpec((tt, D), lambda b, t: (b * nt + t, cx + 1)),
                  pl.BlockSpec((CONV_WIDTH, D), lambda b, t: (0, 0)), vec,
                  wspec, vec, wspec, vec, vec],
        out_specs=pl.BlockSpec((tt, D), lambda b, t: (b * nt + t, 0)),
        out_shape=jax.ShapeDtypeStruct((T, D), BF16),
        scratch_shapes=[pltpu.VMEM((tt, D), F32), pltpu.VMEM((tt, D), F32),
                        pltpu.VMEM((SUBLANES, D), F32), pltpu.VMEM((1, D), F32)],
        compiler_params=_params("parallel", "arbitrary"), name="rglru",
    )(proj, proj, conv_w, conv_b.reshape(1, D), w_ra.astype(BF16), b_ra.reshape(1, D),
      w_rx.astype(BF16), b_rx.reshape(1, D), lru_lambda.reshape(1, D))


def _mix_kernel(at_ref, rn_ref, wa_ref, wr_ref, g0_ref, g1_ref, b0_ref, b1_ref, o_ref):
    pa = jnp.dot(at_ref[...], wa_ref[...], preferred_element_type=F32)
    pr = jnp.dot(rn_ref[...], wr_ref[...], preferred_element_type=F32)
    s0 = jax.nn.sigmoid(g0_ref[...].astype(F32) + b0_ref[...])
    s1 = jax.nn.sigmoid(g1_ref[...].astype(F32) + b1_ref[...])
    o_ref[...] = (s0 * pa + s1 * pr).astype(BF16)


def _mix(attn, rnn, wa, wr, proj, b_gate, *, off_g):
    T, Ka = attn.shape
    Kr = rnn.shape[1]
    D = wa.shape[1]
    tm = _tile(T, 512)
    tn = _tile(D, 512)
    assert off_g % tn == 0
    cg = off_g // tn
    nn = D // tn
    return pl.pallas_call(
        _mix_kernel, grid=(T // tm, nn),
        in_specs=[pl.BlockSpec((tm, Ka), lambda i, j: (i, 0)),
                  pl.BlockSpec((tm, Kr), lambda i, j: (i, 0)),
                  pl.BlockSpec((Ka, tn), lambda i, j: (0, j)),
                  pl.BlockSpec((Kr, tn), lambda i, j: (0, j)),
                  pl.BlockSpec((tm, tn), lambda i, j: (i, cg + j)),
                  pl.BlockSpec((tm, tn), lambda i, j: (i, cg + nn + j)),
                  pl.BlockSpec((1, tn), lambda i, j: (0, j)),
                  pl.BlockSpec((1, tn), lambda i, j: (0, nn + j))],
        out_specs=pl.BlockSpec((tm, tn), lambda i, j: (i, j)),
        out_shape=jax.ShapeDtypeStruct((T, D), BF16),
        compiler_params=_params("parallel", "arbitrary"), name="mix",
    )(attn, rnn, wa, wr, proj, proj, b_gate.reshape(1, 2 * D), b_gate.reshape(1, 2 * D))


def _out_ln_kernel(a_ref, w_ref, h0_ref, g_ref, b_ref, hf_ref, hp_ref, acc_s, *, alpha):
    j = pl.program_id(1)
    n_n, _, tn = acc_s.shape
    acc_s[j] = alpha * h0_ref[...] + jnp.dot(a_ref[...], w_ref[...], preferred_element_type=F32)

    @pl.when(j == n_n - 1)
    def _():
        D = n_n * tn
        mu = sum(jnp.sum(acc_s[c], -1, keepdims=True) for c in range(n_n)) / D
        var = sum(jnp.sum(jnp.square(acc_s[c] - mu), -1, keepdims=True) for c in range(n_n)) / D
        rstd = lax.rsqrt(var + LN_EPS)
        hn = n_n // 2
        for c in range(hn):
            ys = []
            for cc in (c, c + hn):
                cs = slice(cc * tn, (cc + 1) * tn)
                y = (acc_s[cc] - mu) * rstd * g_ref[:, cs] + b_ref[:, cs]
                hf_ref[:, cs] = y
                ys.append(y)
            hp_ref[:, c * tn:(c + 1) * tn] = _pack_halves(jnp.concatenate(ys, axis=1))


def _out_ln(mixed, w, h0f, g, b, *, alpha):
    T, K = mixed.shape
    D = w.shape[1]
    tm = _tile(T, 512)
    tn = _tile(D // 2, 512)
    vec = pl.BlockSpec((1, D), lambda i, j: (0, 0))
    return pl.pallas_call(
        functools.partial(_out_ln_kernel, alpha=alpha), grid=(T // tm, D // tn),
        in_specs=[pl.BlockSpec((tm, K), lambda i, j: (i, 0), pipeline_mode=pl.Buffered(1)),
                  pl.BlockSpec((K, tn), lambda i, j: (0, j)),
                  pl.BlockSpec((tm, tn), lambda i, j: (i, j)), vec, vec],
        out_specs=[pl.BlockSpec((tm, D), lambda i, j: (i, 0)),
                   pl.BlockSpec((tm, D // 2), lambda i, j: (i, 0))],
        out_shape=[jax.ShapeDtypeStruct((T, D), F32), jax.ShapeDtypeStruct((T, D // 2), U32)],
        scratch_shapes=[pltpu.VMEM((D // tn, tm, tn), F32)],
        compiler_params=_params("parallel", "arbitrary"), name="out_ln",
    )(mixed, w, h0f, g.reshape(1, D), b.reshape(1, D))


def _seg_allreduce(x, lane, width, op):
    n = x.shape[1]
    s = 1
    while s < width:
        partner = jnp.where((lane & s) == 0, pltpu.roll(x, n - s, 1), pltpu.roll(x, s, 1))
        x = op(x, partner)
        s *= 2
    return x


def _router_kernel(h_ref, w_ref, bias_ref, w8_ref, idx_ref, pos_ref, cnt_ref, tri_s, run_s):
    @pl.when(pl.program_id(0) == 0)
    def _():
        tm = tri_s.shape[0]
        tri_s[...] = jnp.where(lax.broadcasted_iota(I32, (tm, tm), 1) < lax.broadcasted_iota(I32, (tm, tm), 0),
                               1.0, 0.0).astype(BF16)
        run_s[...] = jnp.zeros_like(run_s)

    h = h_ref[...]
    w = w_ref[...]
    h_hi = h.astype(BF16)
    h_lo = (h - h_hi.astype(F32)).astype(BF16)
    w_hi = w.astype(BF16)
    w_lo = (w - w_hi.astype(F32)).astype(BF16)
    logits = (jnp.dot(h_hi, w_hi, preferred_element_type=F32)
              + (jnp.dot(h_hi, w_lo, preferred_element_type=F32)
                 + jnp.dot(h_lo, w_hi, preferred_element_type=F32)))
    scores = jax.nn.sigmoid(logits)
    biased = scores + bias_ref[...]
    E = biased.shape[1]
    gw = E // N_GROUPS
    lane = lax.broadcasted_iota(I32, biased.shape, 1)
    neg = -jnp.inf

    m1 = _seg_allreduce(biased, lane, gw, jnp.maximum)
    is_m1 = biased == m1
    cnt = _seg_allreduce(is_m1.astype(F32), lane, gw, jnp.add)
    m2 = jnp.where(cnt >= 2.0, m1, _seg_allreduce(jnp.where(is_m1, neg, biased), lane, gw, jnp.maximum))
    gs = m1 + m2

    gi = lane // gw
    rank = jnp.zeros(biased.shape, I32)
    for k in range(1, N_GROUPS):
        other = pltpu.roll(gs, k * gw, 1)
        ogi = (gi - k) % N_GROUPS
        beats = (other > gs) | ((other == gs) & (ogi < gi))
        rank = rank + beats.astype(I32)
    cur = jnp.where(rank < TOPK_GROUPS, biased, neg)

    sel = jnp.zeros(biased.shape, jnp.bool_)
    firsts = []
    for k in range(TOP_K):
        mx = jnp.max(cur, -1, keepdims=True)
        first = jnp.min(jnp.where(cur == mx, lane, E), -1, keepdims=True)
        pick = lane == first
        sel = sel | pick
        firsts.append(first)
        cur = jnp.where(pick, neg, cur)

    sel_f = jnp.where(sel, 1.0, 0.0)
    rank_in_e = run_s[...] + jnp.dot(tri_s[...], sel_f.astype(BF16), preferred_element_type=F32)
    run_s[...] = run_s[...] + jnp.sum(sel_f, 0, keepdims=True)
    cnt_ref[...] = run_s[...]

    wsum = jnp.sum(jnp.where(sel, scores, 0.0), -1, keepdims=True)
    w8 = jnp.zeros(biased.shape, F32)
    idx = jnp.zeros(biased.shape, I32)
    pos = jnp.zeros(biased.shape, F32)
    for k in range(TOP_K):
        pick = lane == firsts[k]
        wk = jnp.sum(jnp.where(pick, scores, 0.0), -1, keepdims=True)
        pk = jnp.sum(jnp.where(pick, rank_in_e, 0.0), -1, keepdims=True)
        w8 = jnp.where(lane == k, wk / wsum * ROUTED_SCALE, w8)
        idx = jnp.where(lane == k, firsts[k], idx)
        pos = jnp.where(lane == k, pk, pos)
    w8_ref[...] = w8
    idx_ref[...] = idx
    pos_ref[...] = pos.astype(I32)


def _router(h1f, w_router, router_bias):
    T, D = h1f.shape
    E = w_router.shape[1]
    assert E == LANES and E % N_GROUPS == 0
    tm = _tile(T, 512)
    row = pl.BlockSpec((tm, E), lambda i: (i, 0))
    return pl.pallas_call(
        _router_kernel, grid=(T // tm,),
        in_specs=[pl.BlockSpec((tm, D), lambda i: (i, 0)),
                  pl.BlockSpec((D, E), lambda i: (0, 0)),
                  pl.BlockSpec((1, E), lambda i: (0, 0))],
        out_specs=[row, row, row, pl.BlockSpec((1, E), lambda i: (0, 0))],
        out_shape=[jax.ShapeDtypeStruct((T, E), F32), jax.ShapeDtypeStruct((T, E), I32),
                   jax.ShapeDtypeStruct((T, E), I32), jax.ShapeDtypeStruct((1, E), F32)],
        scratch_shapes=[pltpu.VMEM((tm, tm), BF16), pltpu.VMEM((1, E), F32)],
        compiler_params=_params("arbitrary"), name="router",
    )(h1f, w_router, router_bias.reshape(1, E))


def _dispatch_kernel(fill_start_ref, fill_n_ref, d8_ref, hp_ref, xs_hbm, zero_s, sem, zsem):
    tm = hp_ref.shape[0]
    E = fill_n_ref.shape[0]

    def row_copy(r, k):
        return pltpu.make_async_copy(hp_ref.at[pl.ds(r, 1), :],
                                     xs_hbm.at[pl.ds(d8_ref[0, 0, r * TOP_K + k], 1), :], sem.at[0])

    def issue(r, c):
        for k in range(TOP_K):
            row_copy(r, k).start()
        return c
    lax.fori_loop(0, tm, issue, 0)

    @pl.when(pl.program_id(0) == 0)
    def _():
        zero_s[...] = jnp.zeros_like(zero_s)

        def zcopy(row):
            return pltpu.make_async_copy(zero_s, xs_hbm.at[pl.ds(row, 1), :], zsem.at[0])

        def per_expert(e, c):
            s0 = fill_start_ref[e]
            n = fill_n_ref[e]

            def zi(r, c2):
                zcopy(s0 + r).start()
                return c2
            lax.fori_loop(0, n, zi, 0)

            def zw(r, c2):
                zcopy(s0 + r).wait()
                return c2
            lax.fori_loop(0, n, zw, 0)
            return c
        lax.fori_loop(0, E, per_expert, 0)

    def wait(r, c):
        for k in range(TOP_K):
            row_copy(r, k).wait()
        return c
    lax.fori_loop(0, tm, wait, 0)


def _dispatch(h1p, dest8, fill_start, fill_n, n_rows):
    T, W = h1p.shape
    tm = _tile(T, 512)
    nt = T // tm
    d8 = dest8.reshape(nt, 1, tm * TOP_K)
    grid_spec = pltpu.PrefetchScalarGridSpec(
        num_scalar_prefetch=2, grid=(nt,),
        in_specs=[pl.BlockSpec((1, 1, tm * TOP_K), lambda i, fs, fn: (i, 0, 0), memory_space=pltpu.SMEM),
                  pl.BlockSpec((tm, W), lambda i, fs, fn: (i, 0))],
        out_specs=pl.BlockSpec(memory_space=pl.ANY),
        scratch_shapes=[pltpu.VMEM((1, W), U32), pltpu.SemaphoreType.DMA((1,)), pltpu.SemaphoreType.DMA((1,))])
    return pl.pallas_call(
        _dispatch_kernel, grid_spec=grid_spec,
        out_shape=jax.ShapeDtypeStruct((n_rows, W), U32),
        compiler_params=_params("arbitrary"), name="dispatch",
    )(fill_start, fill_n, d8, h1p)


def _expert_switch(plan_ref, i, w_hbm, wbuf, sem, recast):
    def copies(e, slot):
        return [pltpu.make_async_copy(w.at[e], wbuf.at[slot, t], sem.at[slot, t]) for t, w in enumerate(w_hbm)]

    @pl.when(plan_ref[1, i] == 1)
    def _():
        e, slot, nxt = plan_ref[0, i], plan_ref[2, i], plan_ref[3, i]

        @pl.when(i == 0)
        def _():
            for c in copies(e, slot):
                c.start()

        for c in copies(e, slot):
            c.wait()
        recast(slot)

        @pl.when(nxt >= 0)
        def _():
            for c in copies(nxt, 1 - slot):
                c.start()


def _experts_up_kernel(plan_ref, n_used_ref, xs_ref, wg_hbm, wu_hbm, hb_ref, wbuf, wgu_s, sem):
    i = pl.program_id(0)
    F = wg_hbm.shape[2]

    @pl.when(i < n_used_ref[0])
    def _():
        def recast(slot):
            wgu_s[:, :F] = wbuf[slot, 0].astype(BF16)
            wgu_s[:, F:] = wbuf[slot, 1].astype(BF16)
        _expert_switch(plan_ref, i, (wg_hbm, wu_hbm), wbuf, sem, recast)

        lo, hi = _unpack_halves(xs_ref[...])
        hw = lo.shape[1]
        gu = (jnp.dot(lo.astype(BF16), wgu_s[:hw, :], preferred_element_type=F32)
              + jnp.dot(hi.astype(BF16), wgu_s[hw:, :], preferred_element_type=F32))
        hb_ref[...] = (_silu(gu[:, :F]) * gu[:, F:]).astype(BF16)

    @pl.when(i >= n_used_ref[0])
    def _():
        hb_ref[...] = jnp.zeros_like(hb_ref)


def _experts_down_kernel(plan_ref, n_used_ref, hb_ref, wd_hbm, ys_ref, wbuf, wd_s, sem):
    i = pl.program_id(0)

    @pl.when(i < n_used_ref[0])
    def _():
        def recast(slot):
            wd_s[...] = wbuf[slot, 0].astype(BF16)
        _expert_switch(plan_ref, i, (wd_hbm,), wbuf, sem, recast)

        ys_ref[...] = _pack_halves(jnp.dot(hb_ref[...], wd_s[...], preferred_element_type=F32))

    @pl.when(i >= n_used_ref[0])
    def _():
        ys_ref[...] = jnp.zeros_like(ys_ref)


def _experts(xs, plan, n_used, wg, wu, wd):
    n_rows, W = xs.shape
    E, D, F = wg.shape
    R = EXPERT_ROWS
    n_blocks = n_rows // R
    xrow = lambda i, pn, nu: (jnp.minimum(i, jnp.maximum(nu[0] - 1, 0)), 0)
    hbm = pl.BlockSpec(memory_space=pl.ANY)
    hb = pl.pallas_call(
        _experts_up_kernel,
        grid_spec=pltpu.PrefetchScalarGridSpec(
            num_scalar_prefetch=2, grid=(n_blocks,),
            in_specs=[pl.BlockSpec((R, W), xrow), hbm, hbm],
            out_specs=pl.BlockSpec((R, F), lambda i, pn, nu: (i, 0)),
            scratch_shapes=[pltpu.VMEM((2, 2, D, F), F32), pltpu.VMEM((D, 2 * F), BF16),
                            pltpu.SemaphoreType.DMA((2, 2))]),
        out_shape=jax.ShapeDtypeStruct((n_rows, F), BF16),
        compiler_params=_params("arbitrary"), name="experts_up",
    )(plan, n_used, xs, wg, wu)
    return pl.pallas_call(
        _experts_down_kernel,
        grid_spec=pltpu.PrefetchScalarGridSpec(
            num_scalar_prefetch=2, grid=(n_blocks,),
            in_specs=[pl.BlockSpec((R, F), lambda i, pn, nu: (i, 0)), hbm],
            out_specs=pl.BlockSpec((R, W), lambda i, pn, nu: (i, 0)),
            scratch_shapes=[pltpu.VMEM((2, 1, F, D), F32), pltpu.VMEM((F, D), BF16),
                            pltpu.SemaphoreType.DMA((2, 1))]),
        out_shape=jax.ShapeDtypeStruct((n_rows, W), U32),
        compiler_params=_params("arbitrary"), name="experts_down",
    )(plan, n_used, hb, wd)


def _final_kernel(d8c_ref, d8n_ref, ys_hbm, hf_ref, hp_ref, w8_ref, wsg_ref, wsu_ref, wsd_ref, g_ref, b_ref,
                  o_ref, gbuf, sem, *, alpha):
    i = pl.program_id(0)
    nt = pl.num_programs(0)
    tm = hf_ref.shape[0]
    n = tm * TOP_K
    slot = i % 2

    def gather(d8_ref, s):
        def issue(g, c):
            for sub in range(SUBLANES):
                pltpu.make_async_copy(ys_hbm.at[pl.ds(d8_ref[0, 0, g * SUBLANES + sub], 1), :],
                                      gbuf.at[s, g, pl.ds(sub, 1), :], sem.at[s]).start()
            return c
        lax.fori_loop(0, n // SUBLANES, issue, 0)

    @pl.when(i == 0)
    def _():
        gather(d8c_ref, 0)

    for s in range(2):
        @pl.when(jnp.logical_and(i + 1 < nt, slot == 1 - s))
        def _():
            gather(d8n_ref, s)

    lo, hi = _unpack_halves(hp_ref[...])
    x = jnp.concatenate([lo.astype(BF16), hi.astype(BF16)], axis=1)
    hs = (_silu(jnp.dot(x, wsg_ref[...], preferred_element_type=F32))
          * jnp.dot(x, wsu_ref[...], preferred_element_type=F32)).astype(BF16)
    acc = alpha * hf_ref[...] + jnp.dot(hs, wsd_ref[...], preferred_element_type=F32)

    def wait(g, c):
        for sub in range(SUBLANES):
            pltpu.make_async_copy(ys_hbm.at[pl.ds(0, 1), :], gbuf.at[slot, g, pl.ds(sub, 1), :],
                                  sem.at[slot]).wait()
        return c
    lax.fori_loop(0, n // SUBLANES, wait, 0)

    r_lo = r_hi = None
    gk = tm // SUBLANES
    for k in range(TOP_K):
        lo, hi = _unpack_halves(gbuf[slot, k * gk:(k + 1) * gk].reshape(tm, gbuf.shape[-1]))
        wk = w8_ref[:, k:k + 1]
        r_lo = lo * wk if r_lo is None else r_lo + lo * wk
        r_hi = hi * wk if r_hi is None else r_hi + hi * wk
    o_ref[...] = _ln_rows(acc + jnp.concatenate([r_lo, r_hi], axis=1), g_ref[...], b_ref[...])


def _final(dest8, w8, ys, h1f, h1p, wsg, wsu, wsd, g, b, *, alpha):
    T, D = h1f.shape
    W = ys.shape[1]
    F = wsg.shape[1]
    tm = _tile(T, 128)
    nt = T // tm
    d8 = dest8.reshape(nt, tm, TOP_K).transpose(0, 2, 1).reshape(nt, 1, TOP_K * tm)
    row = lambda w: pl.BlockSpec((tm, w), lambda i: (i, 0))
    vec = pl.BlockSpec((1, D), lambda i: (0, 0))
    kern = functools.partial(_final_kernel, alpha=alpha)
    return pl.pallas_call(
        kern, grid=(nt,),
        in_specs=[pl.BlockSpec((1, 1, TOP_K * tm), lambda i: (i, 0, 0), memory_space=pltpu.SMEM),
                  pl.BlockSpec((1, 1, TOP_K * tm), lambda i: (jnp.minimum(i + 1, nt - 1), 0, 0),
                               memory_space=pltpu.SMEM),
                  pl.BlockSpec(memory_space=pl.ANY), row(D), row(W), row(TOP_K),
                  pl.BlockSpec((D, F), lambda i: (0, 0)), pl.BlockSpec((D, F), lambda i: (0, 0)),
                  pl.BlockSpec((F, D), lambda i: (0, 0)), vec, vec],
        out_specs=row(D),
        out_shape=jax.ShapeDtypeStruct((T, D), F32),
        scratch_shapes=[pltpu.VMEM((2, TOP_K * tm // SUBLANES, SUBLANES, W), U32),
                        pltpu.SemaphoreType.DMA((2,))],
        compiler_params=_params("arbitrary"), name="final",
    )(d8, d8, ys, h1f, h1p, w8, wsg, wsu, wsd, g.reshape(1, D), b.reshape(1, D))


def _rope_tables(positions, dh):
    rot = dh // ROT_FRACTION
    half = rot // 2
    inv_freq = jnp.power(ROPE_THETA, -jnp.arange(0, rot, 2, dtype=F32) / rot)
    ang = positions.astype(F32).reshape(-1, 1) * inv_freq
    cos, sin = jnp.cos(ang), jnp.sin(ang)
    T = ang.shape[0]
    cosf = jnp.concatenate([cos, cos, jnp.ones((T, dh - rot), F32)], axis=1)
    sinf = jnp.concatenate([-sin, sin, jnp.zeros((T, dh - rot), F32)], axis=1)
    return cosf, sinf, half


def _plan_kernel(idx_ref, pos_ref, pstart_ref, dest_ref):
    starts = jnp.broadcast_to(pstart_ref[...], idx_ref.shape)
    dest_ref[...] = jnp.take_along_axis(starts, idx_ref[...], axis=1) + pos_ref[...]


def _dispatch_plan(idx, pos, cnt):
    T, E = idx.shape
    R = EXPERT_ROWS
    n_blocks = -(-(T * TOP_K) // R) + E
    counts = cnt.reshape(E).astype(I32)
    padded = (counts + R - 1) // R * R
    pend = jnp.cumsum(padded)
    pstart = pend - padded
    tm = _tile(T, 1024)
    row = pl.BlockSpec((tm, E), lambda i: (i, 0))
    dest = pl.pallas_call(
        _plan_kernel, grid=(T // tm,),
        in_specs=[row, row, pl.BlockSpec((1, E), lambda i: (0, 0))], out_specs=row,
        out_shape=jax.ShapeDtypeStruct((T, E), I32),
        compiler_params=_params("parallel"), name="plan",
    )(idx, pos, pstart.reshape(1, E).astype(I32))
    dest8 = dest[:, :TOP_K]
    blk_e = jnp.minimum(jnp.searchsorted(pend, jnp.arange(n_blocks, dtype=I32) * R, side='right'),
                        E - 1).astype(I32)
    n_used = (pend[-1:] // R).astype(I32)
    blk = jnp.arange(n_blocks, dtype=I32)
    first = (blk < n_used[0]) & ((blk == 0) | (blk_e != jnp.roll(blk_e, 1)))
    slot = (jnp.cumsum(first.astype(I32)) - 1) % 2
    cand = jnp.where(counts > 0, jnp.arange(E, dtype=I32), E)
    later = jnp.concatenate([lax.cummin(cand[::-1])[::-1][1:], jnp.full((1,), E, I32)])
    nxt = jnp.where(later < E, later, -1)[blk_e]
    plan = jnp.stack([blk_e, first.astype(I32), slot.astype(I32), nxt.astype(I32)])
    fill_start = (pstart + counts).astype(I32)
    fill_n = (padded - counts).astype(I32)
    return plan, n_used, dest8.astype(I32), fill_start, fill_n, n_blocks * R


def kernel(x, positions, ln_in_g, ln_in_b, w_in, b_gate, lam_q1, lam_k1, lam_q2, lam_k2, subln_g, conv_w, conv_b, w_rg_a, b_rg_a, w_rg_x, b_rg_x, lru_lambda, w_proj_attn, w_proj_rnn, w_out, ln1_g, ln1_b, w_router, router_bias, w_exp_gate, w_exp_up, w_exp_down, w_sh_gate, w_sh_up, w_sh_down, ln2_g, ln2_b):
    B, S, D = x.shape
    T = B * S
    depth = w_in.shape[0]
    alpha = (2 * depth) ** 0.25
    dh = lam_q1.shape[-1]
    att_v = w_proj_attn.shape[1]
    d_rnn = conv_w.shape[-1]
    att_qk = (w_in.shape[2] - att_v - 2 * d_rnn - 2 * D) // 2
    heads = att_v // (2 * dh)
    assert dh == LANES and att_qk == att_v
    cosf, sinf, half = _rope_tables(positions, dh)

    hf, hb = _ln_in(x.reshape(T, D), ln_in_g, ln_in_b)
    for l in range(depth):
        lambda_init = 0.8 - 0.6 * math.exp(-0.3 * l)
        lam = (jnp.exp(jnp.sum(lam_q1[l] * lam_k1[l])) - jnp.exp(jnp.sum(lam_q2[l] * lam_k2[l]))
               + lambda_init).reshape(1).astype(F32)
        proj = _in_proj(hb, w_in[l].astype(BF16), cosf, sinf, att_qk=att_qk,
                        q_scale=dh ** -0.5 * LOG2E, half=half)
        attn = _attention(proj, lam, subln_g[l], B=B, S=S, heads=heads, dh=dh, att_qk=att_qk,
                          lambda_init=lambda_init)
        rnn = _rglru(proj, conv_w[l], conv_b[l], w_rg_a[l], b_rg_a[l], w_rg_x[l], b_rg_x[l], lru_lambda[l],
                     B=B, S=S, off_xr=2 * att_qk + att_v)
        mixed = _mix(attn, rnn, w_proj_attn[l].astype(BF16), w_proj_rnn[l].astype(BF16), proj, b_gate[l],
                     off_g=2 * att_qk + att_v + 2 * d_rnn)
        h1f, h1p = _out_ln(mixed, w_out[l].astype(BF16), hf, ln1_g[l], ln1_b[l], alpha=alpha)
        w8, idx, pos, cnt = _router(h1f, w_router[l], router_bias[l])
        plan, n_used, dest8, fill_start, fill_n, n_rows = _dispatch_plan(idx, pos, cnt)
        xs = _dispatch(h1p, dest8, fill_start, fill_n, n_rows)
        ys = _experts(xs, plan, n_used, w_exp_gate[l], w_exp_up[l], w_exp_down[l])
        hf = _final(dest8, w8[:, :TOP_K], ys, h1f, h1p, w_sh_gate[l].astype(BF16), w_sh_up[l].astype(BF16),
                    w_sh_down[l].astype(BF16), ln2_g[l], ln2_b[l], alpha=alpha)
        if l + 1 < depth:
            hb = hf.astype(BF16)
    return hf.reshape(B, S, D)
```

```python
import functools
import math

import jax
import jax.numpy as jnp
from jax import lax
from jax.experimental import pallas as pl
from jax.experimental.pallas import tpu as pltpu

F32 = jnp.float32
BF16 = jnp.bfloat16
I32 = jnp.int32
U32 = jnp.uint32

CHUNK = 64
ROPE_THETA = 500000.0
ROT_FRACTION = 4
LRU_C = 8.0
CONV_WIDTH = 4
N_GROUPS = 8
TOPK_GROUPS = 4
TOP_K = 8
ROUTED_SCALE = 2.5
LN_EPS = 1e-5
LOG2E = 1.4426950408889634

LANES = 128
SUBLANES = 8
VMEM_LIMIT_BYTES = 56 * 1024 * 1024
EXPERT_ROWS = 256
DMA_UNROLL = 8


def _tile(n, pref):
    t = min(n, pref)
    while n % t:
        t //= 2
    return t


def _params(*sem):
    return pltpu.CompilerParams(dimension_semantics=sem, vmem_limit_bytes=VMEM_LIMIT_BYTES)


def _ln_rows(x, g, b):
    mu = jnp.mean(x, -1, keepdims=True)
    xc = x - mu
    var = jnp.mean(xc * xc, -1, keepdims=True)
    return xc * lax.rsqrt(var + LN_EPS) * g + b


def _silu(x):
    return x * jax.nn.sigmoid(x)


def _pack_halves(y):
    half = y.shape[1] // 2
    bits = lax.bitcast_convert_type(y.astype(BF16).astype(F32), U32)
    return (bits[:, :half] >> 16) | (bits[:, half:] & jnp.uint32(0xFFFF0000))


def _unpack_halves(p):
    lo = lax.bitcast_convert_type(p << 16, F32)
    hi = lax.bitcast_convert_type(p & jnp.uint32(0xFFFF0000), F32)
    return lo, hi


def _ln_in_kernel(x_ref, g_ref, b_ref, yf_ref, yb_ref):
    y = _ln_rows(x_ref[...], g_ref[...], b_ref[...])
    yf_ref[...] = y
    yb_ref[...] = y.astype(BF16)


def _ln_in(x2, g, b):
    T, D = x2.shape
    tm = _tile(T, 256)
    row = pl.BlockSpec((tm, D), lambda i: (i, 0))
    vec = pl.BlockSpec((1, D), lambda i: (0, 0))
    return pl.pallas_call(
        _ln_in_kernel, grid=(T // tm,),
        in_specs=[row, vec, vec], out_specs=[row, row],
        out_shape=[jax.ShapeDtypeStruct((T, D), F32), jax.ShapeDtypeStruct((T, D), BF16)],
        compiler_params=_params("parallel"), name="ln_in",
    )(x2, g.reshape(1, D), b.reshape(1, D))


def _in_proj_kernel(a_ref, w_ref, cos_ref, sin_ref, o_ref, *, n_q, n_qk, q_scale, half, n_chunks):
    j = pl.program_id(1)
    tm, tn = o_ref.shape
    cn = tn // n_chunks
    scale = jnp.where(j < n_q, q_scale, 1.0)
    cosf = jnp.where(j < n_qk, cos_ref[...], 1.0) * scale
    sinf = jnp.where(j < n_qk, sin_ref[...], 0.0) * scale
    low = lax.broadcasted_iota(I32, (tm, LANES), 1) < half
    a = a_ref[...]
    for c in range(n_chunks):
        acc = jnp.dot(a, w_ref[:, c * cn:(c + 1) * cn], preferred_element_type=F32)
        for g in range(cn // LANES):
            x = acc[:, g * LANES:(g + 1) * LANES]
            partner = jnp.where(low, pltpu.roll(x, LANES - half, 1), pltpu.roll(x, half, 1))
            col = c * cn + g * LANES
            o_ref[:, col:col + LANES] = (x * cosf + partner * sinf).astype(BF16)


def _in_proj(h0b, w, cosf, sinf, *, att_qk, q_scale, half):
    T, K = h0b.shape
    N = w.shape[1]
    tm = _tile(T, 1024)
    tn = _tile(att_qk, 1024)
    assert N % tn == 0
    kern = functools.partial(_in_proj_kernel, n_q=att_qk // tn, n_qk=2 * att_qk // tn,
                             q_scale=q_scale, half=half, n_chunks=max(1, tn // 256))
    return pl.pallas_call(
        kern, grid=(T // tm, N // tn),
        in_specs=[pl.BlockSpec((tm, K), lambda i, j: (i, 0)),
                  pl.BlockSpec((K, tn), lambda i, j: (0, j)),
                  pl.BlockSpec((tm, LANES), lambda i, j: (i, 0)),
                  pl.BlockSpec((tm, LANES), lambda i, j: (i, 0))],
        out_specs=pl.BlockSpec((tm, tn), lambda i, j: (i, j)),
        out_shape=jax.ShapeDtypeStruct((T, N), BF16),
        compiler_params=_params("parallel", "arbitrary"), name="in_proj",
    )(h0b, w, cosf, sinf)


def _loop_by_pairs(n, body, carry):
    def pair(c2, carry):
        return body(2 * c2 + 1, body(2 * c2, carry))
    carry = lax.fori_loop(0, n // 2, pair, carry)
    return lax.cond(n % 2 == 1, lambda cr: body(n - 1, cr), lambda cr: cr, carry)


def _attn_kernel(lam_ref, q_ref, k_ref, v_ref, g_ref, o_ref, qt_s, vt_s, st_s, acc_s, *, dh, tk, out_scale):
    i = pl.program_id(2)
    tq = q_ref.shape[0]
    S, wv = v_ref.shape

    @pl.when(i == 0)
    def _():
        for c in range(S // tk):
            vt_s[c, :wv] = v_ref[c * tk:(c + 1) * tk, :].astype(F32).T.astype(BF16)
            vt_s[c, wv:] = jnp.ones((vt_s.shape[1] - wv, tk), BF16)

    qt_s[...] = q_ref[...].astype(F32).T.astype(BF16)

    def scores(c, m):
        r0 = pl.multiple_of(c * tk, tk)
        return jnp.dot(k_ref[pl.ds(r0, tk), m * dh:(m + 1) * dh], qt_s[m * dh:(m + 1) * dh, :],
                       preferred_element_type=F32)

    def phase1(c, mx):
        out = []
        for m in range(2):
            st = scores(c, m)
            st_s[c, m] = st
            out.append(jnp.maximum(mx[m], jnp.max(st, 0, keepdims=True)))
        return tuple(out)

    mx = _loop_by_pairs(i, phase1, tuple(jnp.full((1, tq), -jnp.inf, F32) for _ in range(2)))
    visible = ((lax.broadcasted_iota(I32, (tk, tq), 0) // CHUNK)
               <= (lax.broadcasted_iota(I32, (tk, tq), 1) // CHUNK))
    mxs = []
    for m in range(2):
        st = jnp.where(visible, scores(i, m), -jnp.inf)
        st_s[i, m] = st
        mxs.append(jnp.maximum(mx[m], jnp.max(st, 0, keepdims=True)))

    acc_s[...] = jnp.zeros_like(acc_s)

    def phase2(c, carry):
        vt = vt_s[c]
        for m in range(2):
            p = jnp.exp2(st_s[c, m] - mxs[m])
            acc_s[m] += jnp.dot(vt, p.astype(BF16), preferred_element_type=F32)
        return carry

    _loop_by_pairs(i + 1, phase2, 0)
    o0, o1 = (acc_s[m, :wv] * (1.0 / acc_s[m, wv:wv + 1]) for m in range(2))
    ot = o0 - lam_ref[0] * o1
    ot = ot * lax.rsqrt(jnp.mean(ot * ot, 0, keepdims=True) + LN_EPS)
    o_ref[...] = (ot.T * (g_ref[...] * out_scale)).astype(BF16)


def _attention(proj, lam, subln_g, *, B, S, heads, dh, att_qk, lambda_init):
    T = B * S
    wv = 2 * dh
    tq = _tile(S, 512)
    tk = tq
    nq = S // tq
    kern = functools.partial(_attn_kernel, dh=dh, tk=tk, out_scale=1.0 - lambda_init)
    ones_rows = 2 * SUBLANES
    scratch = [pltpu.VMEM((wv, tq), BF16), pltpu.VMEM((S // tk, wv + ones_rows, tk), BF16),
               pltpu.VMEM((S // tk, 2, tk, tq), F32), pltpu.VMEM((2, wv + ones_rows, tq), F32)]
    return pl.pallas_call(
        kern, grid=(B, heads, nq),
        in_specs=[pl.BlockSpec(memory_space=pltpu.SMEM),
                  pl.BlockSpec((tq, wv), lambda b, h, i: (b * nq + i, h)),
                  pl.BlockSpec((S, wv), lambda b, h, i: (b, att_qk // wv + h)),
                  pl.BlockSpec((S, wv), lambda b, h, i: (b, 2 * att_qk // wv + h)),
                  pl.BlockSpec((1, wv), lambda b, h, i: (0, 0))],
        out_specs=pl.BlockSpec((tq, wv), lambda b, h, i: (b * nq + i, h)),
        out_shape=jax.ShapeDtypeStruct((T, heads * wv), BF16),
        scratch_shapes=scratch,
        compiler_params=_params("parallel", "parallel", "arbitrary"), name="attention",
    )(lam, proj, proj, proj, subln_g.reshape(1, wv))


def _gelu_tanh(x):
    return 0.5 * x * (1.0 + jnp.tanh(0.7978845608028654 * (x + 0.044715 * (x * x * x))))


def _rglru_kernel(xr_ref, yr_ref, cw_ref, cb_ref, wa_ref, ba_ref, wx_ref, bx_ref, lam_ref, o_ref,
                  a_s, u_s, xprev_s, h_s, *, n_heads, blk, scan_cols):
    @pl.when(pl.program_id(1) == 0)
    def _():
        xprev_s[...] = jnp.zeros_like(xprev_s)
        h_s[...] = jnp.zeros_like(h_s)

    tt, D = xr_ref.shape
    z = -lam_ref[...]
    sp = jnp.maximum(z, 0.0) + jnp.log1p(jnp.exp(-jnp.abs(z)))
    for hh in range(n_heads):
        sl = slice(hh * blk, (hh + 1) * blk)
        xe = jnp.concatenate([xprev_s[:, sl], xr_ref[:, sl].astype(F32)], axis=0)
        xc = cb_ref[:, sl]
        for j in range(CONV_WIDTH):
            o = SUBLANES - (CONV_WIDTH - 1) + j
            xc = xc + cw_ref[j:j + 1, sl] * xe[o:o + tt]
        xb = xc.astype(BF16)
        r = jax.nn.sigmoid(jnp.dot(xb, wa_ref[hh], preferred_element_type=F32) + ba_ref[:, sl])
        ig = jax.nn.sigmoid(jnp.dot(xb, wx_ref[hh], preferred_element_type=F32) + bx_ref[:, sl])
        log_a = (-LRU_C * r) * sp[:, sl]
        a = jnp.exp(log_a)
        a_s[:, sl] = a
        u_s[:, sl] = jnp.sqrt(-jnp.tanh(log_a) * (a * a + 1.0)) * ig * xc
    xprev_s[...] = xr_ref[tt - SUBLANES:tt, :].astype(F32)

    row = lax.broadcasted_iota(I32, (SUBLANES, scan_cols), 0)

    def scan_body(g, h):
        r0 = pl.multiple_of(g * SUBLANES, SUBLANES)
        outs = []
        for c in range(D // scan_cols):
            cs = slice(c * scan_cols, (c + 1) * scan_cols)
            a = a_s[pl.ds(r0, SUBLANES), cs]
            u = u_s[pl.ds(r0, SUBLANES), cs]
            for s in (1, 2, 4):
                valid = row >= s
                u = jnp.where(valid, a * pltpu.roll(u, s, 0) + u, u)
                a = jnp.where(valid, a * pltpu.roll(a, s, 0), a)
            hg = u + a * h[:, cs]
            u_s[pl.ds(r0, SUBLANES), cs] = hg
            outs.append(hg[SUBLANES - 1:SUBLANES, :])
        return jnp.concatenate(outs, axis=1)

    h_s[...] = lax.fori_loop(0, tt // SUBLANES, scan_body, h_s[...])
    o_ref[...] = (u_s[...] * _gelu_tanh(yr_ref[...].astype(F32))).astype(BF16)


def _rglru(proj, conv_w, conv_b, w_ra, b_ra, w_rx, b_rx, lru_lambda, *, B, S, off_xr):
    T = B * S
    n_heads, blk, _ = w_ra.shape
    D = n_heads * blk
    tt = _tile(S, 256)
    nt = S // tt
    assert off_xr % D == 0
    cx = off_xr // D
    vec = pl.BlockSpec((1, D), lambda b, t: (0, 0))
    wspec = pl.BlockSpec((n_heads, blk, blk), lambda b, t: (0, 0, 0))
    kern = functools.partial(_rglru_kernel, n_heads=n_heads, blk=blk, scan_cols=_tile(D, 512))
    return pl.pallas_call(
        kern, grid=(B, nt),
        in_specs=[pl.BlockSpec((tt, D), lambda b, t: (b * nt + t, cx)),
                  pl.BlockSpec((tt, D), lambda b, t: (b * nt + t, cx + 1)),
                  pl.BlockSpec((CONV_WIDTH, D), lambda b, t: (0, 0)), vec,
                  wspec, vec, wspec, vec, vec],
        out_specs=pl.BlockSpec((tt, D), lambda b, t: (b * nt + t, 0)),
        out_shape=jax.ShapeDtypeStruct((T, D), BF16),
        scratch_shapes=[pltpu.VMEM((tt, D), F32), pltpu.VMEM((tt, D), F32),
                        pltpu.VMEM((SUBLANES, D), F32), pltpu.VMEM((1, D), F32)],
        compiler_params=_params("parallel", "arbitrary"), name="rglru",
    )(proj, proj, conv_w, conv_b.reshape(1, D), w_ra.astype(BF16), b_ra.reshape(1, D),
      w_rx.astype(BF16), b_rx.reshape(1, D), lru_lambda.reshape(1, D))


def _mix_kernel(at_ref, rn_ref, wa_ref, wr_ref, g0_ref, g1_ref, b0_ref, b1_ref, o_ref):
    pa = jnp.dot(at_ref[...], wa_ref[...], preferred_element_type=F32)
    pr = jnp.dot(rn_ref[...], wr_ref[...], preferred_element_type=F32)
    s0 = jax.nn.sigmoid(g0_ref[...].astype(F32) + b0_ref[...])
    s1 = jax.nn.sigmoid(g1_ref[...].astype(F32) + b1_ref[...])
    o_ref[...] = (s0 * pa + s1 * pr).astype(BF16)


def _mix(attn, rnn, wa, wr, proj, b_gate, *, off_g):
    T, Ka = attn.shape
    Kr = rnn.shape[1]
    D = wa.shape[1]
    tm = _tile(T, 512)
    tn = _tile(D, 512)
    assert off_g % tn == 0
    cg = off_g // tn
    nn = D // tn
    return pl.pallas_call(
        _mix_kernel, grid=(T // tm, nn),
        in_specs=[pl.BlockSpec((tm, Ka), lambda i, j: (i, 0)),
                  pl.BlockSpec((tm, Kr), lambda i, j: (i, 0)),
                  pl.BlockSpec((Ka, tn), lambda i, j: (0, j)),
                  pl.BlockSpec((Kr, tn), lambda i, j: (0, j)),
                  pl.BlockSpec((tm, tn), lambda i, j: (i, cg + j)),
                  pl.BlockSpec((tm, tn), lambda i, j: (i, cg + nn + j)),
                  pl.BlockSpec((1, tn), lambda i, j: (0, j)),
                  pl.BlockSpec((1, tn), lambda i, j: (0, nn + j))],
        out_specs=pl.BlockSpec((tm, tn), lambda i, j: (i, j)),
        out_shape=jax.ShapeDtypeStruct((T, D), BF16),
        compiler_params=_params("parallel", "arbitrary"), name="mix",
    )(attn, rnn, wa, wr, proj, proj, b_gate.reshape(1, 2 * D), b_gate.reshape(1, 2 * D))


def _out_ln_kernel(a_ref, w_ref, h0_ref, g_ref, b_ref, hf_ref, hp_ref, acc_s, *, alpha):
    j = pl.program_id(1)
    n_n, _, tn = acc_s.shape
    acc_s[j] = alpha * h0_ref[...] + jnp.dot(a_ref[...], w_ref[...], preferred_element_type=F32)

    @pl.when(j == n_n - 1)
    def _():
        D = n_n * tn
        mu = sum(jnp.sum(acc_s[c], -1, keepdims=True) for c in range(n_n)) / D
        var = sum(jnp.sum(jnp.square(acc_s[c] - mu), -1, keepdims=True) for c in range(n_n)) / D
        rstd = lax.rsqrt(var + LN_EPS)
        hn = n_n // 2
        for c in range(hn):
            ys = []
            for cc in (c, c + hn):
                cs = slice(cc * tn, (cc + 1) * tn)
                y = (acc_s[cc] - mu) * rstd * g_ref[:, cs] + b_ref[:, cs]
                hf_ref[:, cs] = y
                ys.append(y)
            hp_ref[:, c * tn:(c + 1) * tn] = _pack_halves(jnp.concatenate(ys, axis=1))


def _out_ln(mixed, w, h0f, g, b, *, alpha):
    T, K = mixed.shape
    D = w.shape[1]
    tm = _tile(T, 512)
    tn = _tile(D // 2, 512)
    vec = pl.BlockSpec((1, D), lambda i, j: (0, 0))
    return pl.pallas_call(
        functools.partial(_out_ln_kernel, alpha=alpha), grid=(T // tm, D // tn),
        in_specs=[pl.BlockSpec((tm, K), lambda i, j: (i, 0), pipeline_mode=pl.Buffered(1)),
                  pl.BlockSpec((K, tn), lambda i, j: (0, j)),
                  pl.BlockSpec((tm, tn), lambda i, j: (i, j)), vec, vec],
        out_specs=[pl.BlockSpec((tm, D), lambda i, j: (i, 0)),
                   pl.BlockSpec((tm, D // 2), lambda i, j: (i, 0))],
        out_shape=[jax.ShapeDtypeStruct((T, D), F32), jax.ShapeDtypeStruct((T, D // 2), U32)],
        scratch_shapes=[pltpu.VMEM((D // tn, tm, tn), F32)],
        compiler_params=_params("parallel", "arbitrary"), name="out_ln",
    )(mixed, w, h0f, g.reshape(1, D), b.reshape(1, D))


def _seg_allreduce(x, lane, width, op):
    n = x.shape[1]
    s = 1
    while s < width:
        partner = jnp.where((lane & s) == 0, pltpu.roll(x, n - s, 1), pltpu.roll(x, s, 1))
        x = op(x, partner)
        s *= 2
    return x


def _router_kernel(h_ref, w_ref, bias_ref, w8_ref, idx_ref, pos_ref, cnt_ref, tri_s, run_s):
    @pl.when(pl.program_id(0) == 0)
    def _():
        tm = tri_s.shape[0]
        tri_s[...] = jnp.where(lax.broadcasted_iota(I32, (tm, tm), 1) < lax.broadcasted_iota(I32, (tm, tm), 0),
                               1.0, 0.0).astype(BF16)
        run_s[...] = jnp.zeros_like(run_s)

    h = h_ref[...]
    w = w_ref[...]
    h_hi = h.astype(BF16)
    h_lo = (h - h_hi.astype(F32)).astype(BF16)
    w_hi = w.astype(BF16)
    w_lo = (w - w_hi.astype(F32)).astype(BF16)
    logits = (jnp.dot(h_hi, w_hi, preferred_element_type=F32)
              + (jnp.dot(h_hi, w_lo, preferred_element_type=F32)
                 + jnp.dot(h_lo, w_hi, preferred_element_type=F32)))
    scores = jax.nn.sigmoid(logits)
    biased = scores + bias_ref[...]
    E = biased.shape[1]
    gw = E // N_GROUPS
    lane = lax.broadcasted_iota(I32, biased.shape, 1)
    neg = -jnp.inf

    m1 = _seg_allreduce(biased, lane, gw, jnp.maximum)
    is_m1 = biased == m1
    cnt = _seg_allreduce(is_m1.astype(F32), lane, gw, jnp.add)
    m2 = jnp.where(cnt >= 2.0, m1, _seg_allreduce(jnp.where(is_m1, neg, biased), lane, gw, jnp.maximum))
    gs = m1 + m2

    gi = lane // gw
    rank = jnp.zeros(biased.shape, I32)
    for k in range(1, N_GROUPS):
        other = pltpu.roll(gs, k * gw, 1)
        ogi = (gi - k) % N_GROUPS
        beats = (other > gs) | ((other == gs) & (ogi < gi))
        rank = rank + beats.astype(I32)
    cur = jnp.where(rank < TOPK_GROUPS, biased, neg)

    sel = jnp.zeros(biased.shape, jnp.bool_)
    firsts = []
    for k in range(TOP_K):
        mx = jnp.max(cur, -1, keepdims=True)
        first = jnp.min(jnp.where(cur == mx, lane, E), -1, keepdims=True)
        pick = lane == first
        sel = sel | pick
        firsts.append(first)
        cur = jnp.where(pick, neg, cur)

    sel_f = jnp.where(sel, 1.0, 0.0)
    rank_in_e = run_s[...] + jnp.dot(tri_s[...], sel_f.astype(BF16), preferred_element_type=F32)
    run_s[...] = run_s[...] + jnp.sum(sel_f, 0, keepdims=True)
    cnt_ref[...] = run_s[...]

    wsum = jnp.sum(jnp.where(sel, scores, 0.0), -1, keepdims=True)
    w8 = jnp.zeros(biased.shape, F32)
    idx = jnp.zeros(biased.shape, I32)
    pos = jnp.zeros(biased.shape, F32)
    for k in range(TOP_K):
        pick = lane == firsts[k]
        wk = jnp.sum(jnp.where(pick, scores, 0.0), -1, keepdims=True)
        pk = jnp.sum(jnp.where(pick, rank_in_e, 0.0), -1, keepdims=True)
        w8 = jnp.where(lane == k, wk / wsum * ROUTED_SCALE, w8)
        idx = jnp.where(lane == k, firsts[k], idx)
        pos = jnp.where(lane == k, pk, pos)
    w8_ref[...] = w8
    idx_ref[...] = idx
    pos_ref[...] = pos.astype(I32)


def _router(h1f, w_router, router_bias):
    T, D = h1f.shape
    E = w_router.shape[1]
    assert E == LANES and E % N_GROUPS == 0
    tm = _tile(T, 512)
    row = pl.BlockSpec((tm, E), lambda i: (i, 0))
    return pl.pallas_call(
        _router_kernel, grid=(T // tm,),
        in_specs=[pl.BlockSpec((tm, D), lambda i: (i, 0)),
                  pl.BlockSpec((D, E), lambda i: (0, 0)),
                  pl.BlockSpec((1, E), lambda i: (0, 0))],
        out_specs=[row, row, row, pl.BlockSpec((1, E), lambda i: (0, 0))],
        out_shape=[jax.ShapeDtypeStruct((T, E), F32), jax.ShapeDtypeStruct((T, E), I32),
                   jax.ShapeDtypeStruct((T, E), I32), jax.ShapeDtypeStruct((1, E), F32)],
        scratch_shapes=[pltpu.VMEM((tm, tm), BF16), pltpu.VMEM((1, E), F32)],
        compiler_params=_params("arbitrary"), name="router",
    )(h1f, w_router, router_bias.reshape(1, E))


def _dispatch_kernel(fill_start_ref, fill_n_ref, d8_ref, hp_ref, xs_hbm, zero_s, sem, zsem):
    tm = hp_ref.shape[0]
    E = fill_n_ref.shape[0]

    def row_copy(r, k):
        return pltpu.make_async_copy(hp_ref.at[pl.ds(r, 1), :],
                                     xs_hbm.at[pl.ds(d8_ref[0, 0, r * TOP_K + k], 1), :], sem.at[0])

    def issue(r, c):
        for k in range(TOP_K):
            row_copy(r, k).start()
        return c
    lax.fori_loop(0, tm, issue, 0)

    @pl.when(pl.program_id(0) == 0)
    def _():
        zero_s[...] = jnp.zeros_like(zero_s)

        def zcopy(row):
            return pltpu.make_async_copy(zero_s, xs_hbm.at[pl.ds(row, 1), :], zsem.at[0])

        def per_expert(e, c):
            s0 = fill_start_ref[e]
            n = fill_n_ref[e]

            def zi(r, c2):
                zcopy(s0 + r).start()
                return c2
            lax.fori_loop(0, n, zi, 0)

            def zw(r, c2):
                zcopy(s0 + r).wait()
                return c2
            lax.fori_loop(0, n, zw, 0)
            return c
        lax.fori_loop(0, E, per_expert, 0)

    def wait(r, c):
        for k in range(TOP_K):
            row_copy(r, k).wait()
        return c
    lax.fori_loop(0, tm, wait, 0)


def _dispatch(h1p, dest8, fill_start, fill_n, n_rows):
    T, W = h1p.shape
    tm = _tile(T, 512)
    nt = T // tm
    d8 = dest8.reshape(nt, 1, tm * TOP_K)
    grid_spec = pltpu.PrefetchScalarGridSpec(
        num_scalar_prefetch=2, grid=(nt,),
        in_specs=[pl.BlockSpec((1, 1, tm * TOP_K), lambda i, fs, fn: (i, 0, 0), memory_space=pltpu.SMEM),
                  pl.BlockSpec((tm, W), lambda i, fs, fn: (i, 0))],
        out_specs=pl.BlockSpec(memory_space=pl.ANY),
        scratch_shapes=[pltpu.VMEM((1, W), U32), pltpu.SemaphoreType.DMA((1,)), pltpu.SemaphoreType.DMA((1,))])
    return pl.pallas_call(
        _dispatch_kernel, grid_spec=grid_spec,
        out_shape=jax.ShapeDtypeStruct((n_rows, W), U32),
        compiler_params=_params("arbitrary"), name="dispatch",
    )(fill_start, fill_n, d8, h1p)


def _expert_switch(plan_ref, i, w_hbm, wbuf, sem, recast):
    def copies(e, slot):
        return [pltpu.make_async_copy(w.at[e], wbuf.at[slot, t], sem.at[slot, t]) for t, w in enumerate(w_hbm)]

    @pl.when(plan_ref[1, i] == 1)
    def _():
        e, slot, nxt = plan_ref[0, i], plan_ref[2, i], plan_ref[3, i]

        @pl.when(i == 0)
        def _():
            for c in copies(e, slot):
                c.start()

        for c in copies(e, slot):
            c.wait()
        recast(slot)

        @pl.when(nxt >= 0)
        def _():
            for c in copies(nxt, 1 - slot):
                c.start()


def _experts_up_kernel(plan_ref, n_used_ref, xs_ref, wg_hbm, wu_hbm, hb_ref, wbuf, wgu_s, sem):
    i = pl.program_id(0)
    F = wg_hbm.shape[2]

    @pl.when(i < n_used_ref[0])
    def _():
        def recast(slot):
            wgu_s[:, :F] = wbuf[slot, 0].astype(BF16)
            wgu_s[:, F:] = wbuf[slot, 1].astype(BF16)
        _expert_switch(plan_ref, i, (wg_hbm, wu_hbm), wbuf, sem, recast)

        lo, hi = _unpack_halves(xs_ref[...])
        hw = lo.shape[1]
        gu = (jnp.dot(lo.astype(BF16), wgu_s[:hw, :], preferred_element_type=F32)
              + jnp.dot(hi.astype(BF16), wgu_s[hw:, :], preferred_element_type=F32))
        hb_ref[...] = (_silu(gu[:, :F]) * gu[:, F:]).astype(BF16)

    @pl.when(i >= n_used_ref[0])
    def _():
        hb_ref[...] = jnp.zeros_like(hb_ref)


def _experts_down_kernel(plan_ref, n_used_ref, hb_ref, wd_hbm, ys_ref, wbuf, wd_s, sem):
    i = pl.program_id(0)

    @pl.when(i < n_used_ref[0])
    def _():
        def recast(slot):
            wd_s[...] = wbuf[slot, 0].astype(BF16)
        _expert_switch(plan_ref, i, (wd_hbm,), wbuf, sem, recast)

        ys_ref[...] = _pack_halves(jnp.dot(hb_ref[...], wd_s[...], preferred_element_type=F32))

    @pl.when(i >= n_used_ref[0])
    def _():
        ys_ref[...] = jnp.zeros_like(ys_ref)


def _experts(xs, plan, n_used, wg, wu, wd):
    n_rows, W = xs.shape
    E, D, F = wg.shape
    R = EXPERT_ROWS
    n_blocks = n_rows // R
    xrow = lambda i, pn, nu: (jnp.minimum(i, jnp.maximum(nu[0] - 1, 0)), 0)
    hbm = pl.BlockSpec(memory_space=pl.ANY)
    hb = pl.pallas_call(
        _experts_up_kernel,
        grid_spec=pltpu.PrefetchScalarGridSpec(
            num_scalar_prefetch=2, grid=(n_blocks,),
            in_specs=[pl.BlockSpec((R, W), xrow), hbm, hbm],
            out_specs=pl.BlockSpec((R, F), lambda i, pn, nu: (i, 0)),
            scratch_shapes=[pltpu.VMEM((2, 2, D, F), F32), pltpu.VMEM((D, 2 * F), BF16),
                            pltpu.SemaphoreType.DMA((2, 2))]),
        out_shape=jax.ShapeDtypeStruct((n_rows, F), BF16),
        compiler_params=_params("arbitrary"), name="experts_up",
    )(plan, n_used, xs, wg, wu)
    return pl.pallas_call(
        _experts_down_kernel,
        grid_spec=pltpu.PrefetchScalarGridSpec(
            num_scalar_prefetch=2, grid=(n_blocks,),
            in_specs=[pl.BlockSpec((R, F), lambda i, pn, nu: (i, 0)), hbm],
            out_specs=pl.BlockSpec((R, W), lambda i, pn, nu: (i, 0)),
            scratch_shapes=[pltpu.VMEM((2, 1, F, D), F32), pltpu.VMEM((F, D), BF16),
                            pltpu.SemaphoreType.DMA((2, 1))]),
        out_shape=jax.ShapeDtypeStruct((n_rows, W), U32),
        compiler_params=_params("arbitrary"), name="experts_down",
    )(plan, n_used, hb, wd)


def _final_kernel(d8a_ref, d8b_ref, d8n_ref, ys_hbm, hf_ref, hp_ref, w8_ref, wsg_ref, wsu_ref, wsd_ref, g_ref, b_ref,
                  o_ref, buf_a, buf_b, sem, *, alpha):
    i = pl.program_id(0)
    nt = pl.num_programs(0)
    tm = hf_ref.shape[0] // 2
    n = tm * TOP_K

    def issue(d8_ref, rows, buf, s):
        for r in rows:
            pltpu.make_async_copy(ys_hbm.at[pl.ds(d8_ref[0, 0, r], 1), :],
                                  buf.at[r // SUBLANES, pl.ds(r % SUBLANES, 1), :], sem.at[s]).start()

    def wait_all(buf, s):
        def wait(g, c):
            for sub in range(SUBLANES):
                pltpu.make_async_copy(ys_hbm.at[pl.ds(0, 1), :], buf.at[g, pl.ds(sub, 1), :], sem.at[s]).wait()
            return c
        lax.fori_loop(0, n // SUBLANES, wait, 0)

    @pl.when(i == 0)
    def _():
        def first(g, c):
            for sub in range(SUBLANES):
                pltpu.make_async_copy(ys_hbm.at[pl.ds(d8a_ref[0, 0, g * SUBLANES + sub], 1), :],
                                      buf_a.at[g, pl.ds(sub, 1), :], sem.at[0]).start()
            return c
        lax.fori_loop(0, n // SUBLANES, first, 0)

    def shared(rows):
        lo, hi = _unpack_halves(hp_ref[rows, :])
        x = jnp.concatenate([lo.astype(BF16), hi.astype(BF16)], axis=1)
        hs = (_silu(jnp.dot(x, wsg_ref[...], preferred_element_type=F32))
              * jnp.dot(x, wsu_ref[...], preferred_element_type=F32)).astype(BF16)
        return alpha * hf_ref[rows, :] + jnp.dot(hs, wsd_ref[...], preferred_element_type=F32)

    def combine(rows, acc, buf):
        r_lo = r_hi = None
        gk = tm // SUBLANES
        for k in range(TOP_K):
            lo, hi = _unpack_halves(buf[k * gk:(k + 1) * gk].reshape(tm, buf.shape[-1]))
            wk = w8_ref[rows, k:k + 1]
            r_lo = lo * wk if r_lo is None else r_lo + lo * wk
            r_hi = hi * wk if r_hi is None else r_hi + hi * wk
        o_ref[rows, :] = _ln_rows(acc + jnp.concatenate([r_lo, r_hi], axis=1), g_ref[...], b_ref[...])

    rows_a, rows_b = slice(0, tm), slice(tm, 2 * tm)
    issue(d8b_ref, range(0, n // 2), buf_b, 1)
    acc = shared(rows_a)
    wait_all(buf_a, 0)
    issue(d8b_ref, range(n // 2, n), buf_b, 1)
    combine(rows_a, acc, buf_a)
    issue(d8n_ref, range(0, n // 2), buf_a, 0)
    acc = shared(rows_b)
    wait_all(buf_b, 1)
    issue(d8n_ref, range(n // 2, n), buf_a, 0)
    combine(rows_b, acc, buf_b)

    @pl.when(i == nt - 1)
    def _():
        wait_all(buf_a, 0)


def _final(dest8, w8, ys, h1f, h1p, wsg, wsu, wsd, g, b, *, alpha):
    T, D = h1f.shape
    W = ys.shape[1]
    F = wsg.shape[1]
    tm = _tile(T // 2, 128)
    nt = T // (2 * tm)
    d8 = dest8.reshape(2 * nt, tm, TOP_K).transpose(0, 2, 1).reshape(2 * nt, 1, TOP_K * tm)
    row = lambda w: pl.BlockSpec((2 * tm, w), lambda i: (i, 0))
    vec = pl.BlockSpec((1, D), lambda i: (0, 0))
    idx_spec = lambda f: pl.BlockSpec((1, 1, TOP_K * tm), lambda i: (f(i), 0, 0), memory_space=pltpu.SMEM)
    wspec = lambda shape: pl.BlockSpec(shape, lambda i: (0, 0), pipeline_mode=pl.Buffered(1))
    gbuf = pltpu.VMEM((TOP_K * tm // SUBLANES, SUBLANES, W), U32)
    return pl.pallas_call(
        functools.partial(_final_kernel, alpha=alpha), grid=(nt,),
        in_specs=[idx_spec(lambda i: 2 * i), idx_spec(lambda i: 2 * i + 1),
                  idx_spec(lambda i: jnp.minimum(2 * i + 2, 2 * nt - 2)),
                  pl.BlockSpec(memory_space=pl.ANY), row(D), row(W), row(TOP_K),
                  wspec((D, F)), wspec((D, F)), wspec((F, D)), vec, vec],
        out_specs=row(D),
        out_shape=jax.ShapeDtypeStruct((T, D), F32),
        scratch_shapes=[gbuf, gbuf, pltpu.SemaphoreType.DMA((2,))],
        compiler_params=_params("arbitrary"), name="final",
    )(d8, d8, d8, ys, h1f, h1p, w8, wsg, wsu, wsd, g.reshape(1, D), b.reshape(1, D))


def _rope_tables(positions, dh):
    rot = dh // ROT_FRACTION
    half = rot // 2
    inv_freq = jnp.power(ROPE_THETA, -jnp.arange(0, rot, 2, dtype=F32) / rot)
    ang = positions.astype(F32).reshape(-1, 1) * inv_freq
    cos, sin = jnp.cos(ang), jnp.sin(ang)
    T = ang.shape[0]
    cosf = jnp.concatenate([cos, cos, jnp.ones((T, dh - rot), F32)], axis=1)
    sinf = jnp.concatenate([-sin, sin, jnp.zeros((T, dh - rot), F32)], axis=1)
    return cosf, sinf, half


def _plan_kernel(idx_ref, pos_ref, pstart_ref, dest_ref):
    starts = jnp.broadcast_to(pstart_ref[...], idx_ref.shape)
    dest_ref[...] = jnp.take_along_axis(starts, idx_ref[...], axis=1) + pos_ref[...]


def _dispatch_plan(idx, pos, cnt):
    T, E = idx.shape
    R = EXPERT_ROWS
    n_blocks = -(-(T * TOP_K) // R) + E
    counts = cnt.reshape(E).astype(I32)
    padded = (counts + R - 1) // R * R
    pend = jnp.cumsum(padded)
    pstart = pend - padded
    tm = _tile(T, 1024)
    row = pl.BlockSpec((tm, E), lambda i: (i, 0))
    dest = pl.pallas_call(
        _plan_kernel, grid=(T // tm,),
        in_specs=[row, row, pl.BlockSpec((1, E), lambda i: (0, 0))], out_specs=row,
        out_shape=jax.ShapeDtypeStruct((T, E), I32),
        compiler_params=_params("parallel"), name="plan",
    )(idx, pos, pstart.reshape(1, E).astype(I32))
    dest8 = dest[:, :TOP_K]
    blk_e = jnp.minimum(jnp.searchsorted(pend, jnp.arange(n_blocks, dtype=I32) * R, side='right'),
                        E - 1).astype(I32)
    n_used = (pend[-1:] // R).astype(I32)
    blk = jnp.arange(n_blocks, dtype=I32)
    first = (blk < n_used[0]) & ((blk == 0) | (blk_e != jnp.roll(blk_e, 1)))
    slot = (jnp.cumsum(first.astype(I32)) - 1) % 2
    cand = jnp.where(counts > 0, jnp.arange(E, dtype=I32), E)
    later = jnp.concatenate([lax.cummin(cand[::-1])[::-1][1:], jnp.full((1,), E, I32)])
    nxt = jnp.where(later < E, later, -1)[blk_e]
    plan = jnp.stack([blk_e, first.astype(I32), slot.astype(I32), nxt.astype(I32)])
    fill_start = (pstart + counts).astype(I32)
    fill_n = (padded - counts).astype(I32)
    return plan, n_used, dest8.astype(I32), fill_start, fill_n, n_blocks * R


def kernel(x, positions, ln_in_g, ln_in_b, w_in, b_gate, lam_q1, lam_k1, lam_q2, lam_k2, subln_g, conv_w, conv_b, w_rg_a, b_rg_a, w_rg_x, b_rg_x, lru_lambda, w_proj_attn, w_proj_rnn, w_out, ln1_g, ln1_b, w_router, router_bias, w_exp_gate, w_exp_up, w_exp_down, w_sh_gate, w_sh_up, w_sh_down, ln2_g, ln2_b):
    B, S, D = x.shape
    T = B * S
    depth = w_in.shape[0]
    alpha = (2 * depth) ** 0.25
    dh = lam_q1.shape[-1]
    att_v = w_proj_attn.shape[1]
    d_rnn = conv_w.shape[-1]
    att_qk = (w_in.shape[2] - att_v - 2 * d_rnn - 2 * D) // 2
    heads = att_v // (2 * dh)
    assert dh == LANES and att_qk == att_v
    cosf, sinf, half = _rope_tables(positions, dh)

    hf, hb = _ln_in(x.reshape(T, D), ln_in_g, ln_in_b)
    for l in range(depth):
        lambda_init = 0.8 - 0.6 * math.exp(-0.3 * l)
        lam = (jnp.exp(jnp.sum(lam_q1[l] * lam_k1[l])) - jnp.exp(jnp.sum(lam_q2[l] * lam_k2[l]))
               + lambda_init).reshape(1).astype(F32)
        proj = _in_proj(hb, w_in[l].astype(BF16), cosf, sinf, att_qk=att_qk,
                        q_scale=dh ** -0.5 * LOG2E, half=half)
        attn = _attention(proj, lam, subln_g[l], B=B, S=S, heads=heads, dh=dh, att_qk=att_qk,
                          lambda_init=lambda_init)
        rnn = _rglru(proj, conv_w[l], conv_b[l], w_rg_a[l], b_rg_a[l], w_rg_x[l], b_rg_x[l], lru_lambda[l],
                     B=B, S=S, off_xr=2 * att_qk + att_v)
        mixed = _mix(attn, rnn, w_proj_attn[l].astype(BF16), w_proj_rnn[l].astype(BF16), proj, b_gate[l],
                     off_g=2 * att_qk + att_v + 2 * d_rnn)
        h1f, h1p = _out_ln(mixed, w_out[l].astype(BF16), hf, ln1_g[l], ln1_b[l], alpha=alpha)
        w8, idx, pos, cnt = _router(h1f, w_router[l], router_bias[l])
        plan, n_used, dest8, fill_start, fill_n, n_rows = _dispatch_plan(idx, pos, cnt)
        xs = _dispatch(h1p, dest8, fill_start, fill_n, n_rows)
        ys = _experts(xs, plan, n_used, w_exp_gate[l], w_exp_up[l], w_exp_down[l])
        hf = _final(dest8, w8[:, :TOP_K], ys, h1f, h1p, w_sh_gate[l].astype(BF16), w_sh_up[l].astype(BF16),
                    w_sh_down[l].astype(BF16), ln2_g[l], ln2_b[l], alpha=alpha)
        if l + 1 < depth:
            hb = hf.astype(BF16)
    return hf.reshape(B, S, D)
```

```python
import functools
import math

import jax
import jax.numpy as jnp
from jax import lax
from jax.experimental import pallas as pl
from jax.experimental.pallas import tpu as pltpu

F32 = jnp.float32
BF16 = jnp.bfloat16
I32 = jnp.int32
U32 = jnp.uint32

CHUNK = 64
ROPE_THETA = 500000.0
ROT_FRACTION = 4
LRU_C = 8.0
CONV_WIDTH = 4
N_GROUPS = 8
TOPK_GROUPS = 4
TOP_K = 8
ROUTED_SCALE = 2.5
LN_EPS = 1e-5
LOG2E = 1.4426950408889634

LANES = 128
SUBLANES = 8
VMEM_LIMIT_BYTES = 56 * 1024 * 1024
EXPERT_ROWS = 256
DMA_UNROLL = 8


def _tile(n, pref):
    t = min(n, pref)
    while n % t:
        t //= 2
    return t


def _params(*sem):
    return pltpu.CompilerParams(dimension_semantics=sem, vmem_limit_bytes=VMEM_LIMIT_BYTES)


def _ln_rows(x, g, b):
    mu = jnp.mean(x, -1, keepdims=True)
    xc = x - mu
    var = jnp.mean(xc * xc, -1, keepdims=True)
    return xc * lax.rsqrt(var + LN_EPS) * g + b


def _silu(x):
    return x * jax.nn.sigmoid(x)


def _pack_halves(y):
    half = y.shape[1] // 2
    bits = lax.bitcast_convert_type(y.astype(BF16).astype(F32), U32)
    return (bits[:, :half] >> 16) | (bits[:, half:] & jnp.uint32(0xFFFF0000))


def _unpack_halves(p):
    lo = lax.bitcast_convert_type(p << 16, F32)
    hi = lax.bitcast_convert_type(p & jnp.uint32(0xFFFF0000), F32)
    return lo, hi


def _ln_in_kernel(x_ref, g_ref, b_ref, yf_ref, yb_ref):
    y = _ln_rows(x_ref[...], g_ref[...], b_ref[...])
    yf_ref[...] = y
    yb_ref[...] = y.astype(BF16)


def _ln_in(x2, g, b):
    T, D = x2.shape
    tm = _tile(T, 256)
    row = pl.BlockSpec((tm, D), lambda i: (i, 0))
    vec = pl.BlockSpec((1, D), lambda i: (0, 0))
    return pl.pallas_call(
        _ln_in_kernel, grid=(T // tm,),
        in_specs=[row, vec, vec], out_specs=[row, row],
        out_shape=[jax.ShapeDtypeStruct((T, D), F32), jax.ShapeDtypeStruct((T, D), BF16)],
        compiler_params=_params("parallel"), name="ln_in",
    )(x2, g.reshape(1, D), b.reshape(1, D))


def _in_proj_kernel(a_ref, w_ref, cos_ref, sin_ref, o_ref, *, n_q, n_qk, q_scale, half, n_chunks):
    j = pl.program_id(1)
    tm, tn = o_ref.shape
    cn = tn // n_chunks
    scale = jnp.where(j < n_q, q_scale, 1.0)
    cosf = jnp.where(j < n_qk, cos_ref[...], 1.0) * scale
    sinf = jnp.where(j < n_qk, sin_ref[...], 0.0) * scale
    low = lax.broadcasted_iota(I32, (tm, LANES), 1) < half
    a = a_ref[...]
    for c in range(n_chunks):
        acc = jnp.dot(a, w_ref[:, c * cn:(c + 1) * cn], preferred_element_type=F32)
        for g in range(cn // LANES):
            x = acc[:, g * LANES:(g + 1) * LANES]
            partner = jnp.where(low, pltpu.roll(x, LANES - half, 1), pltpu.roll(x, half, 1))
            col = c * cn + g * LANES
            o_ref[:, col:col + LANES] = (x * cosf + partner * sinf).astype(BF16)


def _in_proj(h0b, w, cosf, sinf, *, att_qk, q_scale, half):
    T, K = h0b.shape
    N = w.shape[1]
    tm = _tile(T, 1024)
    tn = _tile(att_qk, 1024)
    assert N % tn == 0
    kern = functools.partial(_in_proj_kernel, n_q=att_qk // tn, n_qk=2 * att_qk // tn,
                             q_scale=q_scale, half=half, n_chunks=max(1, tn // 256))
    return pl.pallas_call(
        kern, grid=(T // tm, N // tn),
        in_specs=[pl.BlockSpec((tm, K), lambda i, j: (i, 0)),
                  pl.BlockSpec((K, tn), lambda i, j: (0, j)),
                  pl.BlockSpec((tm, LANES), lambda i, j: (i, 0)),
                  pl.BlockSpec((tm, LANES), lambda i, j: (i, 0))],
        out_specs=pl.BlockSpec((tm, tn), lambda i, j: (i, j)),
        out_shape=jax.ShapeDtypeStruct((T, N), BF16),
        compiler_params=_params("parallel", "arbitrary"), name="in_proj",
    )(h0b, w, cosf, sinf)


def _loop_by_pairs(n, body, carry):
    def pair(c2, carry):
        return body(2 * c2 + 1, body(2 * c2, carry))
    carry = lax.fori_loop(0, n // 2, pair, carry)
    return lax.cond(n % 2 == 1, lambda cr: body(n - 1, cr), lambda cr: cr, carry)


def _attn_kernel(lam_ref, q_ref, k_ref, v_ref, g_ref, o_ref, qt_s, vt_s, st_s, acc_s, *, dh, tk, out_scale):
    i = pl.program_id(2)
    tq = q_ref.shape[0]
    S, wv = v_ref.shape

    @pl.when(i == 0)
    def _():
        for c in range(S // tk):
            vt_s[c, :wv] = v_ref[c * tk:(c + 1) * tk, :].astype(F32).T.astype(BF16)
            vt_s[c, wv:] = jnp.ones((vt_s.shape[1] - wv, tk), BF16)

    qt_s[...] = q_ref[...].astype(F32).T.astype(BF16)

    def scores(c, m):
        r0 = pl.multiple_of(c * tk, tk)
        return jnp.dot(k_ref[pl.ds(r0, tk), m * dh:(m + 1) * dh], qt_s[m * dh:(m + 1) * dh, :],
                       preferred_element_type=F32)

    def phase1(c, mx):
        out = []
        for m in range(2):
            st = scores(c, m)
            st_s[c, m] = st
            out.append(jnp.maximum(mx[m], jnp.max(st, 0, keepdims=True)))
        return tuple(out)

    mx = _loop_by_pairs(i, phase1, tuple(jnp.full((1, tq), -jnp.inf, F32) for _ in range(2)))
    visible = ((lax.broadcasted_iota(I32, (tk, tq), 0) // CHUNK)
               <= (lax.broadcasted_iota(I32, (tk, tq), 1) // CHUNK))
    mxs = []
    for m in range(2):
        st = jnp.where(visible, scores(i, m), -jnp.inf)
        st_s[i, m] = st
        mxs.append(jnp.maximum(mx[m], jnp.max(st, 0, keepdims=True)))

    acc_s[...] = jnp.zeros_like(acc_s)

    def phase2(c, carry):
        vt = vt_s[c]
        for m in range(2):
            p = jnp.exp2(st_s[c, m] - mxs[m])
            acc_s[m] += jnp.dot(vt, p.astype(BF16), preferred_element_type=F32)
        return carry

    _loop_by_pairs(i + 1, phase2, 0)
    o0, o1 = (acc_s[m, :wv] * (1.0 / acc_s[m, wv:wv + 1]) for m in range(2))
    ot = o0 - lam_ref[0] * o1
    ot = ot * lax.rsqrt(jnp.mean(ot * ot, 0, keepdims=True) + LN_EPS)
    o_ref[...] = (ot.T * (g_ref[...] * out_scale)).astype(BF16)


def _attention(proj, lam, subln_g, *, B, S, heads, dh, att_qk, lambda_init):
    T = B * S
    wv = 2 * dh
    tq = _tile(S, 512)
    tk = tq
    nq = S // tq
    kern = functools.partial(_attn_kernel, dh=dh, tk=tk, out_scale=1.0 - lambda_init)
    ones_rows = 2 * SUBLANES
    scratch = [pltpu.VMEM((wv, tq), BF16), pltpu.VMEM((S // tk, wv + ones_rows, tk), BF16),
               pltpu.VMEM((S // tk, 2, tk, tq), F32), pltpu.VMEM((2, wv + ones_rows, tq), F32)]
    return pl.pallas_call(
        kern, grid=(B, heads, nq),
        in_specs=[pl.BlockSpec(memory_space=pltpu.SMEM),
                  pl.BlockSpec((tq, wv), lambda b, h, i: (b * nq + i, h)),
                  pl.BlockSpec((S, wv), lambda b, h, i: (b, att_qk // wv + h)),
                  pl.BlockSpec((S, wv), lambda b, h, i: (b, 2 * att_qk // wv + h)),
                  pl.BlockSpec((1, wv), lambda b, h, i: (0, 0))],
        out_specs=pl.BlockSpec((tq, wv), lambda b, h, i: (b * nq + i, h)),
        out_shape=jax.ShapeDtypeStruct((T, heads * wv), BF16),
        scratch_shapes=scratch,
        compiler_params=_params("parallel", "parallel", "arbitrary"), name="attention",
    )(lam, proj, proj, proj, subln_g.reshape(1, wv))


def _gelu_tanh(x):
    return 0.5 * x * (1.0 + jnp.tanh(0.7978845608028654 * (x + 0.044715 * (x * x * x))))


def _rglru_kernel(xr_ref, yr_ref, cw_ref, cb_ref, wa_ref, ba_ref, wx_ref, bx_ref, lam_ref, o_ref,
                  a_s, u_s, xprev_s, h_s, *, n_heads, blk, scan_cols):
    @pl.when(pl.program_id(1) == 0)
    def _():
        xprev_s[...] = jnp.zeros_like(xprev_s)
        h_s[...] = jnp.zeros_like(h_s)

    tt, D = xr_ref.shape
    z = -lam_ref[...]
    sp = jnp.maximum(z, 0.0) + jnp.log1p(jnp.exp(-jnp.abs(z)))
    for hh in range(n_heads):
        sl = slice(hh * blk, (hh + 1) * blk)
        xe = jnp.concatenate([xprev_s[:, sl], xr_ref[:, sl].astype(F32)], axis=0)
        xc = cb_ref[:, sl]
        for j in range(CONV_WIDTH):
            o = SUBLANES - (CONV_WIDTH - 1) + j
            xc = xc + cw_ref[j:j + 1, sl] * xe[o:o + tt]
        xb = xc.astype(BF16)
        r = jax.nn.sigmoid(jnp.dot(xb, wa_ref[hh], preferred_element_type=F32) + ba_ref[:, sl])
        ig = jax.nn.sigmoid(jnp.dot(xb, wx_ref[hh], preferred_element_type=F32) + bx_ref[:, sl])
        log_a = (-LRU_C * r) * sp[:, sl]
        a = jnp.exp(log_a)
        a_s[:, sl] = a
        u_s[:, sl] = jnp.sqrt(-jnp.tanh(log_a) * (a * a + 1.0)) * ig * xc
    xprev_s[...] = xr_ref[tt - SUBLANES:tt, :].astype(F32)

    row = lax.broadcasted_iota(I32, (SUBLANES, scan_cols), 0)

    def scan_body(g, h):
        r0 = pl.multiple_of(g * SUBLANES, SUBLANES)
        outs = []
        for c in range(D // scan_cols):
            cs = slice(c * scan_cols, (c + 1) * scan_cols)
            a = a_s[pl.ds(r0, SUBLANES), cs]
            u = u_s[pl.ds(r0, SUBLANES), cs]
            for s in (1, 2, 4):
                valid = row >= s
                u = jnp.where(valid, a * pltpu.roll(u, s, 0) + u, u)
                a = jnp.where(valid, a * pltpu.roll(a, s, 0), a)
            hg = u + a * h[:, cs]
            u_s[pl.ds(r0, SUBLANES), cs] = hg
            outs.append(hg[SUBLANES - 1:SUBLANES, :])
        return jnp.concatenate(outs, axis=1)

    h_s[...] = lax.fori_loop(0, tt // SUBLANES, scan_body, h_s[...])
    o_ref[...] = (u_s[...] * _gelu_tanh(yr_ref[...].astype(F32))).astype(BF16)


def _rglru(proj, conv_w, conv_b, w_ra, b_ra, w_rx, b_rx, lru_lambda, *, B, S, off_xr):
    T = B * S
    n_heads, blk, _ = w_ra.shape
    D = n_heads * blk
    tt = _tile(S, 256)
    nt = S // tt
    assert off_xr % D == 0
    cx = off_xr // D
    vec = pl.BlockSpec((1, D), lambda b, t: (0, 0))
    wspec = pl.BlockSpec((n_heads, blk, blk), lambda b, t: (0, 0, 0))
    kern = functools.partial(_rglru_kernel, n_heads=n_heads, blk=blk, scan_cols=_tile(D, 512))
    return pl.pallas_call(
        kern, grid=(B, nt),
        in_specs=[pl.BlockSpec((tt, D), lambda b, t: (b * nt + t, cx)),
                  pl.BlockSpec((tt, D), lambda b, t: (b * nt + t, cx + 1)),
                  pl.BlockSpec((CONV_WIDTH, D), lambda b, t: (0, 0)), vec,
                  wspec, vec, wspec, vec, vec],
        out_specs=pl.BlockSpec((tt, D), lambda b, t: (b * nt + t, 0)),
        out_shape=jax.ShapeDtypeStruct((T, D), BF16),
        scratch_shapes=[pltpu.VMEM((tt, D), F32), pltpu.VMEM((tt, D), F32),
                        pltpu.VMEM((SUBLANES, D), F32), pltpu.VMEM((1, D), F32)],
        compiler_params=_params("parallel", "arbitrary"), name="rglru",
    )(proj, proj, conv_w, conv_b.reshape(1, D), w_ra.astype(BF16), b_ra.reshape(1, D),
      w_rx.astype(BF16), b_rx.reshape(1, D), lru_lambda.reshape(1, D))


def _mix_kernel(at_ref, rn_ref, wa_ref, wr_ref, g0_ref, g1_ref, b0_ref, b1_ref, o_ref):
    pa = jnp.dot(at_ref[...], wa_ref[...], preferred_element_type=F32)
    pr = jnp.dot(rn_ref[...], wr_ref[...], preferred_element_type=F32)
    s0 = jax.nn.sigmoid(g0_ref[...].astype(F32) + b0_ref[...])
    s1 = jax.nn.sigmoid(g1_ref[...].astype(F32) + b1_ref[...])
    o_ref[...] = (s0 * pa + s1 * pr).astype(BF16)


def _mix(attn, rnn, wa, wr, proj, b_gate, *, off_g):
    T, Ka = attn.shape
    Kr = rnn.shape[1]
    D = wa.shape[1]
    tm = _tile(T, 512)
    tn = _tile(D, 512)
    assert off_g % tn == 0
    cg = off_g // tn
    nn = D // tn
    return pl.pallas_call(
        _mix_kernel, grid=(T // tm, nn),
        in_specs=[pl.BlockSpec((tm, Ka), lambda i, j: (i, 0)),
                  pl.BlockSpec((tm, Kr), lambda i, j: (i, 0)),
                  pl.BlockSpec((Ka, tn), lambda i, j: (0, j)),
                  pl.BlockSpec((Kr, tn), lambda i, j: (0, j)),
                  pl.BlockSpec((tm, tn), lambda i, j: (i, cg + j)),
                  pl.BlockSpec((tm, tn), lambda i, j: (i, cg + nn + j)),
                  pl.BlockSpec((1, tn), lambda i, j: (0, j)),
                  pl.BlockSpec((1, tn), lambda i, j: (0, nn + j))],
        out_specs=pl.BlockSpec((tm, tn), lambda i, j: (i, j)),
        out_shape=jax.ShapeDtypeStruct((T, D), BF16),
        compiler_params=_params("parallel", "arbitrary"), name="mix",
    )(attn, rnn, wa, wr, proj, proj, b_gate.reshape(1, 2 * D), b_gate.reshape(1, 2 * D))


def _out_ln_kernel(a_ref, w_ref, h0_ref, g_ref, b_ref, hf_ref, hp_ref, acc_s, *, alpha):
    j = pl.program_id(1)
    n_n, _, tn = acc_s.shape
    acc_s[j] = alpha * h0_ref[...] + jnp.dot(a_ref[...], w_ref[...], preferred_element_type=F32)

    @pl.when(j == n_n - 1)
    def _():
        D = n_n * tn
        mu = sum(jnp.sum(acc_s[c], -1, keepdims=True) for c in range(n_n)) / D
        var = sum(jnp.sum(jnp.square(acc_s[c] - mu), -1, keepdims=True) for c in range(n_n)) / D
        rstd = lax.rsqrt(var + LN_EPS)
        hn = n_n // 2
        for c in range(hn):
            ys = []
            for cc in (c, c + hn):
                cs = slice(cc * tn, (cc + 1) * tn)
                y = (acc_s[cc] - mu) * rstd * g_ref[:, cs] + b_ref[:, cs]
                hf_ref[:, cs] = y
                ys.append(y)
            hp_ref[:, c * tn:(c + 1) * tn] = _pack_halves(jnp.concatenate(ys, axis=1))


def _out_ln(mixed, w, h0f, g, b, *, alpha):
    T, K = mixed.shape
    D = w.shape[1]
    tm = _tile(T, 512)
    tn = _tile(D // 2, 512)
    vec = pl.BlockSpec((1, D), lambda i, j: (0, 0))
    return pl.pallas_call(
        functools.partial(_out_ln_kernel, alpha=alpha), grid=(T // tm, D // tn),
        in_specs=[pl.BlockSpec((tm, K), lambda i, j: (i, 0), pipeline_mode=pl.Buffered(1)),
                  pl.BlockSpec((K, tn), lambda i, j: (0, j)),
                  pl.BlockSpec((tm, tn), lambda i, j: (i, j)), vec, vec],
        out_specs=[pl.BlockSpec((tm, D), lambda i, j: (i, 0)),
                   pl.BlockSpec((tm, D // 2), lambda i, j: (i, 0))],
        out_shape=[jax.ShapeDtypeStruct((T, D), F32), jax.ShapeDtypeStruct((T, D // 2), U32)],
        scratch_shapes=[pltpu.VMEM((D // tn, tm, tn), F32)],
        compiler_params=_params("parallel", "arbitrary"), name="out_ln",
    )(mixed, w, h0f, g.reshape(1, D), b.reshape(1, D))


def _seg_allreduce(x, lane, width, op):
    n = x.shape[1]
    s = 1
    while s < width:
        partner = jnp.where((lane & s) == 0, pltpu.roll(x, n - s, 1), pltpu.roll(x, s, 1))
        x = op(x, partner)
        s *= 2
    return x


def _router_kernel(h_ref, w_ref, bias_ref, w8_ref, idx_ref, pos_ref, cnt_ref, tri_s, run_s):
    @pl.when(pl.program_id(0) == 0)
    def _():
        tm = tri_s.shape[0]
        tri_s[...] = jnp.where(lax.broadcasted_iota(I32, (tm, tm), 1) < lax.broadcasted_iota(I32, (tm, tm), 0),
                               1.0, 0.0).astype(BF16)
        run_s[...] = jnp.zeros_like(run_s)

    h = h_ref[...]
    w = w_ref[...]
    h_hi = h.astype(BF16)
    h_lo = (h - h_hi.astype(F32)).astype(BF16)
    w_hi = w.astype(BF16)
    w_lo = (w - w_hi.astype(F32)).astype(BF16)
    logits = (jnp.dot(h_hi, w_hi, preferred_element_type=F32)
              + (jnp.dot(h_hi, w_lo, preferred_element_type=F32)
                 + jnp.dot(h_lo, w_hi, preferred_element_type=F32)))
    scores = jax.nn.sigmoid(logits)
    biased = scores + bias_ref[...]
    E = biased.shape[1]
    gw = E // N_GROUPS
    lane = lax.broadcasted_iota(I32, biased.shape, 1)
    neg = -jnp.inf

    m1 = _seg_allreduce(biased, lane, gw, jnp.maximum)
    is_m1 = biased == m1
    cnt = _seg_allreduce(is_m1.astype(F32), lane, gw, jnp.add)
    m2 = jnp.where(cnt >= 2.0, m1, _seg_allreduce(jnp.where(is_m1, neg, biased), lane, gw, jnp.maximum))
    gs = m1 + m2

    gi = lane // gw
    rank = jnp.zeros(biased.shape, I32)
    for k in range(1, N_GROUPS):
        other = pltpu.roll(gs, k * gw, 1)
        ogi = (gi - k) % N_GROUPS
        beats = (other > gs) | ((other == gs) & (ogi < gi))
        rank = rank + beats.astype(I32)
    cur = jnp.where(rank < TOPK_GROUPS, biased, neg)

    sel = jnp.zeros(biased.shape, jnp.bool_)
    firsts = []
    for k in range(TOP_K):
        mx = jnp.max(cur, -1, keepdims=True)
        first = jnp.min(jnp.where(cur == mx, lane, E), -1, keepdims=True)
        pick = lane == first
        sel = sel | pick
        firsts.append(first)
        cur = jnp.where(pick, neg, cur)

    sel_f = jnp.where(sel, 1.0, 0.0)
    rank_in_e = run_s[...] + jnp.dot(tri_s[...], sel_f.astype(BF16), preferred_element_type=F32)
    run_s[...] = run_s[...] + jnp.sum(sel_f, 0, keepdims=True)
    cnt_ref[...] = run_s[...]

    wsum = jnp.sum(jnp.where(sel, scores, 0.0), -1, keepdims=True)
    w8 = jnp.zeros(biased.shape, F32)
    idx = jnp.zeros(biased.shape, I32)
    pos = jnp.zeros(biased.shape, F32)
    for k in range(TOP_K):
        pick = lane == firsts[k]
        wk = jnp.sum(jnp.where(pick, scores, 0.0), -1, keepdims=True)
        pk = jnp.sum(jnp.where(pick, rank_in_e, 0.0), -1, keepdims=True)
        w8 = jnp.where(lane == k, wk / wsum * ROUTED_SCALE, w8)
        idx = jnp.where(lane == k, firsts[k], idx)
        pos = jnp.where(lane == k, pk, pos)
    w8_ref[...] = w8
    idx_ref[...] = idx
    pos_ref[...] = pos.astype(I32)


def _router(h1f, w_router, router_bias):
    T, D = h1f.shape
    E = w_router.shape[1]
    assert E == LANES and E % N_GROUPS == 0
    tm = _tile(T, 512)
    row = pl.BlockSpec((tm, E), lambda i: (i, 0))
    return pl.pallas_call(
        _router_kernel, grid=(T // tm,),
        in_specs=[pl.BlockSpec((tm, D), lambda i: (i, 0)),
                  pl.BlockSpec((D, E), lambda i: (0, 0)),
                  pl.BlockSpec((1, E), lambda i: (0, 0))],
        out_specs=[row, row, row, pl.BlockSpec((1, E), lambda i: (0, 0))],
        out_shape=[jax.ShapeDtypeStruct((T, E), F32), jax.ShapeDtypeStruct((T, E), I32),
                   jax.ShapeDtypeStruct((T, E), I32), jax.ShapeDtypeStruct((1, E), F32)],
        scratch_shapes=[pltpu.VMEM((tm, tm), BF16), pltpu.VMEM((1, E), F32)],
        compiler_params=_params("arbitrary"), name="router",
    )(h1f, w_router, router_bias.reshape(1, E))


def _dispatch_kernel(fill_start_ref, fill_n_ref, d8_ref, hp_ref, xs_hbm, zero_s, sem, zsem):
    tm = hp_ref.shape[0]
    E = fill_n_ref.shape[0]

    def row_copy(r, k):
        return pltpu.make_async_copy(hp_ref.at[pl.ds(r, 1), :],
                                     xs_hbm.at[pl.ds(d8_ref[0, 0, r * TOP_K + k], 1), :], sem.at[0])

    def issue(r, c):
        for k in range(TOP_K):
            row_copy(r, k).start(priority=k % 2)
        return c
    lax.fori_loop(0, tm, issue, 0)

    @pl.when(pl.program_id(0) == 0)
    def _():
        zero_s[...] = jnp.zeros_like(zero_s)

        def zcopy(row):
            return pltpu.make_async_copy(zero_s, xs_hbm.at[pl.ds(row, 1), :], zsem.at[0])

        def per_expert(e, c):
            s0 = fill_start_ref[e]
            n = fill_n_ref[e]

            def zi(r, c2):
                zcopy(s0 + r).start()
                return c2
            lax.fori_loop(0, n, zi, 0)

            def zw(r, c2):
                zcopy(s0 + r).wait()
                return c2
            lax.fori_loop(0, n, zw, 0)
            return c
        lax.fori_loop(0, E, per_expert, 0)

    def wait(r, c):
        for k in range(TOP_K):
            row_copy(r, k).wait()
        return c
    lax.fori_loop(0, tm, wait, 0)


def _dispatch(h1p, dest8, fill_start, fill_n, n_rows):
    T, W = h1p.shape
    tm = _tile(T, 512)
    nt = T // tm
    d8 = dest8.reshape(nt, 1, tm * TOP_K)
    grid_spec = pltpu.PrefetchScalarGridSpec(
        num_scalar_prefetch=2, grid=(nt,),
        in_specs=[pl.BlockSpec((1, 1, tm * TOP_K), lambda i, fs, fn: (i, 0, 0), memory_space=pltpu.SMEM),
                  pl.BlockSpec((tm, W), lambda i, fs, fn: (i, 0))],
        out_specs=pl.BlockSpec(memory_space=pl.ANY),
        scratch_shapes=[pltpu.VMEM((1, W), U32), pltpu.SemaphoreType.DMA((1,)), pltpu.SemaphoreType.DMA((1,))])
    return pl.pallas_call(
        _dispatch_kernel, grid_spec=grid_spec,
        out_shape=jax.ShapeDtypeStruct((n_rows, W), U32),
        compiler_params=_params("arbitrary"), name="dispatch",
    )(fill_start, fill_n, d8, h1p)


def _expert_switch(plan_ref, i, w_hbm, wbuf, sem, recast):
    def copies(e, slot):
        return [pltpu.make_async_copy(w.at[e], wbuf.at[slot, t], sem.at[slot, t]) for t, w in enumerate(w_hbm)]

    @pl.when(plan_ref[1, i] == 1)
    def _():
        e, slot, nxt = plan_ref[0, i], plan_ref[2, i], plan_ref[3, i]

        @pl.when(i == 0)
        def _():
            for c in copies(e, slot):
                c.start()

        for c in copies(e, slot):
            c.wait()
        recast(slot)

        @pl.when(nxt >= 0)
        def _():
            for c in copies(nxt, 1 - slot):
                c.start()


def _experts_up_kernel(plan_ref, n_used_ref, xs_ref, wg_hbm, wu_hbm, hb_ref, wbuf, wgu_s, sem):
    F = wg_hbm.shape[2]
    R = EXPERT_ROWS
    for h in range(xs_ref.shape[0] // R):
        i = pl.program_id(0) * (xs_ref.shape[0] // R) + h
        rows = slice(h * R, (h + 1) * R)

        @pl.when(i < n_used_ref[0])
        def _():
            def recast(slot):
                wgu_s[:, :F] = wbuf[slot, 0].astype(BF16)
                wgu_s[:, F:] = wbuf[slot, 1].astype(BF16)
            _expert_switch(plan_ref, i, (wg_hbm, wu_hbm), wbuf, sem, recast)

            lo, hi = _unpack_halves(xs_ref[rows, :])
            hw = lo.shape[1]
            gu = (jnp.dot(lo.astype(BF16), wgu_s[:hw, :], preferred_element_type=F32)
                  + jnp.dot(hi.astype(BF16), wgu_s[hw:, :], preferred_element_type=F32))
            hb_ref[rows, :] = (_silu(gu[:, :F]) * gu[:, F:]).astype(BF16)

        @pl.when(i >= n_used_ref[0])
        def _():
            hb_ref[rows, :] = jnp.zeros((R, F), BF16)


def _experts_down_kernel(plan_ref, n_used_ref, hb_ref, wd_hbm, ys_ref, wbuf, wd_s, sem):
    R = EXPERT_ROWS
    for h in range(hb_ref.shape[0] // R):
        i = pl.program_id(0) * (hb_ref.shape[0] // R) + h
        rows = slice(h * R, (h + 1) * R)

        @pl.when(i < n_used_ref[0])
        def _():
            def recast(slot):
                wd_s[...] = wbuf[slot, 0].astype(BF16)
            _expert_switch(plan_ref, i, (wd_hbm,), wbuf, sem, recast)

            ys_ref[rows, :] = _pack_halves(jnp.dot(hb_ref[rows, :], wd_s[...], preferred_element_type=F32))

        @pl.when(i >= n_used_ref[0])
        def _():
            ys_ref[rows, :] = jnp.zeros((R, ys_ref.shape[1]), U32)


def _experts(xs, plan, n_used, wg, wu, wd):
    n_rows, W = xs.shape
    E, D, F = wg.shape
    R = EXPERT_ROWS
    per_step = 2
    n_steps = n_rows // (per_step * R)
    assert n_steps * per_step * R == n_rows
    RS = per_step * R
    xrow = lambda i, pn, nu: (jnp.minimum(i, jnp.maximum(nu[0] - 1, 0) // per_step), 0)
    hbm = pl.BlockSpec(memory_space=pl.ANY)
    hb = pl.pallas_call(
        _experts_up_kernel,
        grid_spec=pltpu.PrefetchScalarGridSpec(
            num_scalar_prefetch=2, grid=(n_steps,),
            in_specs=[pl.BlockSpec((RS, W), xrow), hbm, hbm],
            out_specs=pl.BlockSpec((RS, F), lambda i, pn, nu: (i, 0)),
            scratch_shapes=[pltpu.VMEM((2, 2, D, F), F32), pltpu.VMEM((D, 2 * F), BF16),
                            pltpu.SemaphoreType.DMA((2, 2))]),
        out_shape=jax.ShapeDtypeStruct((n_rows, F), BF16),
        compiler_params=_params("arbitrary"), name="experts_up",
    )(plan, n_used, xs, wg, wu)
    return pl.pallas_call(
        _experts_down_kernel,
        grid_spec=pltpu.PrefetchScalarGridSpec(
            num_scalar_prefetch=2, grid=(n_steps,),
            in_specs=[pl.BlockSpec((RS, F), lambda i, pn, nu: (i, 0)), hbm],
            out_specs=pl.BlockSpec((RS, W), lambda i, pn, nu: (i, 0)),
            scratch_shapes=[pltpu.VMEM((2, 1, F, D), F32), pltpu.VMEM((F, D), BF16),
                            pltpu.SemaphoreType.DMA((2, 1))]),
        out_shape=jax.ShapeDtypeStruct((n_rows, W), U32),
        compiler_params=_params("arbitrary"), name="experts_down",
    )(plan, n_used, hb, wd)


def _final_kernel(d8a_ref, d8b_ref, d8n_ref, ys_hbm, hf_ref, hp_ref, w8_ref, wsg_ref, wsu_ref, wsd_ref, g_ref, b_ref,
                  o_ref, buf_a, buf_b, sem, *, alpha):
    i = pl.program_id(0)
    nt = pl.num_programs(0)
    tm = hf_ref.shape[0] // 2
    n = tm * TOP_K

    def issue(d8_ref, rows, buf, s):
        for r in rows:
            pltpu.make_async_copy(ys_hbm.at[pl.ds(d8_ref[0, 0, r], 1), :],
                                  buf.at[r // SUBLANES, pl.ds(r % SUBLANES, 1), :],
                                  sem.at[s]).start(priority=r % 2)

    def wait_all(buf, s):
        def wait(g, c):
            for sub in range(SUBLANES):
                pltpu.make_async_copy(ys_hbm.at[pl.ds(0, 1), :], buf.at[g, pl.ds(sub, 1), :], sem.at[s]).wait()
            return c
        lax.fori_loop(0, n // SUBLANES, wait, 0)

    @pl.when(i == 0)
    def _():
        def first(g, c):
            for sub in range(SUBLANES):
                pltpu.make_async_copy(ys_hbm.at[pl.ds(d8a_ref[0, 0, g * SUBLANES + sub], 1), :],
                                      buf_a.at[g, pl.ds(sub, 1), :], sem.at[0]).start()
            return c
        lax.fori_loop(0, n // SUBLANES, first, 0)

    def shared(rows):
        lo, hi = _unpack_halves(hp_ref[rows, :])
        x = jnp.concatenate([lo.astype(BF16), hi.astype(BF16)], axis=1)
        hs = (_silu(jnp.dot(x, wsg_ref[...], preferred_element_type=F32))
              * jnp.dot(x, wsu_ref[...], preferred_element_type=F32)).astype(BF16)
        return alpha * hf_ref[rows, :] + jnp.dot(hs, wsd_ref[...], preferred_element_type=F32)

    def combine(rows, acc, buf):
        r_lo = r_hi = None
        gk = tm // SUBLANES
        for k in range(TOP_K):
            lo, hi = _unpack_halves(buf[k * gk:(k + 1) * gk].reshape(tm, buf.shape[-1]))
            wk = w8_ref[rows, k:k + 1]
            r_lo = lo * wk if r_lo is None else r_lo + lo * wk
            r_hi = hi * wk if r_hi is None else r_hi + hi * wk
        o_ref[rows, :] = _ln_rows(acc + jnp.concatenate([r_lo, r_hi], axis=1), g_ref[...], b_ref[...])

    rows_a, rows_b = slice(0, tm), slice(tm, 2 * tm)
    issue(d8b_ref, range(0, n // 2), buf_b, 1)
    acc = shared(rows_a)
    wait_all(buf_a, 0)
    issue(d8b_ref, range(n // 2, n), buf_b, 1)
    combine(rows_a, acc, buf_a)
    issue(d8n_ref, range(0, n // 2), buf_a, 0)
    acc = shared(rows_b)
    wait_all(buf_b, 1)
    issue(d8n_ref, range(n // 2, n), buf_a, 0)
    combine(rows_b, acc, buf_b)

    @pl.when(i == nt - 1)
    def _():
        wait_all(buf_a, 0)


def _final(dest8, w8, ys, h1f, h1p, wsg, wsu, wsd, g, b, *, alpha):
    T, D = h1f.shape
    W = ys.shape[1]
    F = wsg.shape[1]
    tm = _tile(T // 2, 128)
    nt = T // (2 * tm)
    d8 = dest8.reshape(2 * nt, tm, TOP_K).transpose(0, 2, 1).reshape(2 * nt, 1, TOP_K * tm)
    row = lambda w: pl.BlockSpec((2 * tm, w), lambda i: (i, 0))
    vec = pl.BlockSpec((1, D), lambda i: (0, 0))
    idx_spec = lambda f: pl.BlockSpec((1, 1, TOP_K * tm), lambda i: (f(i), 0, 0), memory_space=pltpu.SMEM)
    wspec = lambda shape: pl.BlockSpec(shape, lambda i: (0, 0), pipeline_mode=pl.Buffered(1))
    gbuf = pltpu.VMEM((TOP_K * tm // SUBLANES, SUBLANES, W), U32)
    return pl.pallas_call(
        functools.partial(_final_kernel, alpha=alpha), grid=(nt,),
        in_specs=[idx_spec(lambda i: 2 * i), idx_spec(lambda i: 2 * i + 1),
                  idx_spec(lambda i: jnp.minimum(2 * i + 2, 2 * nt - 2)),
                  pl.BlockSpec(memory_space=pl.ANY), row(D), row(W), row(TOP_K),
                  wspec((D, F)), wspec((D, F)), wspec((F, D)), vec, vec],
        out_specs=row(D),
        out_shape=jax.ShapeDtypeStruct((T, D), F32),
        scratch_shapes=[gbuf, gbuf, pltpu.SemaphoreType.DMA((2,))],
        compiler_params=_params("arbitrary"), name="final",
    )(d8, d8, d8, ys, h1f, h1p, w8, wsg, wsu, wsd, g.reshape(1, D), b.reshape(1, D))


def _rope_tables(positions, dh):
    rot = dh // ROT_FRACTION
    half = rot // 2
    inv_freq = jnp.power(ROPE_THETA, -jnp.arange(0, rot, 2, dtype=F32) / rot)
    ang = positions.astype(F32).reshape(-1, 1) * inv_freq
    cos, sin = jnp.cos(ang), jnp.sin(ang)
    T = ang.shape[0]
    cosf = jnp.concatenate([cos, cos, jnp.ones((T, dh - rot), F32)], axis=1)
    sinf = jnp.concatenate([-sin, sin, jnp.zeros((T, dh - rot), F32)], axis=1)
    return cosf, sinf, half


def _plan_kernel(idx_ref, pos_ref, pstart_ref, dest_ref):
    starts = jnp.broadcast_to(pstart_ref[...], idx_ref.shape)
    dest_ref[...] = jnp.take_along_axis(starts, idx_ref[...], axis=1) + pos_ref[...]


def _dispatch_plan(idx, pos, cnt):
    T, E = idx.shape
    R = EXPERT_ROWS
    n_blocks = -(-(T * TOP_K) // R) + E
    counts = cnt.reshape(E).astype(I32)
    padded = (counts + R - 1) // R * R
    pend = jnp.cumsum(padded)
    pstart = pend - padded
    tm = _tile(T, 1024)
    row = pl.BlockSpec((tm, E), lambda i: (i, 0))
    dest = pl.pallas_call(
        _plan_kernel, grid=(T // tm,),
        in_specs=[row, row, pl.BlockSpec((1, E), lambda i: (0, 0))], out_specs=row,
        out_shape=jax.ShapeDtypeStruct((T, E), I32),
        compiler_params=_params("parallel"), name="plan",
    )(idx, pos, pstart.reshape(1, E).astype(I32))
    dest8 = dest[:, :TOP_K]
    blk_e = jnp.minimum(jnp.searchsorted(pend, jnp.arange(n_blocks, dtype=I32) * R, side='right'),
                        E - 1).astype(I32)
    n_used = (pend[-1:] // R).astype(I32)
    blk = jnp.arange(n_blocks, dtype=I32)
    first = (blk < n_used[0]) & ((blk == 0) | (blk_e != jnp.roll(blk_e, 1)))
    slot = (jnp.cumsum(first.astype(I32)) - 1) % 2
    cand = jnp.where(counts > 0, jnp.arange(E, dtype=I32), E)
    later = jnp.concatenate([lax.cummin(cand[::-1])[::-1][1:], jnp.full((1,), E, I32)])
    nxt = jnp.where(later < E, later, -1)[blk_e]
    plan = jnp.stack([blk_e, first.astype(I32), slot.astype(I32), nxt.astype(I32)])
    fill_start = (pstart + counts).astype(I32)
    fill_n = (padded - counts).astype(I32)
    return plan, n_used, dest8.astype(I32), fill_start, fill_n, n_blocks * R


def kernel(x, positions, ln_in_g, ln_in_b, w_in, b_gate, lam_q1, lam_k1, lam_q2, lam_k2, subln_g, conv_w, conv_b, w_rg_a, b_rg_a, w_rg_x, b_rg_x, lru_lambda, w_proj_attn, w_proj_rnn, w_out, ln1_g, ln1_b, w_router, router_bias, w_exp_gate, w_exp_up, w_exp_down, w_sh_gate, w_sh_up, w_sh_down, ln2_g, ln2_b):
    B, S, D = x.shape
    T = B * S
    depth = w_in.shape[0]
    alpha = (2 * depth) ** 0.25
    dh = lam_q1.shape[-1]
    att_v = w_proj_attn.shape[1]
    d_rnn = conv_w.shape[-1]
    att_qk = (w_in.shape[2] - att_v - 2 * d_rnn - 2 * D) // 2
    heads = att_v // (2 * dh)
    assert dh == LANES and att_qk == att_v
    cosf, sinf, half = _rope_tables(positions, dh)

    hf, hb = _ln_in(x.reshape(T, D), ln_in_g, ln_in_b)
    for l in range(depth):
        lambda_init = 0.8 - 0.6 * math.exp(-0.3 * l)
        lam = (jnp.exp(jnp.sum(lam_q1[l] * lam_k1[l])) - jnp.exp(jnp.sum(lam_q2[l] * lam_k2[l]))
               + lambda_init).reshape(1).astype(F32)
        proj = _in_proj(hb, w_in[l].astype(BF16), cosf, sinf, att_qk=att_qk,
                        q_scale=dh ** -0.5 * LOG2E, half=half)
        attn = _attention(proj, lam, subln_g[l], B=B, S=S, heads=heads, dh=dh, att_qk=att_qk,
                          lambda_init=lambda_init)
        rnn = _rglru(proj, conv_w[l], conv_b[l], w_rg_a[l], b_rg_a[l], w_rg_x[l], b_rg_x[l], lru_lambda[l],
                     B=B, S=S, off_xr=2 * att_qk + att_v)
        mixed = _mix(attn, rnn, w_proj_attn[l].astype(BF16), w_proj_rnn[l].astype(BF16), proj, b_gate[l],
                     off_g=2 * att_qk + att_v + 2 * d_rnn)
        h1f, h1p = _out_ln(mixed, w_out[l].astype(BF16), hf, ln1_g[l], ln1_b[l], alpha=alpha)
        w8, idx, pos, cnt = _router(h1f, w_router[l], router_bias[l])
        plan, n_used, dest8, fill_start, fill_n, n_rows = _dispatch_plan(idx, pos, cnt)
        xs = _dispatch(h1p, dest8, fill_start, fill_n, n_rows)
        ys = _experts(xs, plan, n_used, w_exp_gate[l], w_exp_up[l], w_exp_down[l])
        hf = _final(dest8, w8[:, :TOP_K], ys, h1f, h1p, w_sh_gate[l].astype(BF16), w_sh_up[l].astype(BF16),
                    w_sh_down[l].astype(BF16), ln2_g[l], ln2_b[l], alpha=alpha)
        if l + 1 < depth:
            hb = hf.astype(BF16)
    return hf.reshape(B, S, D)
```

```python
import functools
import math

import jax
import jax.numpy as jnp
from jax import lax
from jax.experimental import pallas as pl
from jax.experimental.pallas import tpu as pltpu

F32 = jnp.float32
BF16 = jnp.bfloat16
I32 = jnp.int32
U32 = jnp.uint32

CHUNK = 64
ROPE_THETA = 500000.0
ROT_FRACTION = 4
LRU_C = 8.0
CONV_WIDTH = 4
N_GROUPS = 8
TOPK_GROUPS = 4
TOP_K = 8
ROUTED_SCALE = 2.5
LN_EPS = 1e-5
LOG2E = 1.4426950408889634

LANES = 128
SUBLANES = 8
VMEM_LIMIT_BYTES = 56 * 1024 * 1024
EXPERT_ROWS = 256
DMA_UNROLL = 8


def _tile(n, pref):
    t = min(n, pref)
    while n % t:
        t //= 2
    return t


def _params(*sem):
    return pltpu.CompilerParams(dimension_semantics=sem, vmem_limit_bytes=VMEM_LIMIT_BYTES)


def _ln_rows(x, g, b):
    mu = jnp.mean(x, -1, keepdims=True)
    xc = x - mu
    var = jnp.mean(xc * xc, -1, keepdims=True)
    return xc * lax.rsqrt(var + LN_EPS) * g + b


def _silu(x):
    return x * jax.nn.sigmoid(x)


def _pack_halves(y):
    half = y.shape[1] // 2
    bits = lax.bitcast_convert_type(y.astype(BF16).astype(F32), U32)
    return (bits[:, :half] >> 16) | (bits[:, half:] & jnp.uint32(0xFFFF0000))


def _unpack_halves(p):
    lo = lax.bitcast_convert_type(p << 16, F32)
    hi = lax.bitcast_convert_type(p & jnp.uint32(0xFFFF0000), F32)
    return lo, hi


def _ln_in_kernel(x_ref, g_ref, b_ref, yf_ref, yb_ref):
    y = _ln_rows(x_ref[...], g_ref[...], b_ref[...])
    yf_ref[...] = y
    yb_ref[...] = y.astype(BF16)


def _ln_in(x2, g, b):
    T, D = x2.shape
    tm = _tile(T, 256)
    row = pl.BlockSpec((tm, D), lambda i: (i, 0))
    vec = pl.BlockSpec((1, D), lambda i: (0, 0))
    return pl.pallas_call(
        _ln_in_kernel, grid=(T // tm,),
        in_specs=[row, vec, vec], out_specs=[row, row],
        out_shape=[jax.ShapeDtypeStruct((T, D), F32), jax.ShapeDtypeStruct((T, D), BF16)],
        compiler_params=_params("parallel"), name="ln_in",
    )(x2, g.reshape(1, D), b.reshape(1, D))


def _in_proj_kernel(a_ref, w_ref, cos_ref, sin_ref, o_ref, *, n_q, n_qk, q_scale, half, n_chunks):
    j = pl.program_id(1)
    tm, tn = o_ref.shape
    cn = tn // n_chunks
    scale = jnp.where(j < n_q, q_scale, 1.0)
    cosf = jnp.where(j < n_qk, cos_ref[...], 1.0) * scale
    sinf = jnp.where(j < n_qk, sin_ref[...], 0.0) * scale
    low = lax.broadcasted_iota(I32, (tm, LANES), 1) < half
    a = a_ref[...]
    for c in range(n_chunks):
        acc = jnp.dot(a, w_ref[:, c * cn:(c + 1) * cn], preferred_element_type=F32)
        for g in range(cn // LANES):
            x = acc[:, g * LANES:(g + 1) * LANES]
            partner = jnp.where(low, pltpu.roll(x, LANES - half, 1), pltpu.roll(x, half, 1))
            col = c * cn + g * LANES
            o_ref[:, col:col + LANES] = (x * cosf + partner * sinf).astype(BF16)


def _in_proj(h0b, w, cosf, sinf, *, att_qk, q_scale, half):
    T, K = h0b.shape
    N = w.shape[1]
    tm = _tile(T, 1024)
    tn = _tile(att_qk, 1024)
    assert N % tn == 0
    kern = functools.partial(_in_proj_kernel, n_q=att_qk // tn, n_qk=2 * att_qk // tn,
                             q_scale=q_scale, half=half, n_chunks=max(1, tn // 256))
    return pl.pallas_call(
        kern, grid=(T // tm, N // tn),
        in_specs=[pl.BlockSpec((tm, K), lambda i, j: (i, 0)),
                  pl.BlockSpec((K, tn), lambda i, j: (0, j)),
                  pl.BlockSpec((tm, LANES), lambda i, j: (i, 0)),
                  pl.BlockSpec((tm, LANES), lambda i, j: (i, 0))],
        out_specs=pl.BlockSpec((tm, tn), lambda i, j: (i, j)),
        out_shape=jax.ShapeDtypeStruct((T, N), BF16),
        compiler_params=_params("parallel", "arbitrary"), name="in_proj",
    )(h0b, w, cosf, sinf)


def _loop_by_pairs(n, body, carry):
    def pair(c2, carry):
        return body(2 * c2 + 1, body(2 * c2, carry))
    carry = lax.fori_loop(0, n // 2, pair, carry)
    return lax.cond(n % 2 == 1, lambda cr: body(n - 1, cr), lambda cr: cr, carry)


def _attn_kernel(lam_ref, q_ref, k_ref, v_ref, g_ref, o_ref, qt_s, vt_s, st_s, acc_s, *, dh, tk, out_scale):
    i = pl.program_id(2)
    tq = q_ref.shape[0]
    S, wv = v_ref.shape

    @pl.when(i == 0)
    def _():
        for c in range(S // tk):
            vt_s[c, :wv] = v_ref[c * tk:(c + 1) * tk, :].astype(F32).T.astype(BF16)
            vt_s[c, wv:] = jnp.ones((vt_s.shape[1] - wv, tk), BF16)

    qt_s[...] = q_ref[...].astype(F32).T.astype(BF16)

    def scores(c, m):
        r0 = pl.multiple_of(c * tk, tk)
        return jnp.dot(k_ref[pl.ds(r0, tk), m * dh:(m + 1) * dh], qt_s[m * dh:(m + 1) * dh, :],
                       preferred_element_type=F32)

    def phase1(c, mx):
        out = []
        for m in range(2):
            st = scores(c, m)
            st_s[c, m] = st
            out.append(jnp.maximum(mx[m], jnp.max(st, 0, keepdims=True)))
        return tuple(out)

    mx = _loop_by_pairs(i, phase1, tuple(jnp.full((1, tq), -jnp.inf, F32) for _ in range(2)))
    visible = ((lax.broadcasted_iota(I32, (tk, tq), 0) // CHUNK)
               <= (lax.broadcasted_iota(I32, (tk, tq), 1) // CHUNK))
    mxs = []
    for m in range(2):
        st = jnp.where(visible, scores(i, m), -jnp.inf)
        st_s[i, m] = st
        mxs.append(jnp.maximum(mx[m], jnp.max(st, 0, keepdims=True)))

    acc_s[...] = jnp.zeros_like(acc_s)

    def phase2(c, carry):
        vt = vt_s[c]
        for m in range(2):
            p = jnp.exp2(st_s[c, m] - mxs[m])
            acc_s[m] += jnp.dot(vt, p.astype(BF16), preferred_element_type=F32)
        return carry

    _loop_by_pairs(i + 1, phase2, 0)
    o0, o1 = (acc_s[m, :wv] * (1.0 / acc_s[m, wv:wv + 1]) for m in range(2))
    ot = o0 - lam_ref[0] * o1
    ot = ot * lax.rsqrt(jnp.mean(ot * ot, 0, keepdims=True) + LN_EPS)
    o_ref[...] = (ot.T * (g_ref[...] * out_scale)).astype(BF16)


def _attention(proj, lam, subln_g, *, B, S, heads, dh, att_qk, lambda_init):
    T = B * S
    wv = 2 * dh
    tq = _tile(S, 512)
    tk = tq
    nq = S // tq
    kern = functools.partial(_attn_kernel, dh=dh, tk=tk, out_scale=1.0 - lambda_init)
    ones_rows = 2 * SUBLANES
    scratch = [pltpu.VMEM((wv, tq), BF16), pltpu.VMEM((S // tk, wv + ones_rows, tk), BF16),
               pltpu.VMEM((S // tk, 2, tk, tq), F32), pltpu.VMEM((2, wv + ones_rows, tq), F32)]
    return pl.pallas_call(
        kern, grid=(B, heads, nq),
        in_specs=[pl.BlockSpec(memory_space=pltpu.SMEM),
                  pl.BlockSpec((tq, wv), lambda b, h, i: (b * nq + i, h)),
                  pl.BlockSpec((S, wv), lambda b, h, i: (b, att_qk // wv + h)),
                  pl.BlockSpec((S, wv), lambda b, h, i: (b, 2 * att_qk // wv + h)),
                  pl.BlockSpec((1, wv), lambda b, h, i: (0, 0))],
        out_specs=pl.BlockSpec((tq, wv), lambda b, h, i: (b * nq + i, h)),
        out_shape=jax.ShapeDtypeStruct((T, heads * wv), BF16),
        scratch_shapes=scratch,
        compiler_params=_params("parallel", "parallel", "arbitrary"), name="attention",
    )(lam, proj, proj, proj, subln_g.reshape(1, wv))


def _gelu_tanh(x):
    return 0.5 * x * (1.0 + jnp.tanh(0.7978845608028654 * (x + 0.044715 * (x * x * x))))


def _rglru_kernel(xr_ref, yr_ref, cw_ref, cb_ref, wa_ref, ba_ref, wx_ref, bx_ref, lam_ref, o_ref,
                  a_s, u_s, xprev_s, h_s, *, n_heads, blk, scan_cols):
    @pl.when(pl.program_id(1) == 0)
    def _():
        xprev_s[...] = jnp.zeros_like(xprev_s)
        h_s[...] = jnp.zeros_like(h_s)

    tt, D = xr_ref.shape
    z = -lam_ref[...]
    sp = jnp.maximum(z, 0.0) + jnp.log1p(jnp.exp(-jnp.abs(z)))
    for hh in range(n_heads):
        sl = slice(hh * blk, (hh + 1) * blk)
        xe = jnp.concatenate([xprev_s[:, sl], xr_ref[:, sl].astype(F32)], axis=0)
        xc = cb_ref[:, sl]
        for j in range(CONV_WIDTH):
            o = SUBLANES - (CONV_WIDTH - 1) + j
            xc = xc + cw_ref[j:j + 1, sl] * xe[o:o + tt]
        xb = xc.astype(BF16)
        r = jax.nn.sigmoid(jnp.dot(xb, wa_ref[hh], preferred_element_type=F32) + ba_ref[:, sl])
        ig = jax.nn.sigmoid(jnp.dot(xb, wx_ref[hh], preferred_element_type=F32) + bx_ref[:, sl])
        log_a = (-LRU_C * r) * sp[:, sl]
        a = jnp.exp(log_a)
        a_s[:, sl] = a
        u_s[:, sl] = jnp.sqrt(-jnp.tanh(log_a) * (a * a + 1.0)) * ig * xc
    xprev_s[...] = xr_ref[tt - SUBLANES:tt, :].astype(F32)

    row = lax.broadcasted_iota(I32, (SUBLANES, scan_cols), 0)

    def scan_body(g, h):
        r0 = pl.multiple_of(g * SUBLANES, SUBLANES)
        outs = []
        for c in range(D // scan_cols):
            cs = slice(c * scan_cols, (c + 1) * scan_cols)
            a = a_s[pl.ds(r0, SUBLANES), cs]
            u = u_s[pl.ds(r0, SUBLANES), cs]
            for s in (1, 2, 4):
                valid = row >= s
                u = jnp.where(valid, a * pltpu.roll(u, s, 0) + u, u)
                a = jnp.where(valid, a * pltpu.roll(a, s, 0), a)
            hg = u + a * h[:, cs]
            u_s[pl.ds(r0, SUBLANES), cs] = hg
            outs.append(hg[SUBLANES - 1:SUBLANES, :])
        return jnp.concatenate(outs, axis=1)

    h_s[...] = lax.fori_loop(0, tt // SUBLANES, scan_body, h_s[...])
    o_ref[...] = (u_s[...] * _gelu_tanh(yr_ref[...].astype(F32))).astype(BF16)


def _rglru(proj, conv_w, conv_b, w_ra, b_ra, w_rx, b_rx, lru_lambda, *, B, S, off_xr):
    T = B * S
    n_heads, blk, _ = w_ra.shape
    D = n_heads * blk
    tt = _tile(S, 256)
    nt = S // tt
    assert off_xr % D == 0
    cx = off_xr // D
    vec = pl.BlockSpec((1, D), lambda b, t: (0, 0))
    wspec = pl.BlockSpec((n_heads, blk, blk), lambda b, t: (0, 0, 0))
    kern = functools.partial(_rglru_kernel, n_heads=n_heads, blk=blk, scan_cols=_tile(D, 512))
    return pl.pallas_call(
        kern, grid=(B, nt),
        in_specs=[pl.BlockSpec((tt, D), lambda b, t: (b * nt + t, cx)),
                  pl.BlockSpec((tt, D), lambda b, t: (b * nt + t, cx + 1)),
                  pl.BlockSpec((CONV_WIDTH, D), lambda b, t: (0, 0)), vec,
                  wspec, vec, wspec, vec, vec],
        out_specs=pl.BlockSpec((tt, D), lambda b, t: (b * nt + t, 0)),
        out_shape=jax.ShapeDtypeStruct((T, D), BF16),
        scratch_shapes=[pltpu.VMEM((tt, D), F32), pltpu.VMEM((tt, D), F32),
                        pltpu.VMEM((SUBLANES, D), F32), pltpu.VMEM((1, D), F32)],
        compiler_params=_params("parallel", "arbitrary"), name="rglru",
    )(proj, proj, conv_w, conv_b.reshape(1, D), w_ra.astype(BF16), b_ra.reshape(1, D),
      w_rx.astype(BF16), b_rx.reshape(1, D), lru_lambda.reshape(1, D))


def _mix_kernel(at_ref, rn_ref, wa_ref, wr_ref, g0_ref, g1_ref, b0_ref, b1_ref, o_ref):
    pa = jnp.dot(at_ref[...], wa_ref[...], preferred_element_type=F32)
    pr = jnp.dot(rn_ref[...], wr_ref[...], preferred_element_type=F32)
    s0 = jax.nn.sigmoid(g0_ref[...].astype(F32) + b0_ref[...])
    s1 = jax.nn.sigmoid(g1_ref[...].astype(F32) + b1_ref[...])
    o_ref[...] = (s0 * pa + s1 * pr).astype(BF16)


def _mix(attn, rnn, wa, wr, proj, b_gate, *, off_g):
    T, Ka = attn.shape
    Kr = rnn.shape[1]
    D = wa.shape[1]
    tm = _tile(T, 1024)
    tn = _tile(D, 512)
    assert off_g % tn == 0
    cg = off_g // tn
    nn = D // tn
    return pl.pallas_call(
        _mix_kernel, grid=(T // tm, nn),
        in_specs=[pl.BlockSpec((tm, Ka), lambda i, j: (i, 0), pipeline_mode=pl.Buffered(1)),
                  pl.BlockSpec((tm, Kr), lambda i, j: (i, 0), pipeline_mode=pl.Buffered(1)),
                  pl.BlockSpec((Ka, tn), lambda i, j: (0, j)),
                  pl.BlockSpec((Kr, tn), lambda i, j: (0, j)),
                  pl.BlockSpec((tm, tn), lambda i, j: (i, cg + j)),
                  pl.BlockSpec((tm, tn), lambda i, j: (i, cg + nn + j)),
                  pl.BlockSpec((1, tn), lambda i, j: (0, j)),
                  pl.BlockSpec((1, tn), lambda i, j: (0, nn + j))],
        out_specs=pl.BlockSpec((tm, tn), lambda i, j: (i, j)),
        out_shape=jax.ShapeDtypeStruct((T, D), BF16),
        compiler_params=_params("parallel", "arbitrary"), name="mix",
    )(attn, rnn, wa, wr, proj, proj, b_gate.reshape(1, 2 * D), b_gate.reshape(1, 2 * D))


def _out_ln_kernel(a_ref, w_ref, h0_ref, g_ref, b_ref, hf_ref, hp_ref, acc_s, *, alpha):
    j = pl.program_id(1)
    n_n, _, tn = acc_s.shape
    acc_s[j] = alpha * h0_ref[...] + jnp.dot(a_ref[...], w_ref[...], preferred_element_type=F32)

    @pl.when(j == n_n - 1)
    def _():
        D = n_n * tn
        mu = sum(jnp.sum(acc_s[c], -1, keepdims=True) for c in range(n_n)) / D
        var = sum(jnp.sum(jnp.square(acc_s[c] - mu), -1, keepdims=True) for c in range(n_n)) / D
        rstd = lax.rsqrt(var + LN_EPS)
        hn = n_n // 2
        for c in range(hn):
            ys = []
            for cc in (c, c + hn):
                cs = slice(cc * tn, (cc + 1) * tn)
                y = (acc_s[cc] - mu) * rstd * g_ref[:, cs] + b_ref[:, cs]
                hf_ref[:, cs] = y
                ys.append(y)
            hp_ref[:, c * tn:(c + 1) * tn] = _pack_halves(jnp.concatenate(ys, axis=1))


def _out_ln(mixed, w, h0f, g, b, *, alpha):
    T, K = mixed.shape
    D = w.shape[1]
    tm = _tile(T, 512)
    tn = _tile(D // 2, 512)
    vec = pl.BlockSpec((1, D), lambda i, j: (0, 0))
    return pl.pallas_call(
        functools.partial(_out_ln_kernel, alpha=alpha), grid=(T // tm, D // tn),
        in_specs=[pl.BlockSpec((tm, K), lambda i, j: (i, 0), pipeline_mode=pl.Buffered(1)),
                  pl.BlockSpec((K, tn), lambda i, j: (0, j)),
                  pl.BlockSpec((tm, tn), lambda i, j: (i, j)), vec, vec],
        out_specs=[pl.BlockSpec((tm, D), lambda i, j: (i, 0)),
                   pl.BlockSpec((tm, D // 2), lambda i, j: (i, 0))],
        out_shape=[jax.ShapeDtypeStruct((T, D), F32), jax.ShapeDtypeStruct((T, D // 2), U32)],
        scratch_shapes=[pltpu.VMEM((D // tn, tm, tn), F32)],
        compiler_params=_params("parallel", "arbitrary"), name="out_ln",
    )(mixed, w, h0f, g.reshape(1, D), b.reshape(1, D))


def _seg_allreduce(x, lane, width, op):
    n = x.shape[1]
    s = 1
    while s < width:
        partner = jnp.where((lane & s) == 0, pltpu.roll(x, n - s, 1), pltpu.roll(x, s, 1))
        x = op(x, partner)
        s *= 2
    return x


def _router_kernel(h_ref, w_ref, bias_ref, w8_ref, idx_ref, pos_ref, cnt_ref, tri_s, run_s):
    @pl.when(pl.program_id(0) == 0)
    def _():
        tm = tri_s.shape[0]
        tri_s[...] = jnp.where(lax.broadcasted_iota(I32, (tm, tm), 1) < lax.broadcasted_iota(I32, (tm, tm), 0),
                               1.0, 0.0).astype(BF16)
        run_s[...] = jnp.zeros_like(run_s)

    h = h_ref[...]
    w = w_ref[...]
    h_hi = h.astype(BF16)
    h_lo = (h - h_hi.astype(F32)).astype(BF16)
    w_hi = w.astype(BF16)
    w_lo = (w - w_hi.astype(F32)).astype(BF16)
    logits = (jnp.dot(h_hi, w_hi, preferred_element_type=F32)
              + (jnp.dot(h_hi, w_lo, preferred_element_type=F32)
                 + jnp.dot(h_lo, w_hi, preferred_element_type=F32)))
    scores = jax.nn.sigmoid(logits)
    biased = scores + bias_ref[...]
    E = biased.shape[1]
    gw = E // N_GROUPS
    lane = lax.broadcasted_iota(I32, biased.shape, 1)
    neg = -jnp.inf

    m1 = _seg_allreduce(biased, lane, gw, jnp.maximum)
    is_m1 = biased == m1
    cnt = _seg_allreduce(is_m1.astype(F32), lane, gw, jnp.add)
    m2 = jnp.where(cnt >= 2.0, m1, _seg_allreduce(jnp.where(is_m1, neg, biased), lane, gw, jnp.maximum))
    gs = m1 + m2

    gi = lane // gw
    rank = jnp.zeros(biased.shape, I32)
    for k in range(1, N_GROUPS):
        other = pltpu.roll(gs, k * gw, 1)
        ogi = (gi - k) % N_GROUPS
        beats = (other > gs) | ((other == gs) & (ogi < gi))
        rank = rank + beats.astype(I32)
    cur = jnp.where(rank < TOPK_GROUPS, biased, neg)

    sel = jnp.zeros(biased.shape, jnp.bool_)
    firsts = []
    for k in range(TOP_K):
        mx = jnp.max(cur, -1, keepdims=True)
        first = jnp.min(jnp.where(cur == mx, lane, E), -1, keepdims=True)
        pick = lane == first
        sel = sel | pick
        firsts.append(first)
        cur = jnp.where(pick, neg, cur)

    sel_f = jnp.where(sel, 1.0, 0.0)
    rank_in_e = run_s[...] + jnp.dot(tri_s[...], sel_f.astype(BF16), preferred_element_type=F32)
    run_s[...] = run_s[...] + jnp.sum(sel_f, 0, keepdims=True)
    cnt_ref[...] = run_s[...]

    wsum = jnp.sum(jnp.where(sel, scores, 0.0), -1, keepdims=True)
    w8 = jnp.zeros(biased.shape, F32)
    idx = jnp.zeros(biased.shape, I32)
    pos = jnp.zeros(biased.shape, F32)
    for k in range(TOP_K):
        pick = lane == firsts[k]
        wk = jnp.sum(jnp.where(pick, scores, 0.0), -1, keepdims=True)
        pk = jnp.sum(jnp.where(pick, rank_in_e, 0.0), -1, keepdims=True)
        w8 = jnp.where(lane == k, wk / wsum * ROUTED_SCALE, w8)
        idx = jnp.where(lane == k, firsts[k], idx)
        pos = jnp.where(lane == k, pk, pos)
    w8_ref[...] = w8
    idx_ref[...] = idx
    pos_ref[...] = pos.astype(I32)


def _router(h1f, w_router, router_bias):
    T, D = h1f.shape
    E = w_router.shape[1]
    assert E == LANES and E % N_GROUPS == 0
    tm = _tile(T, 512)
    row = pl.BlockSpec((tm, E), lambda i: (i, 0))
    return pl.pallas_call(
        _router_kernel, grid=(T // tm,),
        in_specs=[pl.BlockSpec((tm, D), lambda i: (i, 0)),
                  pl.BlockSpec((D, E), lambda i: (0, 0)),
                  pl.BlockSpec((1, E), lambda i: (0, 0))],
        out_specs=[row, row, row, pl.BlockSpec((1, E), lambda i: (0, 0))],
        out_shape=[jax.ShapeDtypeStruct((T, E), F32), jax.ShapeDtypeStruct((T, E), I32),
                   jax.ShapeDtypeStruct((T, E), I32), jax.ShapeDtypeStruct((1, E), F32)],
        scratch_shapes=[pltpu.VMEM((tm, tm), BF16), pltpu.VMEM((1, E), F32)],
        compiler_params=_params("arbitrary"), name="router",
    )(h1f, w_router, router_bias.reshape(1, E))


def _dispatch_kernel(fill_start_ref, fill_n_ref, d8_ref, hp_ref, xs_hbm, zero_s, sem, zsem):
    tm = hp_ref.shape[0]
    E = fill_n_ref.shape[0]

    def row_copy(r, k):
        return pltpu.make_async_copy(hp_ref.at[pl.ds(r, 1), :],
                                     xs_hbm.at[pl.ds(d8_ref[0, 0, r * TOP_K + k], 1), :], sem.at[0])

    def issue(r, c):
        for k in range(TOP_K):
            row_copy(r, k).start()
        return c
    lax.fori_loop(0, tm, issue, 0)

    @pl.when(pl.program_id(0) == 0)
    def _():
        zero_s[...] = jnp.zeros_like(zero_s)
        sizes = [1 << b for b in reversed(range(3, EXPERT_ROWS.bit_length() - 1))]

        def per_expert(e, c):
            s0 = fill_start_ref[e]
            n = fill_n_ref[e]
            head = jnp.minimum((-s0) % SUBLANES, n)
            body = (n - head) // SUBLANES * SUBLANES
            tail = n - head - body
            for start in (True, False):
                def fill(row, size, pred):
                    cp = pltpu.make_async_copy(zero_s.at[pl.ds(0, size), :], xs_hbm.at[pl.ds(row, size), :],
                                               zsem.at[0])

                    @pl.when(pred)
                    def _():
                        cp.start() if start else cp.wait()

                for r in range(SUBLANES - 1):
                    fill(s0 + r, 1, r < head)
                off = s0 + head
                for b in sizes:
                    fill(pl.multiple_of(off, SUBLANES), b, (body & b) != 0)
                    off = off + (body & b)
                for r in range(SUBLANES - 1):
                    fill(off + r, 1, r < tail)
            return c
        lax.fori_loop(0, E, per_expert, 0)

    def wait(r, c):
        for k in range(TOP_K):
            row_copy(r, k).wait()
        return c
    lax.fori_loop(0, tm, wait, 0)


def _dispatch(h1p, dest8, fill_start, fill_n, n_rows):
    T, W = h1p.shape
    tm = _tile(T, 512)
    nt = T // tm
    d8 = dest8.reshape(nt, 1, tm * TOP_K)
    grid_spec = pltpu.PrefetchScalarGridSpec(
        num_scalar_prefetch=2, grid=(nt,),
        in_specs=[pl.BlockSpec((1, 1, tm * TOP_K), lambda i, fs, fn: (i, 0, 0), memory_space=pltpu.SMEM),
                  pl.BlockSpec((tm, W), lambda i, fs, fn: (i, 0))],
        out_specs=pl.BlockSpec(memory_space=pl.ANY),
        scratch_shapes=[pltpu.VMEM((EXPERT_ROWS // 2, W), U32), pltpu.SemaphoreType.DMA((1,)),
                        pltpu.SemaphoreType.DMA((1,))])
    return pl.pallas_call(
        _dispatch_kernel, grid_spec=grid_spec,
        out_shape=jax.ShapeDtypeStruct((n_rows, W), U32),
        compiler_params=_params("arbitrary"), name="dispatch",
    )(fill_start, fill_n, d8, h1p)


def _expert_switch(plan_ref, i, w_hbm, wbuf, sem, recast):
    def copies(e, slot):
        return [pltpu.make_async_copy(w.at[e], wbuf.at[slot, t], sem.at[slot, t]) for t, w in enumerate(w_hbm)]

    @pl.when(plan_ref[1, i] == 1)
    def _():
        e, slot, nxt = plan_ref[0, i], plan_ref[2, i], plan_ref[3, i]

        @pl.when(i == 0)
        def _():
            for c in copies(e, slot):
                c.start()

        for c in copies(e, slot):
            c.wait()
        recast(slot)

        @pl.when(nxt >= 0)
        def _():
            for c in copies(nxt, 1 - slot):
                c.start()


def _experts_up_kernel(plan_ref, n_used_ref, xs_ref, wg_hbm, wu_hbm, hb_ref, wbuf, wgu_s, sem):
    F = wg_hbm.shape[2]
    R = EXPERT_ROWS
    for h in range(xs_ref.shape[0] // R):
        i = pl.program_id(0) * (xs_ref.shape[0] // R) + h
        rows = slice(h * R, (h + 1) * R)

        @pl.when(i < n_used_ref[0])
        def _():
            def recast(slot):
                wgu_s[:, :F] = wbuf[slot, 0].astype(BF16)
                wgu_s[:, F:] = wbuf[slot, 1].astype(BF16)
            _expert_switch(plan_ref, i, (wg_hbm, wu_hbm), wbuf, sem, recast)

            lo, hi = _unpack_halves(xs_ref[rows, :])
            hw = lo.shape[1]
            gu = (jnp.dot(lo.astype(BF16), wgu_s[:hw, :], preferred_element_type=F32)
                  + jnp.dot(hi.astype(BF16), wgu_s[hw:, :], preferred_element_type=F32))
            hb_ref[rows, :] = (_silu(gu[:, :F]) * gu[:, F:]).astype(BF16)

        @pl.when(i >= n_used_ref[0])
        def _():
            hb_ref[rows, :] = jnp.zeros((R, F), BF16)


def _experts_down_kernel(plan_ref, n_used_ref, hb_ref, wd_hbm, ys_ref, wbuf, wd_s, sem):
    R = EXPERT_ROWS
    for h in range(hb_ref.shape[0] // R):
        i = pl.program_id(0) * (hb_ref.shape[0] // R) + h
        rows = slice(h * R, (h + 1) * R)

        @pl.when(i < n_used_ref[0])
        def _():
            def recast(slot):
                wd_s[...] = wbuf[slot, 0].astype(BF16)
            _expert_switch(plan_ref, i, (wd_hbm,), wbuf, sem, recast)

            ys_ref[rows, :] = _pack_halves(jnp.dot(hb_ref[rows, :], wd_s[...], preferred_element_type=F32))

        @pl.when(i >= n_used_ref[0])
        def _():
            ys_ref[rows, :] = jnp.zeros((R, ys_ref.shape[1]), U32)


def _experts(xs, plan, n_used, wg, wu, wd):
    n_rows, W = xs.shape
    E, D, F = wg.shape
    R = EXPERT_ROWS
    per_step = 2
    n_steps = n_rows // (per_step * R)
    assert n_steps * per_step * R == n_rows
    RS = per_step * R
    xrow = lambda i, pn, nu: (jnp.minimum(i, jnp.maximum(nu[0] - 1, 0) // per_step), 0)
    hbm = pl.BlockSpec(memory_space=pl.ANY)
    hb = pl.pallas_call(
        _experts_up_kernel,
        grid_spec=pltpu.PrefetchScalarGridSpec(
            num_scalar_prefetch=2, grid=(n_steps,),
            in_specs=[pl.BlockSpec((RS, W), xrow), hbm, hbm],
            out_specs=pl.BlockSpec((RS, F), lambda i, pn, nu: (i, 0)),
            scratch_shapes=[pltpu.VMEM((2, 2, D, F), F32), pltpu.VMEM((D, 2 * F), BF16),
                            pltpu.SemaphoreType.DMA((2, 2))]),
        out_shape=jax.ShapeDtypeStruct((n_rows, F), BF16),
        compiler_params=_params("arbitrary"), name="experts_up",
    )(plan, n_used, xs, wg, wu)
    return pl.pallas_call(
        _experts_down_kernel,
        grid_spec=pltpu.PrefetchScalarGridSpec(
            num_scalar_prefetch=2, grid=(n_steps,),
            in_specs=[pl.BlockSpec((RS, F), lambda i, pn, nu: (i, 0)), hbm],
            out_specs=pl.BlockSpec((RS, W), lambda i, pn, nu: (i, 0)),
            scratch_shapes=[pltpu.VMEM((2, 1, F, D), F32), pltpu.VMEM((F, D), BF16),
                            pltpu.SemaphoreType.DMA((2, 1))]),
        out_shape=jax.ShapeDtypeStruct((n_rows, W), U32),
        compiler_params=_params("arbitrary"), name="experts_down",
    )(plan, n_used, hb, wd)


def _final_kernel(d8a_ref, d8b_ref, d8n_ref, ys_hbm, hf_ref, hp_ref, w8_ref, wsg_ref, wsu_ref, wsd_ref, g_ref, b_ref,
                  o_ref, buf_a, buf_b, sem, *, alpha):
    i = pl.program_id(0)
    nt = pl.num_programs(0)
    tm = hf_ref.shape[0] // 2
    n = tm * TOP_K

    def issue(d8_ref, rows, buf, s):
        for r in rows:
            pltpu.make_async_copy(ys_hbm.at[pl.ds(d8_ref[0, 0, r], 1), :],
                                  buf.at[r // SUBLANES, pl.ds(r % SUBLANES, 1), :], sem.at[s]).start()

    def wait_all(buf, s):
        def wait(g, c):
            for sub in range(SUBLANES):
                pltpu.make_async_copy(ys_hbm.at[pl.ds(0, 1), :], buf.at[g, pl.ds(sub, 1), :], sem.at[s]).wait()
            return c
        lax.fori_loop(0, n // SUBLANES, wait, 0)

    @pl.when(i == 0)
    def _():
        def first(g, c):
            for sub in range(SUBLANES):
                pltpu.make_async_copy(ys_hbm.at[pl.ds(d8a_ref[0, 0, g * SUBLANES + sub], 1), :],
                                      buf_a.at[g, pl.ds(sub, 1), :], sem.at[0]).start()
            return c
        lax.fori_loop(0, n // SUBLANES, first, 0)

    def shared(rows):
        lo, hi = _unpack_halves(hp_ref[rows, :])
        x = jnp.concatenate([lo.astype(BF16), hi.astype(BF16)], axis=1)
        hs = (_silu(jnp.dot(x, wsg_ref[...], preferred_element_type=F32))
              * jnp.dot(x, wsu_ref[...], preferred_element_type=F32)).astype(BF16)
        return alpha * hf_ref[rows, :] + jnp.dot(hs, wsd_ref[...], preferred_element_type=F32)

    def combine(rows, acc, buf):
        r_lo = r_hi = None
        gk = tm // SUBLANES
        for k in range(TOP_K):
            lo, hi = _unpack_halves(buf[k * gk:(k + 1) * gk].reshape(tm, buf.shape[-1]))
            wk = w8_ref[rows, k:k + 1]
            r_lo = lo * wk if r_lo is None else r_lo + lo * wk
            r_hi = hi * wk if r_hi is None else r_hi + hi * wk
        o_ref[rows, :] = _ln_rows(acc + jnp.concatenate([r_lo, r_hi], axis=1), g_ref[...], b_ref[...])

    rows_a, rows_b = slice(0, tm), slice(tm, 2 * tm)
    issue(d8b_ref, range(0, n // 2), buf_b, 1)
    acc = shared(rows_a)
    wait_all(buf_a, 0)
    issue(d8b_ref, range(n // 2, n), buf_b, 1)
    combine(rows_a, acc, buf_a)
    issue(d8n_ref, range(0, n // 2), buf_a, 0)
    acc = shared(rows_b)
    wait_all(buf_b, 1)
    issue(d8n_ref, range(n // 2, n), buf_a, 0)
    combine(rows_b, acc, buf_b)

    @pl.when(i == nt - 1)
    def _():
        wait_all(buf_a, 0)


def _final(dest8, w8, ys, h1f, h1p, wsg, wsu, wsd, g, b, *, alpha):
    T, D = h1f.shape
    W = ys.shape[1]
    F = wsg.shape[1]
    tm = _tile(T // 2, 128)
    nt = T // (2 * tm)
    d8 = dest8.reshape(2 * nt, tm, TOP_K).transpose(0, 2, 1).reshape(2 * nt, 1, TOP_K * tm)
    row = lambda w: pl.BlockSpec((2 * tm, w), lambda i: (i, 0))
    vec = pl.BlockSpec((1, D), lambda i: (0, 0))
    idx_spec = lambda f: pl.BlockSpec((1, 1, TOP_K * tm), lambda i: (f(i), 0, 0), memory_space=pltpu.SMEM)
    wspec = lambda shape: pl.BlockSpec(shape, lambda i: (0, 0), pipeline_mode=pl.Buffered(1))
    gbuf = pltpu.VMEM((TOP_K * tm // SUBLANES, SUBLANES, W), U32)
    return pl.pallas_call(
        functools.partial(_final_kernel, alpha=alpha), grid=(nt,),
        in_specs=[idx_spec(lambda i: 2 * i), idx_spec(lambda i: 2 * i + 1),
                  idx_spec(lambda i: jnp.minimum(2 * i + 2, 2 * nt - 2)),
                  pl.BlockSpec(memory_space=pl.ANY), row(D), row(W), row(TOP_K),
                  wspec((D, F)), wspec((D, F)), wspec((F, D)), vec, vec],
        out_specs=row(D),
        out_shape=jax.ShapeDtypeStruct((T, D), F32),
        scratch_shapes=[gbuf, gbuf, pltpu.SemaphoreType.DMA((2,))],
        compiler_params=_params("arbitrary"), name="final",
    )(d8, d8, d8, ys, h1f, h1p, w8, wsg, wsu, wsd, g.reshape(1, D), b.reshape(1, D))


def _rope_tables(positions, dh):
    rot = dh // ROT_FRACTION
    half = rot // 2
    inv_freq = jnp.power(ROPE_THETA, -jnp.arange(0, rot, 2, dtype=F32) / rot)
    ang = positions.astype(F32).reshape(-1, 1) * inv_freq
    cos, sin = jnp.cos(ang), jnp.sin(ang)
    T = ang.shape[0]
    cosf = jnp.concatenate([cos, cos, jnp.ones((T, dh - rot), F32)], axis=1)
    sinf = jnp.concatenate([-sin, sin, jnp.zeros((T, dh - rot), F32)], axis=1)
    return cosf, sinf, half


def _plan_kernel(idx_ref, pos_ref, pstart_ref, dest_ref):
    starts = jnp.broadcast_to(pstart_ref[...], idx_ref.shape)
    dest_ref[...] = jnp.take_along_axis(starts, idx_ref[...], axis=1) + pos_ref[...]


def _dispatch_plan(idx, pos, cnt):
    T, E = idx.shape
    R = EXPERT_ROWS
    n_blocks = -(-(T * TOP_K) // R) + E
    counts = cnt.reshape(E).astype(I32)
    padded = (counts + R - 1) // R * R
    pend = jnp.cumsum(padded)
    pstart = pend - padded
    tm = _tile(T, 1024)
    row = pl.BlockSpec((tm, E), lambda i: (i, 0))
    dest = pl.pallas_call(
        _plan_kernel, grid=(T // tm,),
        in_specs=[row, row, pl.BlockSpec((1, E), lambda i: (0, 0))], out_specs=row,
        out_shape=jax.ShapeDtypeStruct((T, E), I32),
        compiler_params=_params("parallel"), name="plan",
    )(idx, pos, pstart.reshape(1, E).astype(I32))
    dest8 = dest[:, :TOP_K]
    blk = jnp.arange(n_blocks, dtype=I32)
    ex = jnp.arange(E, dtype=I32)
    blk_e = jnp.minimum(jnp.sum(pend[None, :] <= (blk * R)[:, None], axis=1), E - 1).astype(I32)
    n_used = (pend[-1:] // R).astype(I32)
    first = (blk < n_used[0]) & ((blk == 0) | (blk_e != jnp.roll(blk_e, 1)))
    slot = (jnp.cumsum(first.astype(I32)) - 1) % 2
    cand = jnp.where(counts > 0, ex, E)
    later = jnp.min(jnp.where(ex[None, :] > ex[:, None], cand[None, :], E), axis=1)
    nxt_e = jnp.where(later < E, later, -1)
    nxt = jnp.sum(jnp.where(blk_e[:, None] == ex[None, :], nxt_e[None, :], 0), axis=1)
    plan = jnp.stack([blk_e, first.astype(I32), slot.astype(I32), nxt.astype(I32)])
    fill_start = (pstart + counts).astype(I32)
    fill_n = (padded - counts).astype(I32)
    return plan, n_used, dest8.astype(I32), fill_start, fill_n, n_blocks * R


def kernel(x, positions, ln_in_g, ln_in_b, w_in, b_gate, lam_q1, lam_k1, lam_q2, lam_k2, subln_g, conv_w, conv_b, w_rg_a, b_rg_a, w_rg_x, b_rg_x, lru_lambda, w_proj_attn, w_proj_rnn, w_out, ln1_g, ln1_b, w_router, router_bias, w_exp_gate, w_exp_up, w_exp_down, w_sh_gate, w_sh_up, w_sh_down, ln2_g, ln2_b):
    B, S, D = x.shape
    T = B * S
    depth = w_in.shape[0]
    alpha = (2 * depth) ** 0.25
    dh = lam_q1.shape[-1]
    att_v = w_proj_attn.shape[1]
    d_rnn = conv_w.shape[-1]
    att_qk = (w_in.shape[2] - att_v - 2 * d_rnn - 2 * D) // 2
    heads = att_v // (2 * dh)
    assert dh == LANES and att_qk == att_v
    cosf, sinf, half = _rope_tables(positions, dh)

    hf, hb = _ln_in(x.reshape(T, D), ln_in_g, ln_in_b)
    for l in range(depth):
        lambda_init = 0.8 - 0.6 * math.exp(-0.3 * l)
        lam = (jnp.exp(jnp.sum(lam_q1[l] * lam_k1[l])) - jnp.exp(jnp.sum(lam_q2[l] * lam_k2[l]))
               + lambda_init).reshape(1).astype(F32)
        proj = _in_proj(hb, w_in[l].astype(BF16), cosf, sinf, att_qk=att_qk,
                        q_scale=dh ** -0.5 * LOG2E, half=half)
        attn = _attention(proj, lam, subln_g[l], B=B, S=S, heads=heads, dh=dh, att_qk=att_qk,
                          lambda_init=lambda_init)
        rnn = _rglru(proj, conv_w[l], conv_b[l], w_rg_a[l], b_rg_a[l], w_rg_x[l], b_rg_x[l], lru_lambda[l],
                     B=B, S=S, off_xr=2 * att_qk + att_v)
        mixed = _mix(attn, rnn, w_proj_attn[l].astype(BF16), w_proj_rnn[l].astype(BF16), proj, b_gate[l],
                     off_g=2 * att_qk + att_v + 2 * d_rnn)
        h1f, h1p = _out_ln(mixed, w_out[l].astype(BF16), hf, ln1_g[l], ln1_b[l], alpha=alpha)
        w8, idx, pos, cnt = _router(h1f, w_router[l], router_bias[l])
        plan, n_used, dest8, fill_start, fill_n, n_rows = _dispatch_plan(idx, pos, cnt)
        xs = _dispatch(h1p, dest8, fill_start, fill_n, n_rows)
        ys = _experts(xs, plan, n_used, w_exp_gate[l], w_exp_up[l], w_exp_down[l])
        hf = _final(dest8, w8[:, :TOP_K], ys, h1f, h1p, w_sh_gate[l].astype(BF16), w_sh_up[l].astype(BF16),
                    w_sh_down[l].astype(BF16), ln2_g[l], ln2_b[l], alpha=alpha)
        if l + 1 < depth:
            hb = hf.astype(BF16)
    return hf.reshape(B, S, D)
```

```python
import functools
import math

import jax
import jax.numpy as jnp
from jax import lax
from jax.experimental import pallas as pl
from jax.experimental.pallas import tpu as pltpu

F32 = jnp.float32
BF16 = jnp.bfloat16
I32 = jnp.int32
U32 = jnp.uint32

CHUNK = 64
ROPE_THETA = 500000.0
ROT_FRACTION = 4
LRU_C = 8.0
CONV_WIDTH = 4
N_GROUPS = 8
TOPK_GROUPS = 4
TOP_K = 8
ROUTED_SCALE = 2.5
LN_EPS = 1e-5
LOG2E = 1.4426950408889634

LANES = 128
SUBLANES = 8
VMEM_LIMIT_BYTES = 56 * 1024 * 1024
EXPERT_ROWS = 256
DMA_UNROLL = 8


def _tile(n, pref):
    t = min(n, pref)
    while n % t:
        t //= 2
    return t


def _params(*sem):
    return pltpu.CompilerParams(dimension_semantics=sem, vmem_limit_bytes=VMEM_LIMIT_BYTES)


def _ln_rows(x, g, b):
    mu = jnp.mean(x, -1, keepdims=True)
    xc = x - mu
    var = jnp.mean(xc * xc, -1, keepdims=True)
    return xc * lax.rsqrt(var + LN_EPS) * g + b


def _silu(x):
    return x * jax.nn.sigmoid(x)


def _pack_halves(y):
    half = y.shape[1] // 2
    bits = lax.bitcast_convert_type(y.astype(BF16).astype(F32), U32)
    return (bits[:, :half] >> 16) | (bits[:, half:] & jnp.uint32(0xFFFF0000))


def _unpack_halves(p):
    lo = lax.bitcast_convert_type(p << 16, F32)
    hi = lax.bitcast_convert_type(p & jnp.uint32(0xFFFF0000), F32)
    return lo, hi


def _ln_in_kernel(x_ref, g_ref, b_ref, yf_ref, yb_ref):
    y = _ln_rows(x_ref[...], g_ref[...], b_ref[...])
    yf_ref[...] = y
    yb_ref[...] = y.astype(BF16)


def _ln_in(x2, g, b):
    T, D = x2.shape
    tm = _tile(T, 256)
    row = pl.BlockSpec((tm, D), lambda i: (i, 0))
    vec = pl.BlockSpec((1, D), lambda i: (0, 0))
    return pl.pallas_call(
        _ln_in_kernel, grid=(T // tm,),
        in_specs=[row, vec, vec], out_specs=[row, row],
        out_shape=[jax.ShapeDtypeStruct((T, D), F32), jax.ShapeDtypeStruct((T, D), BF16)],
        compiler_params=_params("parallel"), name="ln_in",
    )(x2, g.reshape(1, D), b.reshape(1, D))


def _in_proj_kernel(a_ref, w_ref, cos_ref, sin_ref, o_ref, *, n_q, n_qk, q_scale, half, n_chunks):
    j = pl.program_id(1)
    tm, tn = o_ref.shape
    cn = tn // n_chunks
    scale = jnp.where(j < n_q, q_scale, 1.0)
    cosf = jnp.where(j < n_qk, cos_ref[...], 1.0) * scale
    sinf = jnp.where(j < n_qk, sin_ref[...], 0.0) * scale
    low = lax.broadcasted_iota(I32, (tm, LANES), 1) < half
    a = a_ref[...]
    for c in range(n_chunks):
        acc = jnp.dot(a, w_ref[:, c * cn:(c + 1) * cn], preferred_element_type=F32)
        for g in range(cn // LANES):
            x = acc[:, g * LANES:(g + 1) * LANES]
            partner = jnp.where(low, pltpu.roll(x, LANES - half, 1), pltpu.roll(x, half, 1))
            col = c * cn + g * LANES
            o_ref[:, col:col + LANES] = (x * cosf + partner * sinf).astype(BF16)


def _in_proj(h0b, w, cosf, sinf, *, att_qk, q_scale, half):
    T, K = h0b.shape
    N = w.shape[1]
    tm = _tile(T, 1024)
    tn = _tile(att_qk, 1024)
    assert N % tn == 0
    kern = functools.partial(_in_proj_kernel, n_q=att_qk // tn, n_qk=2 * att_qk // tn,
                             q_scale=q_scale, half=half, n_chunks=max(1, tn // 256))
    return pl.pallas_call(
        kern, grid=(T // tm, N // tn),
        in_specs=[pl.BlockSpec((tm, K), lambda i, j: (i, 0)),
                  pl.BlockSpec((K, tn), lambda i, j: (0, j)),
                  pl.BlockSpec((tm, LANES), lambda i, j: (i, 0)),
                  pl.BlockSpec((tm, LANES), lambda i, j: (i, 0))],
        out_specs=pl.BlockSpec((tm, tn), lambda i, j: (i, j)),
        out_shape=jax.ShapeDtypeStruct((T, N), BF16),
        compiler_params=_params("parallel", "arbitrary"), name="in_proj",
    )(h0b, w, cosf, sinf)


def _loop_by_pairs(n, body, carry):
    def pair(c2, carry):
        return body(2 * c2 + 1, body(2 * c2, carry))
    carry = lax.fori_loop(0, n // 2, pair, carry)
    return lax.cond(n % 2 == 1, lambda cr: body(n - 1, cr), lambda cr: cr, carry)


def _attn_kernel(lam_ref, q_ref, k_ref, v_ref, g_ref, o_ref, qt_s, vt_s, st_s, acc_s, *, dh, tk, out_scale):
    i = pl.program_id(2)
    tq = q_ref.shape[0]
    S, wv = v_ref.shape

    @pl.when(i == 0)
    def _():
        for c in range(S // tk):
            vt_s[c, :wv] = v_ref[c * tk:(c + 1) * tk, :].astype(F32).T.astype(BF16)
            vt_s[c, wv:] = jnp.ones((vt_s.shape[1] - wv, tk), BF16)

    qt_s[...] = q_ref[...].astype(F32).T.astype(BF16)

    def scores(c, m):
        r0 = pl.multiple_of(c * tk, tk)
        return jnp.dot(k_ref[pl.ds(r0, tk), m * dh:(m + 1) * dh], qt_s[m * dh:(m + 1) * dh, :],
                       preferred_element_type=F32)

    def phase1(c, mx):
        out = []
        for m in range(2):
            st = scores(c, m)
            st_s[c, m] = st
            out.append(jnp.maximum(mx[m], jnp.max(st, 0, keepdims=True)))
        return tuple(out)

    mx = _loop_by_pairs(i, phase1, tuple(jnp.full((1, tq), -jnp.inf, F32) for _ in range(2)))
    visible = ((lax.broadcasted_iota(I32, (tk, tq), 0) // CHUNK)
               <= (lax.broadcasted_iota(I32, (tk, tq), 1) // CHUNK))
    mxs = []
    for m in range(2):
        st = jnp.where(visible, scores(i, m), -jnp.inf)
        st_s[i, m] = st
        mxs.append(jnp.maximum(mx[m], jnp.max(st, 0, keepdims=True)))

    def pv(c, m):
        p = jnp.exp2(st_s[c, m] - mxs[m])
        return jnp.dot(vt_s[c], p.astype(BF16), preferred_element_type=F32)

    for m in range(2):
        acc_s[m] = pv(0, m)

    def phase2(c, carry):
        for m in range(2):
            acc_s[m] += pv(c + 1, m)
        return carry

    _loop_by_pairs(i, phase2, 0)
    o0, o1 = (acc_s[m, :wv] * (1.0 / acc_s[m, wv:wv + 1]) for m in range(2))
    ot = o0 - lam_ref[0] * o1
    ot = ot * lax.rsqrt(jnp.mean(ot * ot, 0, keepdims=True) + LN_EPS)
    o_ref[...] = (ot.T * (g_ref[...] * out_scale)).astype(BF16)


def _attention(proj, lam, subln_g, *, B, S, heads, dh, att_qk, lambda_init):
    T = B * S
    wv = 2 * dh
    tq = _tile(S, 512)
    tk = tq
    nq = S // tq
    kern = functools.partial(_attn_kernel, dh=dh, tk=tk, out_scale=1.0 - lambda_init)
    ones_rows = 2 * SUBLANES
    scratch = [pltpu.VMEM((wv, tq), BF16), pltpu.VMEM((S // tk, wv + ones_rows, tk), BF16),
               pltpu.VMEM((S // tk, 2, tk, tq), F32), pltpu.VMEM((2, wv + ones_rows, tq), F32)]
    return pl.pallas_call(
        kern, grid=(B, heads, nq),
        in_specs=[pl.BlockSpec(memory_space=pltpu.SMEM),
                  pl.BlockSpec((tq, wv), lambda b, h, i: (b * nq + i, h)),
                  pl.BlockSpec((S, wv), lambda b, h, i: (b, att_qk // wv + h)),
                  pl.BlockSpec((S, wv), lambda b, h, i: (b, 2 * att_qk // wv + h)),
                  pl.BlockSpec((1, wv), lambda b, h, i: (0, 0))],
        out_specs=pl.BlockSpec((tq, wv), lambda b, h, i: (b * nq + i, h)),
        out_shape=jax.ShapeDtypeStruct((T, heads * wv), BF16),
        scratch_shapes=scratch,
        compiler_params=_params("parallel", "parallel", "arbitrary"), name="attention",
    )(lam, proj, proj, proj, subln_g.reshape(1, wv))


def _gelu_tanh(x):
    return 0.5 * x * (1.0 + jnp.tanh(0.7978845608028654 * (x + 0.044715 * (x * x * x))))


def _rglru_kernel(xr_ref, yr_ref, cw_ref, cb_ref, wa_ref, ba_ref, wx_ref, bx_ref, lam_ref, o_ref,
                  a_s, u_s, xprev_s, h_s, *, n_heads, blk, scan_cols):
    @pl.when(pl.program_id(1) == 0)
    def _():
        xprev_s[...] = jnp.zeros_like(xprev_s)
        h_s[...] = jnp.zeros_like(h_s)

    tt, D = xr_ref.shape
    z = -lam_ref[...]
    sp = jnp.maximum(z, 0.0) + jnp.log1p(jnp.exp(-jnp.abs(z)))
    for hh in range(n_heads):
        sl = slice(hh * blk, (hh + 1) * blk)
        xe = jnp.concatenate([xprev_s[:, sl], xr_ref[:, sl].astype(F32)], axis=0)
        xc = cb_ref[:, sl]
        for j in range(CONV_WIDTH):
            o = SUBLANES - (CONV_WIDTH - 1) + j
            xc = xc + cw_ref[j:j + 1, sl] * xe[o:o + tt]
        xb = xc.astype(BF16)
        r = jax.nn.sigmoid(jnp.dot(xb, wa_ref[hh], preferred_element_type=F32) + ba_ref[:, sl])
        ig = jax.nn.sigmoid(jnp.dot(xb, wx_ref[hh], preferred_element_type=F32) + bx_ref[:, sl])
        log_a = (-LRU_C * r) * sp[:, sl]
        a = jnp.exp(log_a)
        a_s[:, sl] = a
        u_s[:, sl] = jnp.sqrt(-jnp.tanh(log_a) * (a * a + 1.0)) * ig * xc
    xprev_s[...] = xr_ref[tt - SUBLANES:tt, :].astype(F32)

    row = lax.broadcasted_iota(I32, (SUBLANES, scan_cols), 0)

    def scan_body(g, h):
        r0 = pl.multiple_of(g * SUBLANES, SUBLANES)
        outs = []
        for c in range(D // scan_cols):
            cs = slice(c * scan_cols, (c + 1) * scan_cols)
            a = a_s[pl.ds(r0, SUBLANES), cs]
            u = u_s[pl.ds(r0, SUBLANES), cs]
            for s in (1, 2, 4):
                valid = row >= s
                u = jnp.where(valid, a * pltpu.roll(u, s, 0) + u, u)
                a = jnp.where(valid, a * pltpu.roll(a, s, 0), a)
            hg = u + a * h[:, cs]
            u_s[pl.ds(r0, SUBLANES), cs] = hg
            outs.append(hg[SUBLANES - 1:SUBLANES, :])
        return jnp.concatenate(outs, axis=1)

    h_s[...] = lax.fori_loop(0, tt // SUBLANES, scan_body, h_s[...])
    o_ref[...] = (u_s[...] * _gelu_tanh(yr_ref[...].astype(F32))).astype(BF16)


def _rglru(proj, conv_w, conv_b, w_ra, b_ra, w_rx, b_rx, lru_lambda, *, B, S, off_xr):
    T = B * S
    n_heads, blk, _ = w_ra.shape
    D = n_heads * blk
    tt = _tile(S, 256)
    nt = S // tt
    assert off_xr % D == 0
    cx = off_xr // D
    vec = pl.BlockSpec((1, D), lambda b, t: (0, 0))
    wspec = pl.BlockSpec((n_heads, blk, blk), lambda b, t: (0, 0, 0))
    kern = functools.partial(_rglru_kernel, n_heads=n_heads, blk=blk, scan_cols=_tile(D, 512))
    return pl.pallas_call(
        kern, grid=(B, nt),
        in_specs=[pl.BlockSpec((tt, D), lambda b, t: (b * nt + t, cx)),
                  pl.BlockSpec((tt, D), lambda b, t: (b * nt + t, cx + 1)),
                  pl.BlockSpec((CONV_WIDTH, D), lambda b, t: (0, 0)), vec,
                  wspec, vec, wspec, vec, vec],
        out_specs=pl.BlockSpec((tt, D), lambda b, t: (b * nt + t, 0)),
        out_shape=jax.ShapeDtypeStruct((T, D), BF16),
        scratch_shapes=[pltpu.VMEM((tt, D), F32), pltpu.VMEM((tt, D), F32),
                        pltpu.VMEM((SUBLANES, D), F32), pltpu.VMEM((1, D), F32)],
        compiler_params=_params("parallel", "arbitrary"), name="rglru",
    )(proj, proj, conv_w, conv_b.reshape(1, D), w_ra.astype(BF16), b_ra.reshape(1, D),
      w_rx.astype(BF16), b_rx.reshape(1, D), lru_lambda.reshape(1, D))


def _mix_kernel(at_ref, rn_ref, wa_ref, wr_ref, g0_ref, g1_ref, b0_ref, b1_ref, o_ref):
    pa = jnp.dot(at_ref[...], wa_ref[...], preferred_element_type=F32)
    pr = jnp.dot(rn_ref[...], wr_ref[...], preferred_element_type=F32)
    s0 = jax.nn.sigmoid(g0_ref[...].astype(F32) + b0_ref[...])
    s1 = jax.nn.sigmoid(g1_ref[...].astype(F32) + b1_ref[...])
    o_ref[...] = (s0 * pa + s1 * pr).astype(BF16)


def _mix(attn, rnn, wa, wr, proj, b_gate, *, off_g):
    T, Ka = attn.shape
    Kr = rnn.shape[1]
    D = wa.shape[1]
    tm = _tile(T, 512)
    tn = _tile(D, 512)
    assert off_g % tn == 0
    cg = off_g // tn
    nn = D // tn
    return pl.pallas_call(
        _mix_kernel, grid=(T // tm, nn),
        in_specs=[pl.BlockSpec((tm, Ka), lambda i, j: (i, 0)),
                  pl.BlockSpec((tm, Kr), lambda i, j: (i, 0)),
                  pl.BlockSpec((Ka, tn), lambda i, j: (0, j)),
                  pl.BlockSpec((Kr, tn), lambda i, j: (0, j)),
                  pl.BlockSpec((tm, tn), lambda i, j: (i, cg + j)),
                  pl.BlockSpec((tm, tn), lambda i, j: (i, cg + nn + j)),
                  pl.BlockSpec((1, tn), lambda i, j: (0, j)),
                  pl.BlockSpec((1, tn), lambda i, j: (0, nn + j))],
        out_specs=pl.BlockSpec((tm, tn), lambda i, j: (i, j)),
        out_shape=jax.ShapeDtypeStruct((T, D), BF16),
        compiler_params=_params("parallel", "arbitrary"), name="mix",
    )(attn, rnn, wa, wr, proj, proj, b_gate.reshape(1, 2 * D), b_gate.reshape(1, 2 * D))


def _out_ln_kernel(a_ref, w_ref, h0_ref, g_ref, b_ref, hf_ref, hp_ref, acc_s, *, alpha):
    j = pl.program_id(1)
    n_n, _, tn = acc_s.shape
    acc_s[j] = alpha * h0_ref[...] + jnp.dot(a_ref[...], w_ref[...], preferred_element_type=F32)

    @pl.when(j == n_n - 1)
    def _():
        D = n_n * tn
        mu = sum(jnp.sum(acc_s[c], -1, keepdims=True) for c in range(n_n)) / D
        var = sum(jnp.sum(jnp.square(acc_s[c] - mu), -1, keepdims=True) for c in range(n_n)) / D
        rstd = lax.rsqrt(var + LN_EPS)
        hn = n_n // 2
        for c in range(hn):
            ys = []
            for cc in (c, c + hn):
                cs = slice(cc * tn, (cc + 1) * tn)
                y = (acc_s[cc] - mu) * rstd * g_ref[:, cs] + b_ref[:, cs]
                hf_ref[:, cs] = y
                ys.append(y)
            hp_ref[:, c * tn:(c + 1) * tn] = _pack_halves(jnp.concatenate(ys, axis=1))


def _out_ln(mixed, w, h0f, g, b, *, alpha):
    T, K = mixed.shape
    D = w.shape[1]
    tm = _tile(T, 512)
    tn = _tile(D // 2, 512)
    vec = pl.BlockSpec((1, D), lambda i, j: (0, 0))
    return pl.pallas_call(
        functools.partial(_out_ln_kernel, alpha=alpha), grid=(T // tm, D // tn),
        in_specs=[pl.BlockSpec((tm, K), lambda i, j: (i, 0), pipeline_mode=pl.Buffered(1)),
                  pl.BlockSpec((K, tn), lambda i, j: (0, j)),
                  pl.BlockSpec((tm, tn), lambda i, j: (i, j)), vec, vec],
        out_specs=[pl.BlockSpec((tm, D), lambda i, j: (i, 0)),
                   pl.BlockSpec((tm, D // 2), lambda i, j: (i, 0))],
        out_shape=[jax.ShapeDtypeStruct((T, D), F32), jax.ShapeDtypeStruct((T, D // 2), U32)],
        scratch_shapes=[pltpu.VMEM((D // tn, tm, tn), F32)],
        compiler_params=_params("parallel", "arbitrary"), name="out_ln",
    )(mixed, w, h0f, g.reshape(1, D), b.reshape(1, D))


def _seg_allreduce(x, lane, width, op):
    n = x.shape[1]
    s = 1
    while s < width:
        partner = jnp.where((lane & s) == 0, pltpu.roll(x, n - s, 1), pltpu.roll(x, s, 1))
        x = op(x, partner)
        s *= 2
    return x


def _router_kernel(h_ref, w_ref, bias_ref, w8_ref, idx_ref, pos_ref, cnt_ref, tri_s, run_s, w2_s):
    E = w_ref.shape[1]

    @pl.when(pl.program_id(0) == 0)
    def _():
        tm = tri_s.shape[0]
        tri_s[...] = jnp.where(lax.broadcasted_iota(I32, (tm, tm), 1) < lax.broadcasted_iota(I32, (tm, tm), 0),
                               1.0, 0.0).astype(BF16)
        run_s[...] = jnp.zeros_like(run_s)
        w = w_ref[...]
        w_hi = w.astype(BF16)
        w2_s[:, :E] = w_hi
        w2_s[:, E:] = (w - w_hi.astype(F32)).astype(BF16)

    h = h_ref[...]
    h_hi = h.astype(BF16)
    h_lo = (h - h_hi.astype(F32)).astype(BF16)
    hh = jnp.dot(h_hi, w2_s[...], preferred_element_type=F32)
    logits = hh[:, :E] + (hh[:, E:] + jnp.dot(h_lo, w2_s[:, :E], preferred_element_type=F32))
    scores = jax.nn.sigmoid(logits)
    biased = scores + bias_ref[...]
    gw = E // N_GROUPS
    lane = lax.broadcasted_iota(I32, biased.shape, 1)
    neg = -jnp.inf

    m1 = _seg_allreduce(biased, lane, gw, jnp.maximum)
    is_m1 = biased == m1
    cnt = _seg_allreduce(is_m1.astype(F32), lane, gw, jnp.add)
    m2 = jnp.where(cnt >= 2.0, m1, _seg_allreduce(jnp.where(is_m1, neg, biased), lane, gw, jnp.maximum))
    gs = m1 + m2

    gi = lane // gw
    rank = jnp.zeros(biased.shape, I32)
    for k in range(1, N_GROUPS):
        other = pltpu.roll(gs, k * gw, 1)
        ogi = (gi - k) % N_GROUPS
        beats = (other > gs) | ((other == gs) & (ogi < gi))
        rank = rank + beats.astype(I32)
    cur = jnp.where(rank < TOPK_GROUPS, biased, neg)

    sel = jnp.zeros(biased.shape, jnp.bool_)
    firsts = []
    for k in range(TOP_K):
        mx = jnp.max(cur, -1, keepdims=True)
        first = jnp.min(jnp.where(cur == mx, lane, E), -1, keepdims=True)
        pick = lane == first
        sel = sel | pick
        firsts.append(first)
        cur = jnp.where(pick, neg, cur)

    sel_f = jnp.where(sel, 1.0, 0.0)
    rank_in_e = run_s[...] + jnp.dot(tri_s[...], sel_f.astype(BF16), preferred_element_type=F32)
    run_s[...] = run_s[...] + jnp.sum(sel_f, 0, keepdims=True)
    cnt_ref[...] = run_s[...]

    wsum = jnp.sum(jnp.where(sel, scores, 0.0), -1, keepdims=True)
    w8 = jnp.zeros(biased.shape, F32)
    idx = jnp.zeros(biased.shape, I32)
    pos = jnp.zeros(biased.shape, F32)
    for k in range(TOP_K):
        pick = lane == firsts[k]
        wk = jnp.sum(jnp.where(pick, scores, 0.0), -1, keepdims=True)
        pk = jnp.sum(jnp.where(pick, rank_in_e, 0.0), -1, keepdims=True)
        w8 = jnp.where(lane == k, wk / wsum * ROUTED_SCALE, w8)
        idx = jnp.where(lane == k, firsts[k], idx)
        pos = jnp.where(lane == k, pk, pos)
    w8_ref[...] = w8
    idx_ref[...] = idx
    pos_ref[...] = pos.astype(I32)


def _router(h1f, w_router, router_bias):
    T, D = h1f.shape
    E = w_router.shape[1]
    assert E == LANES and E % N_GROUPS == 0
    tm = _tile(T, 512)
    row = pl.BlockSpec((tm, E), lambda i: (i, 0))
    return pl.pallas_call(
        _router_kernel, grid=(T // tm,),
        in_specs=[pl.BlockSpec((tm, D), lambda i: (i, 0)),
                  pl.BlockSpec((D, E), lambda i: (0, 0)),
                  pl.BlockSpec((1, E), lambda i: (0, 0))],
        out_specs=[row, row, row, pl.BlockSpec((1, E), lambda i: (0, 0))],
        out_shape=[jax.ShapeDtypeStruct((T, E), F32), jax.ShapeDtypeStruct((T, E), I32),
                   jax.ShapeDtypeStruct((T, E), I32), jax.ShapeDtypeStruct((1, E), F32)],
        scratch_shapes=[pltpu.VMEM((tm, tm), BF16), pltpu.VMEM((1, E), F32), pltpu.VMEM((D, 2 * E), BF16)],
        compiler_params=_params("arbitrary"), name="router",
    )(h1f, w_router, router_bias.reshape(1, E))


def _dispatch_kernel(fill_start_ref, fill_n_ref, d8_ref, hp_ref, xs_hbm, zero_s, sem, zsem):
    tm = hp_ref.shape[0]
    E = fill_n_ref.shape[0]

    def row_copy(r, k):
        return pltpu.make_async_copy(hp_ref.at[pl.ds(r, 1), :],
                                     xs_hbm.at[pl.ds(d8_ref[0, 0, r * TOP_K + k], 1), :], sem.at[0])

    def issue(r, c):
        for k in range(TOP_K):
            row_copy(r, k).start()
        return c
    lax.fori_loop(0, tm, issue, 0)

    @pl.when(pl.program_id(0) == 0)
    def _():
        zero_s[...] = jnp.zeros_like(zero_s)
        sizes = [1 << b for b in reversed(range(3, EXPERT_ROWS.bit_length() - 1))]

        def per_expert(e, c):
            s0 = fill_start_ref[e]
            n = fill_n_ref[e]
            head = jnp.minimum((-s0) % SUBLANES, n)
            body = (n - head) // SUBLANES * SUBLANES
            tail = n - head - body
            for start in (True, False):
                def fill(row, size, pred):
                    cp = pltpu.make_async_copy(zero_s.at[pl.ds(0, size), :], xs_hbm.at[pl.ds(row, size), :],
                                               zsem.at[0])

                    @pl.when(pred)
                    def _():
                        cp.start() if start else cp.wait()

                for r in range(SUBLANES - 1):
                    fill(s0 + r, 1, r < head)
                off = s0 + head
                for b in sizes:
                    fill(pl.multiple_of(off, SUBLANES), b, (body & b) != 0)
                    off = off + (body & b)
                for r in range(SUBLANES - 1):
                    fill(off + r, 1, r < tail)
            return c
        lax.fori_loop(0, E, per_expert, 0)

    def wait(r, c):
        for k in range(TOP_K):
            row_copy(r, k).wait()
        return c
    lax.fori_loop(0, tm, wait, 0)


def _dispatch(h1p, dest8, fill_start, fill_n, n_rows):
    T, W = h1p.shape
    tm = _tile(T, 512)
    nt = T // tm
    d8 = dest8.reshape(nt, 1, tm * TOP_K)
    grid_spec = pltpu.PrefetchScalarGridSpec(
        num_scalar_prefetch=2, grid=(nt,),
        in_specs=[pl.BlockSpec((1, 1, tm * TOP_K), lambda i, fs, fn: (i, 0, 0), memory_space=pltpu.SMEM),
                  pl.BlockSpec((tm, W), lambda i, fs, fn: (i, 0))],
        out_specs=pl.BlockSpec(memory_space=pl.ANY),
        scratch_shapes=[pltpu.VMEM((EXPERT_ROWS // 2, W), U32), pltpu.SemaphoreType.DMA((1,)),
                        pltpu.SemaphoreType.DMA((1,))])
    return pl.pallas_call(
        _dispatch_kernel, grid_spec=grid_spec,
        out_shape=jax.ShapeDtypeStruct((n_rows, W), U32),
        compiler_params=_params("arbitrary"), name="dispatch",
    )(fill_start, fill_n, d8, h1p)


def _expert_switch(plan_ref, i, w_hbm, wbuf, sem, recast):
    def copies(e, slot):
        return [pltpu.make_async_copy(w.at[e], wbuf.at[slot, t], sem.at[slot, t]) for t, w in enumerate(w_hbm)]

    @pl.when(plan_ref[1, i] == 1)
    def _():
        e, slot, nxt = plan_ref[0, i], plan_ref[2, i], plan_ref[3, i]

        @pl.when(i == 0)
        def _():
            for c in copies(e, slot):
                c.start()

        for c in copies(e, slot):
            c.wait()
        recast(slot)

        @pl.when(nxt >= 0)
        def _():
            for c in copies(nxt, 1 - slot):
                c.start()


def _experts_up_kernel(plan_ref, n_used_ref, xs_ref, wg_hbm, wu_hbm, hb_ref, wbuf, wgu_s, sem):
    F = wg_hbm.shape[2]
    R = EXPERT_ROWS
    for h in range(xs_ref.shape[0] // R):
        i = pl.program_id(0) * (xs_ref.shape[0] // R) + h
        rows = slice(h * R, (h + 1) * R)

        @pl.when(i < n_used_ref[0])
        def _():
            def recast(slot):
                wgu_s[:, :F] = wbuf[slot, 0].astype(BF16)
                wgu_s[:, F:] = wbuf[slot, 1].astype(BF16)
            _expert_switch(plan_ref, i, (wg_hbm, wu_hbm), wbuf, sem, recast)

            lo, hi = _unpack_halves(xs_ref[rows, :])
            hw = lo.shape[1]
            gu = (jnp.dot(lo.astype(BF16), wgu_s[:hw, :], preferred_element_type=F32)
                  + jnp.dot(hi.astype(BF16), wgu_s[hw:, :], preferred_element_type=F32))
            hb_ref[rows, :] = (_silu(gu[:, :F]) * gu[:, F:]).astype(BF16)

        @pl.when(i >= n_used_ref[0])
        def _():
            hb_ref[rows, :] = jnp.zeros((R, F), BF16)


def _experts_down_kernel(plan_ref, n_used_ref, hb_ref, wd_hbm, ys_ref, wbuf, wd_s, sem):
    R = EXPERT_ROWS
    for h in range(hb_ref.shape[0] // R):
        i = pl.program_id(0) * (hb_ref.shape[0] // R) + h
        rows = slice(h * R, (h + 1) * R)

        @pl.when(i < n_used_ref[0])
        def _():
            def recast(slot):
                wd_s[...] = wbuf[slot, 0].astype(BF16)
            _expert_switch(plan_ref, i, (wd_hbm,), wbuf, sem, recast)

            ys_ref[rows, :] = _pack_halves(jnp.dot(hb_ref[rows, :], wd_s[...], preferred_element_type=F32))

        @pl.when(i >= n_used_ref[0])
        def _():
            ys_ref[rows, :] = jnp.zeros((R, ys_ref.shape[1]), U32)


def _experts(xs, plan, n_used, wg, wu, wd):
    n_rows, W = xs.shape
    E, D, F = wg.shape
    R = EXPERT_ROWS
    per_step = 2
    n_steps = n_rows // (per_step * R)
    assert n_steps * per_step * R == n_rows
    RS = per_step * R
    xrow = lambda i, pn, nu: (jnp.minimum(i, jnp.maximum(nu[0] - 1, 0) // per_step), 0)
    hbm = pl.BlockSpec(memory_space=pl.ANY)
    hb = pl.pallas_call(
        _experts_up_kernel,
        grid_spec=pltpu.PrefetchScalarGridSpec(
            num_scalar_prefetch=2, grid=(n_steps,),
            in_specs=[pl.BlockSpec((RS, W), xrow), hbm, hbm],
            out_specs=pl.BlockSpec((RS, F), lambda i, pn, nu: (i, 0)),
            scratch_shapes=[pltpu.VMEM((2, 2, D, F), F32), pltpu.VMEM((D, 2 * F), BF16),
                            pltpu.SemaphoreType.DMA((2, 2))]),
        out_shape=jax.ShapeDtypeStruct((n_rows, F), BF16),
        compiler_params=_params("arbitrary"), name="experts_up",
    )(plan, n_used, xs, wg, wu)
    return pl.pallas_call(
        _experts_down_kernel,
        grid_spec=pltpu.PrefetchScalarGridSpec(
            num_scalar_prefetch=2, grid=(n_steps,),
            in_specs=[pl.BlockSpec((RS, F), lambda i, pn, nu: (i, 0)), hbm],
            out_specs=pl.BlockSpec((RS, W), lambda i, pn, nu: (i, 0)),
            scratch_shapes=[pltpu.VMEM((2, 1, F, D), F32), pltpu.VMEM((F, D), BF16),
                            pltpu.SemaphoreType.DMA((2, 1))]),
        out_shape=jax.ShapeDtypeStruct((n_rows, W), U32),
        compiler_params=_params("arbitrary"), name="experts_down",
    )(plan, n_used, hb, wd)


def _final_kernel(d8a_ref, d8b_ref, d8n_ref, ys_hbm, hf_ref, hp_ref, w8_ref, wsg_ref, wsu_ref, wsd_ref, g_ref, b_ref,
                  o_ref, buf_a, buf_b, sem, *, alpha):
    i = pl.program_id(0)
    nt = pl.num_programs(0)
    tm = hf_ref.shape[0] // 2
    n = tm * TOP_K

    def issue(d8_ref, rows, buf, s):
        for r in rows:
            pltpu.make_async_copy(ys_hbm.at[pl.ds(d8_ref[0, 0, r], 1), :],
                                  buf.at[r // SUBLANES, pl.ds(r % SUBLANES, 1), :], sem.at[s]).start()

    def wait_all(buf, s):
        def wait(g, c):
            for sub in range(SUBLANES):
                pltpu.make_async_copy(ys_hbm.at[pl.ds(0, 1), :], buf.at[g, pl.ds(sub, 1), :], sem.at[s]).wait()
            return c
        lax.fori_loop(0, n // SUBLANES, wait, 0)

    @pl.when(i == 0)
    def _():
        def first(g, c):
            for sub in range(SUBLANES):
                pltpu.make_async_copy(ys_hbm.at[pl.ds(d8a_ref[0, 0, g * SUBLANES + sub], 1), :],
                                      buf_a.at[g, pl.ds(sub, 1), :], sem.at[0]).start()
            return c
        lax.fori_loop(0, n // SUBLANES, first, 0)

    def shared(rows):
        lo, hi = _unpack_halves(hp_ref[rows, :])
        x = jnp.concatenate([lo.astype(BF16), hi.astype(BF16)], axis=1)
        hs = (_silu(jnp.dot(x, wsg_ref[...], preferred_element_type=F32))
              * jnp.dot(x, wsu_ref[...], preferred_element_type=F32)).astype(BF16)
        return alpha * hf_ref[rows, :] + jnp.dot(hs, wsd_ref[...], preferred_element_type=F32)

    def combine(rows, acc, buf):
        r_lo = r_hi = None
        gk = tm // SUBLANES
        for k in range(TOP_K):
            lo, hi = _unpack_halves(buf[k * gk:(k + 1) * gk].reshape(tm, buf.shape[-1]))
            wk = w8_ref[rows, k:k + 1]
            r_lo = lo * wk if r_lo is None else r_lo + lo * wk
            r_hi = hi * wk if r_hi is None else r_hi + hi * wk
        o_ref[rows, :] = _ln_rows(acc + jnp.concatenate([r_lo, r_hi], axis=1), g_ref[...], b_ref[...])

    rows_a, rows_b = slice(0, tm), slice(tm, 2 * tm)
    issue(d8b_ref, range(0, n // 2), buf_b, 1)
    acc = shared(rows_a)
    wait_all(buf_a, 0)
    issue(d8b_ref, range(n // 2, n), buf_b, 1)
    combine(rows_a, acc, buf_a)
    issue(d8n_ref, range(0, n // 2), buf_a, 0)
    acc = shared(rows_b)
    wait_all(buf_b, 1)
    issue(d8n_ref, range(n // 2, n), buf_a, 0)
    combine(rows_b, acc, buf_b)

    @pl.when(i == nt - 1)
    def _():
        wait_all(buf_a, 0)


def _final(dest8, w8, ys, h1f, h1p, wsg, wsu, wsd, g, b, *, alpha):
    T, D = h1f.shape
    W = ys.shape[1]
    F = wsg.shape[1]
    tm = _tile(T // 2, 128)
    nt = T // (2 * tm)
    d8 = dest8.reshape(2 * nt, tm, TOP_K).transpose(0, 2, 1).reshape(2 * nt, 1, TOP_K * tm)
    row = lambda w: pl.BlockSpec((2 * tm, w), lambda i: (i, 0))
    vec = pl.BlockSpec((1, D), lambda i: (0, 0))
    idx_spec = lambda f: pl.BlockSpec((1, 1, TOP_K * tm), lambda i: (f(i), 0, 0), memory_space=pltpu.SMEM)
    wspec = lambda shape: pl.BlockSpec(shape, lambda i: (0, 0), pipeline_mode=pl.Buffered(1))
    gbuf = pltpu.VMEM((TOP_K * tm // SUBLANES, SUBLANES, W), U32)
    return pl.pallas_call(
        functools.partial(_final_kernel, alpha=alpha), grid=(nt,),
        in_specs=[idx_spec(lambda i: 2 * i), idx_spec(lambda i: 2 * i + 1),
                  idx_spec(lambda i: jnp.minimum(2 * i + 2, 2 * nt - 2)),
                  pl.BlockSpec(memory_space=pl.ANY), row(D), row(W), row(TOP_K),
                  wspec((D, F)), wspec((D, F)), wspec((F, D)), vec, vec],
        out_specs=row(D),
        out_shape=jax.ShapeDtypeStruct((T, D), F32),
        scratch_shapes=[gbuf, gbuf, pltpu.SemaphoreType.DMA((2,))],
        compiler_params=_params("arbitrary"), name="final",
    )(d8, d8, d8, ys, h1f, h1p, w8, wsg, wsu, wsd, g.reshape(1, D), b.reshape(1, D))


def _rope_tables(positions, dh):
    rot = dh // ROT_FRACTION
    half = rot // 2
    inv_freq = jnp.power(ROPE_THETA, -jnp.arange(0, rot, 2, dtype=F32) / rot)
    ang = positions.astype(F32).reshape(-1, 1) * inv_freq
    cos, sin = jnp.cos(ang), jnp.sin(ang)
    T = ang.shape[0]
    cosf = jnp.concatenate([cos, cos, jnp.ones((T, dh - rot), F32)], axis=1)
    sinf = jnp.concatenate([-sin, sin, jnp.zeros((T, dh - rot), F32)], axis=1)
    return cosf, sinf, half


def _plan_kernel(idx_ref, pos_ref, pstart_ref, dest_ref):
    starts = jnp.broadcast_to(pstart_ref[...], idx_ref.shape)
    dest_ref[...] = jnp.take_along_axis(starts, idx_ref[...], axis=1) + pos_ref[...]


def _dispatch_plan(idx, pos, cnt):
    T, E = idx.shape
    R = EXPERT_ROWS
    n_blocks = -(-(T * TOP_K) // R) + E
    counts = cnt.reshape(E).astype(I32)
    padded = (counts + R - 1) // R * R
    pend = jnp.cumsum(padded)
    pstart = pend - padded
    tm = _tile(T, 1024)
    row = pl.BlockSpec((tm, E), lambda i: (i, 0))
    dest = pl.pallas_call(
        _plan_kernel, grid=(T // tm,),
        in_specs=[row, row, pl.BlockSpec((1, E), lambda i: (0, 0))], out_specs=row,
        out_shape=jax.ShapeDtypeStruct((T, E), I32),
        compiler_params=_params("parallel"), name="plan",
    )(idx, pos, pstart.reshape(1, E).astype(I32))
    dest8 = dest[:, :TOP_K]
    blk = jnp.arange(n_blocks, dtype=I32)
    ex = jnp.arange(E, dtype=I32)
    blk_e = jnp.minimum(jnp.sum(pend[None, :] <= (blk * R)[:, None], axis=1), E - 1).astype(I32)
    n_used = (pend[-1:] // R).astype(I32)
    first = (blk < n_used[0]) & ((blk == 0) | (blk_e != jnp.roll(blk_e, 1)))
    slot = (jnp.cumsum(first.astype(I32)) - 1) % 2
    cand = jnp.where(counts > 0, ex, E)
    later = jnp.min(jnp.where(ex[None, :] > ex[:, None], cand[None, :], E), axis=1)
    nxt_e = jnp.where(later < E, later, -1)
    nxt = jnp.sum(jnp.where(blk_e[:, None] == ex[None, :], nxt_e[None, :], 0), axis=1)
    plan = jnp.stack([blk_e, first.astype(I32), slot.astype(I32), nxt.astype(I32)])
    fill_start = (pstart + counts).astype(I32)
    fill_n = (padded - counts).astype(I32)
    return plan, n_used, dest8.astype(I32), fill_start, fill_n, n_blocks * R


def kernel(x, positions, ln_in_g, ln_in_b, w_in, b_gate, lam_q1, lam_k1, lam_q2, lam_k2, subln_g, conv_w, conv_b, w_rg_a, b_rg_a, w_rg_x, b_rg_x, lru_lambda, w_proj_attn, w_proj_rnn, w_out, ln1_g, ln1_b, w_router, router_bias, w_exp_gate, w_exp_up, w_exp_down, w_sh_gate, w_sh_up, w_sh_down, ln2_g, ln2_b):
    B, S, D = x.shape
    T = B * S
    depth = w_in.shape[0]
    alpha = (2 * depth) ** 0.25
    dh = lam_q1.shape[-1]
    att_v = w_proj_attn.shape[1]
    d_rnn = conv_w.shape[-1]
    att_qk = (w_in.shape[2] - att_v - 2 * d_rnn - 2 * D) // 2
    heads = att_v // (2 * dh)
    assert dh == LANES and att_qk == att_v
    cosf, sinf, half = _rope_tables(positions, dh)

    hf, hb = _ln_in(x.reshape(T, D), ln_in_g, ln_in_b)
    for l in range(depth):
        lambda_init = 0.8 - 0.6 * math.exp(-0.3 * l)
        lam = (jnp.exp(jnp.sum(lam_q1[l] * lam_k1[l])) - jnp.exp(jnp.sum(lam_q2[l] * lam_k2[l]))
               + lambda_init).reshape(1).astype(F32)
        proj = _in_proj(hb, w_in[l].astype(BF16), cosf, sinf, att_qk=att_qk,
                        q_scale=dh ** -0.5 * LOG2E, half=half)
        attn = _attention(proj, lam, subln_g[l], B=B, S=S, heads=heads, dh=dh, att_qk=att_qk,
                          lambda_init=lambda_init)
        rnn = _rglru(proj, conv_w[l], conv_b[l], w_rg_a[l], b_rg_a[l], w_rg_x[l], b_rg_x[l], lru_lambda[l],
                     B=B, S=S, off_xr=2 * att_qk + att_v)
        mixed = _mix(attn, rnn, w_proj_attn[l].astype(BF16), w_proj_rnn[l].astype(BF16), proj, b_gate[l],
                     off_g=2 * att_qk + att_v + 2 * d_rnn)
        h1f, h1p = _out_ln(mixed, w_out[l].astype(BF16), hf, ln1_g[l], ln1_b[l], alpha=alpha)
        w8, idx, pos, cnt = _router(h1f, w_router[l], router_bias[l])
        plan, n_used, dest8, fill_start, fill_n, n_rows = _dispatch_plan(idx, pos, cnt)
        xs = _dispatch(h1p, dest8, fill_start, fill_n, n_rows)
        ys = _experts(xs, plan, n_used, w_exp_gate[l], w_exp_up[l], w_exp_down[l])
        hf = _final(dest8, w8[:, :TOP_K], ys, h1f, h1p, w_sh_gate[l].astype(BF16), w_sh_up[l].astype(BF16),
                    w_sh_down[l].astype(BF16), ln2_g[l], ln2_b[l], alpha=alpha)
        if l + 1 < depth:
            hb = hf.astype(BF16)
    return hf.reshape(B, S, D)
```

```python
import functools
import math

import jax
import jax.numpy as jnp
from jax import lax
from jax.experimental import pallas as pl
from jax.experimental.pallas import tpu as pltpu

F32 = jnp.float32
BF16 = jnp.bfloat16
I32 = jnp.int32
U32 = jnp.uint32

CHUNK = 64
ROPE_THETA = 500000.0
ROT_FRACTION = 4
LRU_C = 8.0
CONV_WIDTH = 4
N_GROUPS = 8
TOPK_GROUPS = 4
TOP_K = 8
ROUTED_SCALE = 2.5
LN_EPS = 1e-5
LOG2E = 1.4426950408889634

LANES = 128
SUBLANES = 8
VMEM_LIMIT_BYTES = 56 * 1024 * 1024
EXPERT_ROWS = 256
DMA_UNROLL = 8


def _tile(n, pref):
    t = min(n, pref)
    while n % t:
        t //= 2
    return t


def _params(*sem):
    return pltpu.CompilerParams(dimension_semantics=sem, vmem_limit_bytes=VMEM_LIMIT_BYTES)


def _ln_rows(x, g, b):
    mu = jnp.mean(x, -1, keepdims=True)
    xc = x - mu
    var = jnp.mean(xc * xc, -1, keepdims=True)
    return xc * lax.rsqrt(var + LN_EPS) * g + b


def _silu(x):
    return x * jax.nn.sigmoid(x)


def _pack_halves(y):
    half = y.shape[1] // 2
    bits = lax.bitcast_convert_type(y.astype(BF16).astype(F32), U32)
    return (bits[:, :half] >> 16) | (bits[:, half:] & jnp.uint32(0xFFFF0000))


def _unpack_halves(p):
    lo = lax.bitcast_convert_type(p << 16, F32)
    hi = lax.bitcast_convert_type(p & jnp.uint32(0xFFFF0000), F32)
    return lo, hi


def _ln_in_kernel(x_ref, g_ref, b_ref, yf_ref, yb_ref):
    y = _ln_rows(x_ref[...], g_ref[...], b_ref[...])
    yf_ref[...] = y
    yb_ref[...] = y.astype(BF16)


def _ln_in(x2, g, b):
    T, D = x2.shape
    tm = _tile(T, 256)
    row = pl.BlockSpec((tm, D), lambda i: (i, 0))
    vec = pl.BlockSpec((1, D), lambda i: (0, 0))
    return pl.pallas_call(
        _ln_in_kernel, grid=(T // tm,),
        in_specs=[row, vec, vec], out_specs=[row, row],
        out_shape=[jax.ShapeDtypeStruct((T, D), F32), jax.ShapeDtypeStruct((T, D), BF16)],
        compiler_params=_params("parallel"), name="ln_in",
    )(x2, g.reshape(1, D), b.reshape(1, D))


def _in_proj_kernel(a_ref, w_ref, cos_ref, sin_ref, o_ref, *, n_q, n_qk, q_scale, half, n_chunks):
    j = pl.program_id(1)
    tm, tn = o_ref.shape
    cn = tn // n_chunks
    scale = jnp.where(j < n_q, q_scale, 1.0)
    cosf = jnp.where(j < n_qk, cos_ref[...], 1.0) * scale
    sinf = jnp.where(j < n_qk, sin_ref[...], 0.0) * scale
    low = lax.broadcasted_iota(I32, (tm, LANES), 1) < half
    a = a_ref[...]
    for c in range(n_chunks):
        acc = jnp.dot(a, w_ref[:, c * cn:(c + 1) * cn], preferred_element_type=F32)
        for g in range(cn // LANES):
            x = acc[:, g * LANES:(g + 1) * LANES]
            partner = jnp.where(low, pltpu.roll(x, LANES - half, 1), pltpu.roll(x, half, 1))
            col = c * cn + g * LANES
            o_ref[:, col:col + LANES] = (x * cosf + partner * sinf).astype(BF16)


def _in_proj(h0b, w, cosf, sinf, *, att_qk, q_scale, half):
    T, K = h0b.shape
    N = w.shape[1]
    tm = _tile(T, 1024)
    tn = _tile(att_qk, 1024)
    assert N % tn == 0
    kern = functools.partial(_in_proj_kernel, n_q=att_qk // tn, n_qk=2 * att_qk // tn,
                             q_scale=q_scale, half=half, n_chunks=max(1, tn // 256))
    return pl.pallas_call(
        kern, grid=(T // tm, N // tn),
        in_specs=[pl.BlockSpec((tm, K), lambda i, j: (i, 0)),
                  pl.BlockSpec((K, tn), lambda i, j: (0, j)),
                  pl.BlockSpec((tm, LANES), lambda i, j: (i, 0)),
                  pl.BlockSpec((tm, LANES), lambda i, j: (i, 0))],
        out_specs=pl.BlockSpec((tm, tn), lambda i, j: (i, j)),
        out_shape=jax.ShapeDtypeStruct((T, N), BF16),
        compiler_params=_params("parallel", "arbitrary"), name="in_proj",
    )(h0b, w, cosf, sinf)


def _loop_by_pairs(n, body, carry):
    def pair(c2, carry):
        return body(2 * c2 + 1, body(2 * c2, carry))
    carry = lax.fori_loop(0, n // 2, pair, carry)
    return lax.cond(n % 2 == 1, lambda cr: body(n - 1, cr), lambda cr: cr, carry)


def _attn_kernel(lam_ref, q_ref, k_ref, v_ref, g_ref, o_ref, qt_s, vt_s, st_s, acc_s, *, dh, tk, out_scale):
    i = pl.program_id(2)
    tq = q_ref.shape[0]
    S, wv = v_ref.shape

    @pl.when(i == 0)
    def _():
        for c in range(S // tk):
            vt_s[c, :wv] = v_ref[c * tk:(c + 1) * tk, :].astype(F32).T.astype(BF16)
            vt_s[c, wv:] = jnp.ones((vt_s.shape[1] - wv, tk), BF16)

    qt_s[...] = q_ref[...].astype(F32).T.astype(BF16)

    def scores(c, m):
        r0 = pl.multiple_of(c * tk, tk)
        return jnp.dot(k_ref[pl.ds(r0, tk), m * dh:(m + 1) * dh], qt_s[m * dh:(m + 1) * dh, :],
                       preferred_element_type=F32)

    def phase1(c, mx):
        out = []
        for m in range(2):
            st = scores(c, m)
            st_s[c, m] = st
            out.append(jnp.maximum(mx[m], jnp.max(st, 0, keepdims=True)))
        return tuple(out)

    mx = _loop_by_pairs(i, phase1, tuple(jnp.full((1, tq), -jnp.inf, F32) for _ in range(2)))
    visible = ((lax.broadcasted_iota(I32, (tk, tq), 0) // CHUNK)
               <= (lax.broadcasted_iota(I32, (tk, tq), 1) // CHUNK))
    mxs = []
    for m in range(2):
        st = jnp.where(visible, scores(i, m), -jnp.inf)
        st_s[i, m] = st
        mxs.append(jnp.maximum(mx[m], jnp.max(st, 0, keepdims=True)))

    acc_s[...] = jnp.zeros_like(acc_s)

    def phase2(c, carry):
        vt = vt_s[c]
        for m in range(2):
            p = jnp.exp2(st_s[c, m] - mxs[m])
            acc_s[m] += jnp.dot(vt, p.astype(BF16), preferred_element_type=F32)
        return carry

    _loop_by_pairs(i + 1, phase2, 0)
    o0, o1 = (acc_s[m, :wv] * (1.0 / acc_s[m, wv:wv + 1]) for m in range(2))
    ot = o0 - lam_ref[0] * o1
    ot = ot * lax.rsqrt(jnp.mean(ot * ot, 0, keepdims=True) + LN_EPS)
    o_ref[...] = (ot.T * (g_ref[...] * out_scale)).astype(BF16)


def _attention(proj, lam, subln_g, *, B, S, heads, dh, att_qk, lambda_init):
    T = B * S
    wv = 2 * dh
    tq = _tile(S, 512)
    tk = tq
    nq = S // tq
    kern = functools.partial(_attn_kernel, dh=dh, tk=tk, out_scale=1.0 - lambda_init)
    ones_rows = 2 * SUBLANES
    scratch = [pltpu.VMEM((wv, tq), BF16), pltpu.VMEM((S // tk, wv + ones_rows, tk), BF16),
               pltpu.VMEM((S // tk, 2, tk, tq), F32), pltpu.VMEM((2, wv + ones_rows, tq), F32)]
    return pl.pallas_call(
        kern, grid=(B, heads, nq),
        in_specs=[pl.BlockSpec(memory_space=pltpu.SMEM),
                  pl.BlockSpec((tq, wv), lambda b, h, i: (b * nq + i, h)),
                  pl.BlockSpec((S, wv), lambda b, h, i: (b, att_qk // wv + h)),
                  pl.BlockSpec((S, wv), lambda b, h, i: (b, 2 * att_qk // wv + h)),
                  pl.BlockSpec((1, wv), lambda b, h, i: (0, 0))],
        out_specs=pl.BlockSpec((tq, wv), lambda b, h, i: (b * nq + i, h)),
        out_shape=jax.ShapeDtypeStruct((T, heads * wv), BF16),
        scratch_shapes=scratch,
        compiler_params=_params("parallel", "parallel", "arbitrary"), name="attention",
    )(lam, proj, proj, proj, subln_g.reshape(1, wv))


def _gelu_tanh(x):
    return 0.5 * x * (1.0 + jnp.tanh(0.7978845608028654 * (x + 0.044715 * (x * x * x))))


def _rglru_kernel(xr_ref, yr_ref, cw_ref, cb_ref, wa_ref, ba_ref, wx_ref, bx_ref, lam_ref, o_ref,
                  a_s, u_s, xprev_s, h_s, *, n_heads, blk, scan_cols):
    @pl.when(pl.program_id(1) == 0)
    def _():
        xprev_s[...] = jnp.zeros_like(xprev_s)
        h_s[...] = jnp.zeros_like(h_s)

    tt, D = xr_ref.shape
    z = -lam_ref[...]
    sp = jnp.maximum(z, 0.0) + jnp.log1p(jnp.exp(-jnp.abs(z)))
    for hh in range(n_heads):
        sl = slice(hh * blk, (hh + 1) * blk)
        xe = jnp.concatenate([xprev_s[:, sl], xr_ref[:, sl].astype(F32)], axis=0)
        xc = cb_ref[:, sl]
        for j in range(CONV_WIDTH):
            o = SUBLANES - (CONV_WIDTH - 1) + j
            xc = xc + cw_ref[j:j + 1, sl] * xe[o:o + tt]
        xb = xc.astype(BF16)
        r = jax.nn.sigmoid(jnp.dot(xb, wa_ref[hh], preferred_element_type=F32) + ba_ref[:, sl])
        ig = jax.nn.sigmoid(jnp.dot(xb, wx_ref[hh], preferred_element_type=F32) + bx_ref[:, sl])
        log_a = (-LRU_C * r) * sp[:, sl]
        a = jnp.exp(log_a)
        a_s[:, sl] = a
        u_s[:, sl] = jnp.sqrt(-jnp.tanh(log_a) * (a * a + 1.0)) * ig * xc
    xprev_s[...] = xr_ref[tt - SUBLANES:tt, :].astype(F32)

    row = lax.broadcasted_iota(I32, (SUBLANES, scan_cols), 0)

    def scan_body(g, h):
        r0 = pl.multiple_of(g * SUBLANES, SUBLANES)
        outs = []
        for c in range(D // scan_cols):
            cs = slice(c * scan_cols, (c + 1) * scan_cols)
            a = a_s[pl.ds(r0, SUBLANES), cs]
            u = u_s[pl.ds(r0, SUBLANES), cs]
            for s in (1, 2, 4):
                valid = row >= s
                u = jnp.where(valid, a * pltpu.roll(u, s, 0) + u, u)
                a = jnp.where(valid, a * pltpu.roll(a, s, 0), a)
            hg = u + a * h[:, cs]
            u_s[pl.ds(r0, SUBLANES), cs] = hg
            outs.append(hg[SUBLANES - 1:SUBLANES, :])
        return jnp.concatenate(outs, axis=1)

    h_s[...] = lax.fori_loop(0, tt // SUBLANES, scan_body, h_s[...])
    o_ref[...] = (u_s[...] * _gelu_tanh(yr_ref[...].astype(F32))).astype(BF16)


def _rglru(proj, conv_w, conv_b, w_ra, b_ra, w_rx, b_rx, lru_lambda, *, B, S, off_xr):
    T = B * S
    n_heads, blk, _ = w_ra.shape
    D = n_heads * blk
    tt = _tile(S, 256)
    nt = S // tt
    assert off_xr % D == 0
    cx = off_xr // D
    vec = pl.BlockSpec((1, D), lambda b, t: (0, 0))
    wspec = pl.BlockSpec((n_heads, blk, blk), lambda b, t: (0, 0, 0))
    kern = functools.partial(_rglru_kernel, n_heads=n_heads, blk=blk, scan_cols=_tile(D, 512))
    return pl.pallas_call(
        kern, grid=(B, nt),
        in_specs=[pl.BlockSpec((tt, D), lambda b, t: (b * nt + t, cx)),
                  pl.BlockSpec((tt, D), lambda b, t: (b * nt + t, cx + 1)),
                  pl.BlockSpec((CONV_WIDTH, D), lambda b, t: (0, 0)), vec,
                  wspec, vec, wspec, vec, vec],
        out_specs=pl.BlockSpec((tt, D), lambda b, t: (b * nt + t, 0)),
        out_shape=jax.ShapeDtypeStruct((T, D), BF16),
        scratch_shapes=[pltpu.VMEM((tt, D), F32), pltpu.VMEM((tt, D), F32),
                        pltpu.VMEM((SUBLANES, D), F32), pltpu.VMEM((1, D), F32)],
        compiler_params=_params("parallel", "arbitrary"), name="rglru",
    )(proj, proj, conv_w, conv_b.reshape(1, D), w_ra.astype(BF16), b_ra.reshape(1, D),
      w_rx.astype(BF16), b_rx.reshape(1, D), lru_lambda.reshape(1, D))


def _mix_kernel(at_ref, rn_ref, wa_ref, wr_ref, g0_ref, g1_ref, b0_ref, b1_ref, o_ref):
    pa = jnp.dot(at_ref[...], wa_ref[...], preferred_element_type=F32)
    pr = jnp.dot(rn_ref[...], wr_ref[...], preferred_element_type=F32)
    s0 = jax.nn.sigmoid(g0_ref[...].astype(F32) + b0_ref[...])
    s1 = jax.nn.sigmoid(g1_ref[...].astype(F32) + b1_ref[...])
    o_ref[...] = (s0 * pa + s1 * pr).astype(BF16)


def _mix(attn, rnn, wa, wr, proj, b_gate, *, off_g):
    T, Ka = attn.shape
    Kr = rnn.shape[1]
    D = wa.shape[1]
    tm = _tile(T, 512)
    tn = _tile(D, 512)
    assert off_g % tn == 0
    cg = off_g // tn
    nn = D // tn
    return pl.pallas_call(
        _mix_kernel, grid=(T // tm, nn),
        in_specs=[pl.BlockSpec((tm, Ka), lambda i, j: (i, 0)),
                  pl.BlockSpec((tm, Kr), lambda i, j: (i, 0)),
                  pl.BlockSpec((Ka, tn), lambda i, j: (0, j)),
                  pl.BlockSpec((Kr, tn), lambda i, j: (0, j)),
                  pl.BlockSpec((tm, tn), lambda i, j: (i, cg + j)),
                  pl.BlockSpec((tm, tn), lambda i, j: (i, cg + nn + j)),
                  pl.BlockSpec((1, tn), lambda i, j: (0, j)),
                  pl.BlockSpec((1, tn), lambda i, j: (0, nn + j))],
        out_specs=pl.BlockSpec((tm, tn), lambda i, j: (i, j)),
        out_shape=jax.ShapeDtypeStruct((T, D), BF16),
        compiler_params=_params("parallel", "arbitrary"), name="mix",
    )(attn, rnn, wa, wr, proj, proj, b_gate.reshape(1, 2 * D), b_gate.reshape(1, 2 * D))


def _out_ln_kernel(a_ref, w_ref, h0_ref, g_ref, b_ref, hf_ref, hp_ref, acc_s, *, alpha):
    j = pl.program_id(1)
    n_n, _, tn = acc_s.shape
    acc_s[j] = alpha * h0_ref[...] + jnp.dot(a_ref[...], w_ref[...], preferred_element_type=F32)

    @pl.when(j == n_n - 1)
    def _():
        D = n_n * tn
        mu = sum(jnp.sum(acc_s[c], -1, keepdims=True) for c in range(n_n)) / D
        var = sum(jnp.sum(jnp.square(acc_s[c] - mu), -1, keepdims=True) for c in range(n_n)) / D
        rstd = lax.rsqrt(var + LN_EPS)
        hn = n_n // 2
        for c in range(hn):
            ys = []
            for cc in (c, c + hn):
                cs = slice(cc * tn, (cc + 1) * tn)
                y = (acc_s[cc] - mu) * rstd * g_ref[:, cs] + b_ref[:, cs]
                hf_ref[:, cs] = y
                ys.append(y)
            hp_ref[:, c * tn:(c + 1) * tn] = _pack_halves(jnp.concatenate(ys, axis=1))


def _out_ln(mixed, w, h0f, g, b, *, alpha):
    T, K = mixed.shape
    D = w.shape[1]
    tm = _tile(T, 512)
    tn = _tile(D // 2, 512)
    vec = pl.BlockSpec((1, D), lambda i, j: (0, 0))
    return pl.pallas_call(
        functools.partial(_out_ln_kernel, alpha=alpha), grid=(T // tm, D // tn),
        in_specs=[pl.BlockSpec((tm, K), lambda i, j: (i, 0), pipeline_mode=pl.Buffered(1)),
                  pl.BlockSpec((K, tn), lambda i, j: (0, j)),
                  pl.BlockSpec((tm, tn), lambda i, j: (i, j)), vec, vec],
        out_specs=[pl.BlockSpec((tm, D), lambda i, j: (i, 0)),
                   pl.BlockSpec((tm, D // 2), lambda i, j: (i, 0))],
        out_shape=[jax.ShapeDtypeStruct((T, D), F32), jax.ShapeDtypeStruct((T, D // 2), U32)],
        scratch_shapes=[pltpu.VMEM((D // tn, tm, tn), F32)],
        compiler_params=_params("parallel", "arbitrary"), name="out_ln",
    )(mixed, w, h0f, g.reshape(1, D), b.reshape(1, D))


def _seg_allreduce(x, lane, width, op):
    n = x.shape[1]
    s = 1
    while s < width:
        partner = jnp.where((lane & s) == 0, pltpu.roll(x, n - s, 1), pltpu.roll(x, s, 1))
        x = op(x, partner)
        s *= 2
    return x


def _router_kernel(h_ref, w_ref, bias_ref, w8_ref, idx_ref, pos_ref, cnt_ref, tri_s, run_s, w2_s):
    E = w_ref.shape[1]

    @pl.when(pl.program_id(0) == 0)
    def _():
        tm = tri_s.shape[0]
        tri_s[...] = jnp.where(lax.broadcasted_iota(I32, (tm, tm), 1) < lax.broadcasted_iota(I32, (tm, tm), 0),
                               1.0, 0.0).astype(BF16)
        run_s[...] = jnp.zeros_like(run_s)
        w = w_ref[...]
        w_hi = w.astype(BF16)
        w2_s[:, :E] = w_hi
        w2_s[:, E:] = (w - w_hi.astype(F32)).astype(BF16)

    h = h_ref[...]
    h_hi = h.astype(BF16)
    h_lo = (h - h_hi.astype(F32)).astype(BF16)
    hh = jnp.dot(h_hi, w2_s[...], preferred_element_type=F32)
    logits = hh[:, :E] + (hh[:, E:] + jnp.dot(h_lo, w2_s[:, :E], preferred_element_type=F32))
    scores = jax.nn.sigmoid(logits)
    biased = scores + bias_ref[...]
    gw = E // N_GROUPS
    lane = lax.broadcasted_iota(I32, biased.shape, 1)
    neg = -jnp.inf

    m1 = _seg_allreduce(biased, lane, gw, jnp.maximum)
    is_m1 = biased == m1
    cnt = _seg_allreduce(is_m1.astype(F32), lane, gw, jnp.add)
    m2 = jnp.where(cnt >= 2.0, m1, _seg_allreduce(jnp.where(is_m1, neg, biased), lane, gw, jnp.maximum))
    gs = m1 + m2

    gi = lane // gw
    rank = jnp.zeros(biased.shape, I32)
    for k in range(1, N_GROUPS):
        other = pltpu.roll(gs, k * gw, 1)
        ogi = (gi - k) % N_GROUPS
        beats = (other > gs) | ((other == gs) & (ogi < gi))
        rank = rank + beats.astype(I32)
    cur = jnp.where(rank < TOPK_GROUPS, biased, neg)

    sel = jnp.zeros(biased.shape, jnp.bool_)
    firsts = []
    for k in range(TOP_K):
        mx = jnp.max(cur, -1, keepdims=True)
        first = jnp.min(jnp.where(cur == mx, lane, E), -1, keepdims=True)
        pick = lane == first
        sel = sel | pick
        firsts.append(first)
        cur = jnp.where(pick, neg, cur)

    sel_f = jnp.where(sel, 1.0, 0.0)
    rank_in_e = run_s[...] + jnp.dot(tri_s[...], sel_f.astype(BF16), preferred_element_type=F32)
    run_s[...] = run_s[...] + jnp.sum(sel_f, 0, keepdims=True)
    cnt_ref[...] = run_s[...]

    wsum = jnp.sum(jnp.where(sel, scores, 0.0), -1, keepdims=True)
    w8 = jnp.zeros(biased.shape, F32)
    idx = jnp.zeros(biased.shape, I32)
    pos = jnp.zeros(biased.shape, F32)
    for k in range(TOP_K):
        pick = lane == firsts[k]
        wk = jnp.sum(jnp.where(pick, scores, 0.0), -1, keepdims=True)
        pk = jnp.sum(jnp.where(pick, rank_in_e, 0.0), -1, keepdims=True)
        w8 = jnp.where(lane == k, wk / wsum * ROUTED_SCALE, w8)
        idx = jnp.where(lane == k, firsts[k], idx)
        pos = jnp.where(lane == k, pk, pos)
    w8_ref[...] = w8
    idx_ref[...] = idx
    pos_ref[...] = pos.astype(I32)


def _router(h1f, w_router, router_bias):
    T, D = h1f.shape
    E = w_router.shape[1]
    assert E == LANES and E % N_GROUPS == 0
    tm = _tile(T, 512)
    row = pl.BlockSpec((tm, E), lambda i: (i, 0))
    return pl.pallas_call(
        _router_kernel, grid=(T // tm,),
        in_specs=[pl.BlockSpec((tm, D), lambda i: (i, 0)),
                  pl.BlockSpec((D, E), lambda i: (0, 0)),
                  pl.BlockSpec((1, E), lambda i: (0, 0))],
        out_specs=[row, row, row, pl.BlockSpec((1, E), lambda i: (0, 0))],
        out_shape=[jax.ShapeDtypeStruct((T, E), F32), jax.ShapeDtypeStruct((T, E), I32),
                   jax.ShapeDtypeStruct((T, E), I32), jax.ShapeDtypeStruct((1, E), F32)],
        scratch_shapes=[pltpu.VMEM((tm, tm), BF16), pltpu.VMEM((1, E), F32), pltpu.VMEM((D, 2 * E), BF16)],
        compiler_params=_params("arbitrary"), name="router",
    )(h1f, w_router, router_bias.reshape(1, E))


def _dispatch_kernel(fill_start_ref, fill_n_ref, d8_ref, hp_ref, xs_hbm, zero_s, sem, zsem):
    tm = hp_ref.shape[0]
    E = fill_n_ref.shape[0]

    def row_copy(r, k):
        return pltpu.make_async_copy(hp_ref.at[pl.ds(r, 1), :],
                                     xs_hbm.at[pl.ds(d8_ref[0, 0, r * TOP_K + k], 1), :], sem.at[0])

    def issue(r, c):
        for k in range(TOP_K):
            row_copy(r, k).start()
        return c
    lax.fori_loop(0, tm, issue, 0)

    @pl.when(pl.program_id(0) == 0)
    def _():
        zero_s[...] = jnp.zeros_like(zero_s)
        sizes = [1 << b for b in reversed(range(3, EXPERT_ROWS.bit_length() - 1))]

        def per_expert(e, c):
            s0 = fill_start_ref[e]
            n = fill_n_ref[e]
            head = jnp.minimum((-s0) % SUBLANES, n)
            body = (n - head) // SUBLANES * SUBLANES
            tail = n - head - body
            for start in (True, False):
                def fill(row, size, pred):
                    cp = pltpu.make_async_copy(zero_s.at[pl.ds(0, size), :], xs_hbm.at[pl.ds(row, size), :],
                                               zsem.at[0])

                    @pl.when(pred)
                    def _():
                        cp.start() if start else cp.wait()

                for r in range(SUBLANES - 1):
                    fill(s0 + r, 1, r < head)
                off = s0 + head
                for b in sizes:
                    fill(pl.multiple_of(off, SUBLANES), b, (body & b) != 0)
                    off = off + (body & b)
                for r in range(SUBLANES - 1):
                    fill(off + r, 1, r < tail)
            return c
        lax.fori_loop(0, E, per_expert, 0)

    def wait(r, c):
        for k in range(TOP_K):
            row_copy(r, k).wait()
        return c
    lax.fori_loop(0, tm, wait, 0)


def _dispatch(h1p, dest8, fill_start, fill_n, n_rows):
    T, W = h1p.shape
    tm = _tile(T, 512)
    nt = T // tm
    d8 = dest8.reshape(nt, 1, tm * TOP_K)
    grid_spec = pltpu.PrefetchScalarGridSpec(
        num_scalar_prefetch=2, grid=(nt,),
        in_specs=[pl.BlockSpec((1, 1, tm * TOP_K), lambda i, fs, fn: (i, 0, 0), memory_space=pltpu.SMEM),
                  pl.BlockSpec((tm, W), lambda i, fs, fn: (i, 0))],
        out_specs=pl.BlockSpec(memory_space=pl.ANY),
        scratch_shapes=[pltpu.VMEM((EXPERT_ROWS // 2, W), U32), pltpu.SemaphoreType.DMA((1,)),
                        pltpu.SemaphoreType.DMA((1,))])
    return pl.pallas_call(
        _dispatch_kernel, grid_spec=grid_spec,
        out_shape=jax.ShapeDtypeStruct((n_rows, W), U32),
        compiler_params=_params("arbitrary"), name="dispatch",
    )(fill_start, fill_n, d8, h1p)


def _expert_switch(plan_ref, i, w_hbm, wbuf, sem, recast):
    def copies(e, slot):
        return [pltpu.make_async_copy(w.at[e], wbuf.at[slot, t], sem.at[slot, t]) for t, w in enumerate(w_hbm)]

    @pl.when(plan_ref[1, i] == 1)
    def _():
        e, slot, nxt = plan_ref[0, i], plan_ref[2, i], plan_ref[3, i]

        @pl.when(i == 0)
        def _():
            for c in copies(e, slot):
                c.start()

        for c in copies(e, slot):
            c.wait()
        recast(slot)

        @pl.when(nxt >= 0)
        def _():
            for c in copies(nxt, 1 - slot):
                c.start()


def _expert_step(plan_ref, n_used_ref, n_rows, switch, compute, clear):
    R = EXPERT_ROWS
    assert n_rows == 2 * R
    i0 = pl.program_id(0) * 2
    merged = jnp.logical_and(i0 + 1 < n_used_ref[0], plan_ref[1, i0 + 1] == 0)

    @pl.when(merged)
    def _():
        switch(i0)
        compute(slice(0, 2 * R))

    @pl.when(jnp.logical_not(merged))
    def _():
        for h in range(2):
            rows = slice(h * R, (h + 1) * R)

            @pl.when(i0 + h < n_used_ref[0])
            def _():
                switch(i0 + h)
                compute(rows)

            @pl.when(i0 + h >= n_used_ref[0])
            def _():
                clear(rows)


def _experts_up_kernel(plan_ref, n_used_ref, xs_ref, wg_hbm, wu_hbm, hb_ref, wbuf, wgu_s, sem):
    F = wg_hbm.shape[2]

    def recast(slot):
        wgu_s[:, :F] = wbuf[slot, 0].astype(BF16)
        wgu_s[:, F:] = wbuf[slot, 1].astype(BF16)

    def compute(rows):
        lo, hi = _unpack_halves(xs_ref[rows, :])
        hw = lo.shape[1]
        gu = (jnp.dot(lo.astype(BF16), wgu_s[:hw, :], preferred_element_type=F32)
              + jnp.dot(hi.astype(BF16), wgu_s[hw:, :], preferred_element_type=F32))
        hb_ref[rows, :] = (_silu(gu[:, :F]) * gu[:, F:]).astype(BF16)

    def clear(rows):
        hb_ref[rows, :] = jnp.zeros((rows.stop - rows.start, F), BF16)

    _expert_step(plan_ref, n_used_ref, xs_ref.shape[0],
                 lambda i: _expert_switch(plan_ref, i, (wg_hbm, wu_hbm), wbuf, sem, recast), compute, clear)


def _experts_down_kernel(plan_ref, n_used_ref, hb_ref, wd_hbm, ys_ref, wbuf, wd_s, sem):
    def recast(slot):
        wd_s[...] = wbuf[slot, 0].astype(BF16)

    def compute(rows):
        ys_ref[rows, :] = _pack_halves(jnp.dot(hb_ref[rows, :], wd_s[...], preferred_element_type=F32))

    def clear(rows):
        ys_ref[rows, :] = jnp.zeros((rows.stop - rows.start, ys_ref.shape[1]), U32)

    _expert_step(plan_ref, n_used_ref, hb_ref.shape[0],
                 lambda i: _expert_switch(plan_ref, i, (wd_hbm,), wbuf, sem, recast), compute, clear)


def _experts(xs, plan, n_used, wg, wu, wd):
    n_rows, W = xs.shape
    E, D, F = wg.shape
    R = EXPERT_ROWS
    per_step = 2
    n_steps = n_rows // (per_step * R)
    assert n_steps * per_step * R == n_rows
    RS = per_step * R
    xrow = lambda i, pn, nu: (jnp.minimum(i, jnp.maximum(nu[0] - 1, 0) // per_step), 0)
    hbm = pl.BlockSpec(memory_space=pl.ANY)
    hb = pl.pallas_call(
        _experts_up_kernel,
        grid_spec=pltpu.PrefetchScalarGridSpec(
            num_scalar_prefetch=2, grid=(n_steps,),
            in_specs=[pl.BlockSpec((RS, W), xrow), hbm, hbm],
            out_specs=pl.BlockSpec((RS, F), lambda i, pn, nu: (i, 0)),
            scratch_shapes=[pltpu.VMEM((2, 2, D, F), F32), pltpu.VMEM((D, 2 * F), BF16),
                            pltpu.SemaphoreType.DMA((2, 2))]),
        out_shape=jax.ShapeDtypeStruct((n_rows, F), BF16),
        compiler_params=_params("arbitrary"), name="experts_up",
    )(plan, n_used, xs, wg, wu)
    return pl.pallas_call(
        _experts_down_kernel,
        grid_spec=pltpu.PrefetchScalarGridSpec(
            num_scalar_prefetch=2, grid=(n_steps,),
            in_specs=[pl.BlockSpec((RS, F), lambda i, pn, nu: (i, 0)), hbm],
            out_specs=pl.BlockSpec((RS, W), lambda i, pn, nu: (i, 0)),
            scratch_shapes=[pltpu.VMEM((2, 1, F, D), F32), pltpu.VMEM((F, D), BF16),
                            pltpu.SemaphoreType.DMA((2, 1))]),
        out_shape=jax.ShapeDtypeStruct((n_rows, W), U32),
        compiler_params=_params("arbitrary"), name="experts_down",
    )(plan, n_used, hb, wd)


def _final_kernel(d8a_ref, d8b_ref, d8n_ref, ys_hbm, hf_ref, hp_ref, w8_ref, wsg_ref, wsu_ref, wsd_ref, g_ref, b_ref,
                  o_ref, buf_a, buf_b, sem, *, alpha):
    i = pl.program_id(0)
    nt = pl.num_programs(0)
    tm = hf_ref.shape[0] // 2
    n = tm * TOP_K

    def issue(d8_ref, rows, buf, s):
        for r in rows:
            pltpu.make_async_copy(ys_hbm.at[pl.ds(d8_ref[0, 0, r], 1), :],
                                  buf.at[r // SUBLANES, pl.ds(r % SUBLANES, 1), :], sem.at[s]).start()

    def wait_all(buf, s):
        def wait(g, c):
            for sub in range(SUBLANES):
                pltpu.make_async_copy(ys_hbm.at[pl.ds(0, 1), :], buf.at[g, pl.ds(sub, 1), :], sem.at[s]).wait()
            return c
        lax.fori_loop(0, n // SUBLANES, wait, 0)

    @pl.when(i == 0)
    def _():
        def first(g, c):
            for sub in range(SUBLANES):
                pltpu.make_async_copy(ys_hbm.at[pl.ds(d8a_ref[0, 0, g * SUBLANES + sub], 1), :],
                                      buf_a.at[g, pl.ds(sub, 1), :], sem.at[0]).start()
            return c
        lax.fori_loop(0, n // SUBLANES, first, 0)

    def shared(rows):
        lo, hi = _unpack_halves(hp_ref[rows, :])
        x = jnp.concatenate([lo.astype(BF16), hi.astype(BF16)], axis=1)
        hs = (_silu(jnp.dot(x, wsg_ref[...], preferred_element_type=F32))
              * jnp.dot(x, wsu_ref[...], preferred_element_type=F32)).astype(BF16)
        return alpha * hf_ref[rows, :] + jnp.dot(hs, wsd_ref[...], preferred_element_type=F32)

    def combine(rows, acc, buf):
        r_lo = r_hi = None
        gk = tm // SUBLANES
        for k in range(TOP_K):
            lo, hi = _unpack_halves(buf[k * gk:(k + 1) * gk].reshape(tm, buf.shape[-1]))
            wk = w8_ref[rows, k:k + 1]
            r_lo = lo * wk if r_lo is None else r_lo + lo * wk
            r_hi = hi * wk if r_hi is None else r_hi + hi * wk
        o_ref[rows, :] = _ln_rows(acc + jnp.concatenate([r_lo, r_hi], axis=1), g_ref[...], b_ref[...])

    rows_a, rows_b = slice(0, tm), slice(tm, 2 * tm)
    issue(d8b_ref, range(0, n // 2), buf_b, 1)
    acc = shared(rows_a)
    wait_all(buf_a, 0)
    issue(d8b_ref, range(n // 2, n), buf_b, 1)
    combine(rows_a, acc, buf_a)
    issue(d8n_ref, range(0, n // 2), buf_a, 0)
    acc = shared(rows_b)
    wait_all(buf_b, 1)
    issue(d8n_ref, range(n // 2, n), buf_a, 0)
    combine(rows_b, acc, buf_b)

    @pl.when(i == nt - 1)
    def _():
        wait_all(buf_a, 0)


def _final(dest8, w8, ys, h1f, h1p, wsg, wsu, wsd, g, b, *, alpha):
    T, D = h1f.shape
    W = ys.shape[1]
    F = wsg.shape[1]
    tm = _tile(T // 2, 128)
    nt = T // (2 * tm)
    d8 = dest8.reshape(2 * nt, tm, TOP_K).transpose(0, 2, 1).reshape(2 * nt, 1, TOP_K * tm)
    row = lambda w: pl.BlockSpec((2 * tm, w), lambda i: (i, 0))
    vec = pl.BlockSpec((1, D), lambda i: (0, 0))
    idx_spec = lambda f: pl.BlockSpec((1, 1, TOP_K * tm), lambda i: (f(i), 0, 0), memory_space=pltpu.SMEM)
    wspec = lambda shape: pl.BlockSpec(shape, lambda i: (0, 0), pipeline_mode=pl.Buffered(1))
    gbuf = pltpu.VMEM((TOP_K * tm // SUBLANES, SUBLANES, W), U32)
    return pl.pallas_call(
        functools.partial(_final_kernel, alpha=alpha), grid=(nt,),
        in_specs=[idx_spec(lambda i: 2 * i), idx_spec(lambda i: 2 * i + 1),
                  idx_spec(lambda i: jnp.minimum(2 * i + 2, 2 * nt - 2)),
                  pl.BlockSpec(memory_space=pl.ANY), row(D), row(W), row(TOP_K),
                  wspec((D, F)), wspec((D, F)), wspec((F, D)), vec, vec],
        out_specs=row(D),
        out_shape=jax.ShapeDtypeStruct((T, D), F32),
        scratch_shapes=[gbuf, gbuf, pltpu.SemaphoreType.DMA((2,))],
        compiler_params=_params("arbitrary"), name="final",
    )(d8, d8, d8, ys, h1f, h1p, w8, wsg, wsu, wsd, g.reshape(1, D), b.reshape(1, D))


def _rope_tables(positions, dh):
    rot = dh // ROT_FRACTION
    half = rot // 2
    inv_freq = jnp.power(ROPE_THETA, -jnp.arange(0, rot, 2, dtype=F32) / rot)
    ang = positions.astype(F32).reshape(-1, 1) * inv_freq
    cos, sin = jnp.cos(ang), jnp.sin(ang)
    T = ang.shape[0]
    cosf = jnp.concatenate([cos, cos, jnp.ones((T, dh - rot), F32)], axis=1)
    sinf = jnp.concatenate([-sin, sin, jnp.zeros((T, dh - rot), F32)], axis=1)
    return cosf, sinf, half


def _plan_kernel(idx_ref, pos_ref, pstart_ref, dest_ref):
    starts = jnp.broadcast_to(pstart_ref[...], idx_ref.shape)
    dest_ref[...] = jnp.take_along_axis(starts, idx_ref[...], axis=1) + pos_ref[...]


def _dispatch_plan(idx, pos, cnt):
    T, E = idx.shape
    R = EXPERT_ROWS
    n_blocks = -(-(T * TOP_K) // R) + E
    counts = cnt.reshape(E).astype(I32)
    padded = (counts + R - 1) // R * R
    pend = jnp.cumsum(padded)
    pstart = pend - padded
    tm = _tile(T, 1024)
    row = pl.BlockSpec((tm, E), lambda i: (i, 0))
    dest = pl.pallas_call(
        _plan_kernel, grid=(T // tm,),
        in_specs=[row, row, pl.BlockSpec((1, E), lambda i: (0, 0))], out_specs=row,
        out_shape=jax.ShapeDtypeStruct((T, E), I32),
        compiler_params=_params("parallel"), name="plan",
    )(idx, pos, pstart.reshape(1, E).astype(I32))
    dest8 = dest[:, :TOP_K]
    blk = jnp.arange(n_blocks, dtype=I32)
    ex = jnp.arange(E, dtype=I32)
    blk_e = jnp.minimum(jnp.sum(pend[None, :] <= (blk * R)[:, None], axis=1), E - 1).astype(I32)
    n_used = (pend[-1:] // R).astype(I32)
    first = (blk < n_used[0]) & ((blk == 0) | (blk_e != jnp.roll(blk_e, 1)))
    slot = (jnp.cumsum(first.astype(I32)) - 1) % 2
    cand = jnp.where(counts > 0, ex, E)
    later = jnp.min(jnp.where(ex[None, :] > ex[:, None], cand[None, :], E), axis=1)
    nxt_e = jnp.where(later < E, later, -1)
    nxt = jnp.sum(jnp.where(blk_e[:, None] == ex[None, :], nxt_e[None, :], 0), axis=1)
    plan = jnp.stack([blk_e, first.astype(I32), slot.astype(I32), nxt.astype(I32)])
    fill_start = (pstart + counts).astype(I32)
    fill_n = (padded - counts).astype(I32)
    return plan, n_used, dest8.astype(I32), fill_start, fill_n, n_blocks * R


def kernel(x, positions, ln_in_g, ln_in_b, w_in, b_gate, lam_q1, lam_k1, lam_q2, lam_k2, subln_g, conv_w, conv_b, w_rg_a, b_rg_a, w_rg_x, b_rg_x, lru_lambda, w_proj_attn, w_proj_rnn, w_out, ln1_g, ln1_b, w_router, router_bias, w_exp_gate, w_exp_up, w_exp_down, w_sh_gate, w_sh_up, w_sh_down, ln2_g, ln2_b):
    B, S, D = x.shape
    T = B * S
    depth = w_in.shape[0]
    alpha = (2 * depth) ** 0.25
    dh = lam_q1.shape[-1]
    att_v = w_proj_attn.shape[1]
    d_rnn = conv_w.shape[-1]
    att_qk = (w_in.shape[2] - att_v - 2 * d_rnn - 2 * D) // 2
    heads = att_v // (2 * dh)
    assert dh == LANES and att_qk == att_v
    cosf, sinf, half = _rope_tables(positions, dh)

    hf, hb = _ln_in(x.reshape(T, D), ln_in_g, ln_in_b)
    for l in range(depth):
        lambda_init = 0.8 - 0.6 * math.exp(-0.3 * l)
        lam = (jnp.exp(jnp.sum(lam_q1[l] * lam_k1[l])) - jnp.exp(jnp.sum(lam_q2[l] * lam_k2[l]))
               + lambda_init).reshape(1).astype(F32)
        proj = _in_proj(hb, w_in[l].astype(BF16), cosf, sinf, att_qk=att_qk,
                        q_scale=dh ** -0.5 * LOG2E, half=half)
        attn = _attention(proj, lam, subln_g[l], B=B, S=S, heads=heads, dh=dh, att_qk=att_qk,
                          lambda_init=lambda_init)
        rnn = _rglru(proj, conv_w[l], conv_b[l], w_rg_a[l], b_rg_a[l], w_rg_x[l], b_rg_x[l], lru_lambda[l],
                     B=B, S=S, off_xr=2 * att_qk + att_v)
        mixed = _mix(attn, rnn, w_proj_attn[l].astype(BF16), w_proj_rnn[l].astype(BF16), proj, b_gate[l],
                     off_g=2 * att_qk + att_v + 2 * d_rnn)
        h1f, h1p = _out_ln(mixed, w_out[l].astype(BF16), hf, ln1_g[l], ln1_b[l], alpha=alpha)
        w8, idx, pos, cnt = _router(h1f, w_router[l], router_bias[l])
        plan, n_used, dest8, fill_start, fill_n, n_rows = _dispatch_plan(idx, pos, cnt)
        xs = _dispatch(h1p, dest8, fill_start, fill_n, n_rows)
        ys = _experts(xs, plan, n_used, w_exp_gate[l], w_exp_up[l], w_exp_down[l])
        hf = _final(dest8, w8[:, :TOP_K], ys, h1f, h1p, w_sh_gate[l].astype(BF16), w_sh_up[l].astype(BF16),
                    w_sh_down[l].astype(BF16), ln2_g[l], ln2_b[l], alpha=alpha)
        if l + 1 < depth:
            hb = hf.astype(BF16)
    return hf.reshape(B, S, D)
```

```python
import functools
import math

import jax
import jax.numpy as jnp
from jax import lax
from jax.experimental import pallas as pl
from jax.experimental.pallas import tpu as pltpu

F32 = jnp.float32
BF16 = jnp.bfloat16
I32 = jnp.int32
U32 = jnp.uint32

CHUNK = 64
ROPE_THETA = 500000.0
ROT_FRACTION = 4
LRU_C = 8.0
CONV_WIDTH = 4
N_GROUPS = 8
TOPK_GROUPS = 4
TOP_K = 8
ROUTED_SCALE = 2.5
LN_EPS = 1e-5
LOG2E = 1.4426950408889634

LANES = 128
SUBLANES = 8
VMEM_LIMIT_BYTES = 56 * 1024 * 1024
EXPERT_ROWS = 256
DMA_UNROLL = 8


def _tile(n, pref):
    t = min(n, pref)
    while n % t:
        t //= 2
    return t


def _params(*sem):
    return pltpu.CompilerParams(dimension_semantics=sem, vmem_limit_bytes=VMEM_LIMIT_BYTES)


def _ln_rows(x, g, b):
    mu = jnp.mean(x, -1, keepdims=True)
    xc = x - mu
    var = jnp.mean(xc * xc, -1, keepdims=True)
    return xc * lax.rsqrt(var + LN_EPS) * g + b


def _silu(x):
    return x * jax.nn.sigmoid(x)


def _pack_halves(y):
    half = y.shape[1] // 2
    bits = lax.bitcast_convert_type(y.astype(BF16).astype(F32), U32)
    return (bits[:, :half] >> 16) | (bits[:, half:] & jnp.uint32(0xFFFF0000))


def _unpack_halves(p):
    lo = lax.bitcast_convert_type(p << 16, F32)
    hi = lax.bitcast_convert_type(p & jnp.uint32(0xFFFF0000), F32)
    return lo, hi


def _ln_in_kernel(x_ref, g_ref, b_ref, yf_ref, yb_ref):
    y = _ln_rows(x_ref[...], g_ref[...], b_ref[...])
    yf_ref[...] = y
    yb_ref[...] = y.astype(BF16)


def _ln_in(x2, g, b):
    T, D = x2.shape
    tm = _tile(T, 256)
    row = pl.BlockSpec((tm, D), lambda i: (i, 0))
    vec = pl.BlockSpec((1, D), lambda i: (0, 0))
    return pl.pallas_call(
        _ln_in_kernel, grid=(T // tm,),
        in_specs=[row, vec, vec], out_specs=[row, row],
        out_shape=[jax.ShapeDtypeStruct((T, D), F32), jax.ShapeDtypeStruct((T, D), BF16)],
        compiler_params=_params("parallel"), name="ln_in",
    )(x2, g.reshape(1, D), b.reshape(1, D))


def _in_proj_kernel(a_ref, w_ref, cos_ref, sin_ref, o_ref, *, n_q, n_qk, q_scale, half, n_chunks):
    j = pl.program_id(1)
    tm, tn = o_ref.shape
    cn = tn // n_chunks
    scale = jnp.where(j < n_q, q_scale, 1.0)
    cosf = jnp.where(j < n_qk, cos_ref[...], 1.0) * scale
    sinf = jnp.where(j < n_qk, sin_ref[...], 0.0) * scale
    low = lax.broadcasted_iota(I32, (tm, LANES), 1) < half
    a = a_ref[...]
    for c in range(n_chunks):
        acc = jnp.dot(a, w_ref[:, c * cn:(c + 1) * cn], preferred_element_type=F32)
        for g in range(cn // LANES):
            x = acc[:, g * LANES:(g + 1) * LANES]
            partner = jnp.where(low, pltpu.roll(x, LANES - half, 1), pltpu.roll(x, half, 1))
            col = c * cn + g * LANES
            o_ref[:, col:col + LANES] = (x * cosf + partner * sinf).astype(BF16)


def _in_proj(h0b, w, cosf, sinf, *, att_qk, q_scale, half):
    T, K = h0b.shape
    N = w.shape[1]
    tm = _tile(T, 1024)
    tn = _tile(att_qk, 1024)
    assert N % tn == 0
    kern = functools.partial(_in_proj_kernel, n_q=att_qk // tn, n_qk=2 * att_qk // tn,
                             q_scale=q_scale, half=half, n_chunks=max(1, tn // 256))
    return pl.pallas_call(
        kern, grid=(T // tm, N // tn),
        in_specs=[pl.BlockSpec((tm, K), lambda i, j: (i, 0)),
                  pl.BlockSpec((K, tn), lambda i, j: (0, j)),
                  pl.BlockSpec((tm, LANES), lambda i, j: (i, 0)),
                  pl.BlockSpec((tm, LANES), lambda i, j: (i, 0))],
        out_specs=pl.BlockSpec((tm, tn), lambda i, j: (i, j)),
        out_shape=jax.ShapeDtypeStruct((T, N), BF16),
        compiler_params=_params("parallel", "arbitrary"), name="in_proj",
    )(h0b, w, cosf, sinf)


def _loop_by_pairs(n, body, carry):
    def pair(c2, carry):
        return body(2 * c2 + 1, body(2 * c2, carry))
    carry = lax.fori_loop(0, n // 2, pair, carry)
    return lax.cond(n % 2 == 1, lambda cr: body(n - 1, cr), lambda cr: cr, carry)


def _attn_kernel(lam_ref, q_ref, k_ref, v_ref, g_ref, o_ref, qt_s, vt_s, st_s, acc_s, *, dh, tk, out_scale):
    i = pl.program_id(2)
    tq = q_ref.shape[0]
    S, wv = v_ref.shape

    @pl.when(i == 0)
    def _():
        for c in range(S // tk):
            vt_s[c, :wv] = v_ref[c * tk:(c + 1) * tk, :].astype(F32).T.astype(BF16)
            vt_s[c, wv:] = jnp.ones((vt_s.shape[1] - wv, tk), BF16)

    qt_s[...] = q_ref[...].astype(F32).T.astype(BF16)

    def scores(c, m):
        r0 = pl.multiple_of(c * tk, tk)
        return jnp.dot(k_ref[pl.ds(r0, tk), m * dh:(m + 1) * dh], qt_s[m * dh:(m + 1) * dh, :],
                       preferred_element_type=F32)

    def phase1(c, mx):
        out = []
        for m in range(2):
            st = scores(c, m)
            st_s[c, m] = st
            out.append(jnp.maximum(mx[m], jnp.max(st, 0, keepdims=True)))
        return tuple(out)

    mx = _loop_by_pairs(i, phase1, tuple(jnp.full((1, tq), -jnp.inf, F32) for _ in range(2)))
    visible = ((lax.broadcasted_iota(I32, (tk, tq), 0) // CHUNK)
               <= (lax.broadcasted_iota(I32, (tk, tq), 1) // CHUNK))
    mxs = []
    for m in range(2):
        st = jnp.where(visible, scores(i, m), -jnp.inf)
        st_s[i, m] = st
        mxs.append(jnp.maximum(mx[m], jnp.max(st, 0, keepdims=True)))

    acc_s[...] = jnp.zeros_like(acc_s)

    def phase2(c, carry):
        vt = vt_s[c]
        for m in range(2):
            p = jnp.exp2(st_s[c, m] - mxs[m])
            acc_s[m] += jnp.dot(vt, p.astype(BF16), preferred_element_type=F32)
        return carry

    _loop_by_pairs(i + 1, phase2, 0)
    o0, o1 = (acc_s[m, :wv] * (1.0 / acc_s[m, wv:wv + 1]) for m in range(2))
    ot = o0 - lam_ref[0] * o1
    ot = ot * lax.rsqrt(jnp.mean(ot * ot, 0, keepdims=True) + LN_EPS)
    o_ref[...] = (ot.T * (g_ref[...] * out_scale)).astype(BF16)


def _attention(proj, lam, subln_g, *, B, S, heads, dh, att_qk, lambda_init):
    T = B * S
    wv = 2 * dh
    tq = _tile(S, 1024)
    tk = tq
    nq = S // tq
    kern = functools.partial(_attn_kernel, dh=dh, tk=tk, out_scale=1.0 - lambda_init)
    ones_rows = 2 * SUBLANES
    scratch = [pltpu.VMEM((wv, tq), BF16), pltpu.VMEM((S // tk, wv + ones_rows, tk), BF16),
               pltpu.VMEM((S // tk, 2, tk, tq), F32), pltpu.VMEM((2, wv + ones_rows, tq), F32)]
    return pl.pallas_call(
        kern, grid=(B, heads, nq),
        in_specs=[pl.BlockSpec(memory_space=pltpu.SMEM),
                  pl.BlockSpec((tq, wv), lambda b, h, i: (b * nq + i, h)),
                  pl.BlockSpec((S, wv), lambda b, h, i: (b, att_qk // wv + h)),
                  pl.BlockSpec((S, wv), lambda b, h, i: (b, 2 * att_qk // wv + h)),
                  pl.BlockSpec((1, wv), lambda b, h, i: (0, 0))],
        out_specs=pl.BlockSpec((tq, wv), lambda b, h, i: (b * nq + i, h)),
        out_shape=jax.ShapeDtypeStruct((T, heads * wv), BF16),
        scratch_shapes=scratch,
        compiler_params=_params("parallel", "parallel", "arbitrary"), name="attention",
    )(lam, proj, proj, proj, subln_g.reshape(1, wv))


def _gelu_tanh(x):
    return 0.5 * x * (1.0 + jnp.tanh(0.7978845608028654 * (x + 0.044715 * (x * x * x))))


def _rglru_kernel(xr_ref, yr_ref, cw_ref, cb_ref, wa_ref, ba_ref, wx_ref, bx_ref, lam_ref, o_ref,
                  a_s, u_s, xprev_s, h_s, *, n_heads, blk, scan_cols):
    @pl.when(pl.program_id(1) == 0)
    def _():
        xprev_s[...] = jnp.zeros_like(xprev_s)
        h_s[...] = jnp.zeros_like(h_s)

    tt, D = xr_ref.shape
    z = -lam_ref[...]
    sp = jnp.maximum(z, 0.0) + jnp.log1p(jnp.exp(-jnp.abs(z)))
    for hh in range(n_heads):
        sl = slice(hh * blk, (hh + 1) * blk)
        xe = jnp.concatenate([xprev_s[:, sl], xr_ref[:, sl].astype(F32)], axis=0)
        xc = cb_ref[:, sl]
        for j in range(CONV_WIDTH):
            o = SUBLANES - (CONV_WIDTH - 1) + j
            xc = xc + cw_ref[j:j + 1, sl] * xe[o:o + tt]
        xb = xc.astype(BF16)
        r = jax.nn.sigmoid(jnp.dot(xb, wa_ref[hh], preferred_element_type=F32) + ba_ref[:, sl])
        ig = jax.nn.sigmoid(jnp.dot(xb, wx_ref[hh], preferred_element_type=F32) + bx_ref[:, sl])
        log_a = (-LRU_C * r) * sp[:, sl]
        a = jnp.exp(log_a)
        a_s[:, sl] = a
        u_s[:, sl] = jnp.sqrt(-jnp.tanh(log_a) * (a * a + 1.0)) * ig * xc
    xprev_s[...] = xr_ref[tt - SUBLANES:tt, :].astype(F32)

    row = lax.broadcasted_iota(I32, (SUBLANES, scan_cols), 0)

    def scan_body(g, h):
        r0 = pl.multiple_of(g * SUBLANES, SUBLANES)
        outs = []
        for c in range(D // scan_cols):
            cs = slice(c * scan_cols, (c + 1) * scan_cols)
            a = a_s[pl.ds(r0, SUBLANES), cs]
            u = u_s[pl.ds(r0, SUBLANES), cs]
            for s in (1, 2, 4):
                valid = row >= s
                u = jnp.where(valid, a * pltpu.roll(u, s, 0) + u, u)
                a = jnp.where(valid, a * pltpu.roll(a, s, 0), a)
            hg = u + a * h[:, cs]
            u_s[pl.ds(r0, SUBLANES), cs] = hg
            outs.append(hg[SUBLANES - 1:SUBLANES, :])
        return jnp.concatenate(outs, axis=1)

    h_s[...] = lax.fori_loop(0, tt // SUBLANES, scan_body, h_s[...])
    o_ref[...] = (u_s[...] * _gelu_tanh(yr_ref[...].astype(F32))).astype(BF16)


def _rglru(proj, conv_w, conv_b, w_ra, b_ra, w_rx, b_rx, lru_lambda, *, B, S, off_xr):
    T = B * S
    n_heads, blk, _ = w_ra.shape
    D = n_heads * blk
    tt = _tile(S, 256)
    nt = S // tt
    assert off_xr % D == 0
    cx = off_xr // D
    vec = pl.BlockSpec((1, D), lambda b, t: (0, 0))
    wspec = pl.BlockSpec((n_heads, blk, blk), lambda b, t: (0, 0, 0))
    kern = functools.partial(_rglru_kernel, n_heads=n_heads, blk=blk, scan_cols=_tile(D, 512))
    return pl.pallas_call(
        kern, grid=(B, nt),
        in_specs=[pl.BlockSpec((tt, D), lambda b, t: (b * nt + t, cx)),
                  pl.BlockSpec((tt, D), lambda b, t: (b * nt + t, cx + 1)),
                  pl.BlockSpec((CONV_WIDTH, D), lambda b, t: (0, 0)), vec,
                  wspec, vec, wspec, vec, vec],
        out_specs=pl.BlockSpec((tt, D), lambda b, t: (b * nt + t, 0)),
        out_shape=jax.ShapeDtypeStruct((T, D), BF16),
        scratch_shapes=[pltpu.VMEM((tt, D), F32), pltpu.VMEM((tt, D), F32),
                        pltpu.VMEM((SUBLANES, D), F32), pltpu.VMEM((1, D), F32)],
        compiler_params=_params("parallel", "arbitrary"), name="rglru",
    )(proj, proj, conv_w, conv_b.reshape(1, D), w_ra.astype(BF16), b_ra.reshape(1, D),
      w_rx.astype(BF16), b_rx.reshape(1, D), lru_lambda.reshape(1, D))


def _mix_kernel(at_ref, rn_ref, wa_ref, wr_ref, g0_ref, g1_ref, b0_ref, b1_ref, o_ref):
    pa = jnp.dot(at_ref[...], wa_ref[...], preferred_element_type=F32)
    pr = jnp.dot(rn_ref[...], wr_ref[...], preferred_element_type=F32)
    s0 = jax.nn.sigmoid(g0_ref[...].astype(F32) + b0_ref[...])
    s1 = jax.nn.sigmoid(g1_ref[...].astype(F32) + b1_ref[...])
    o_ref[...] = (s0 * pa + s1 * pr).astype(BF16)


def _mix(attn, rnn, wa, wr, proj, b_gate, *, off_g):
    T, Ka = attn.shape
    Kr = rnn.shape[1]
    D = wa.shape[1]
    tm = _tile(T, 512)
    tn = _tile(D, 512)
    assert off_g % tn == 0
    cg = off_g // tn
    nn = D // tn
    return pl.pallas_call(
        _mix_kernel, grid=(T // tm, nn),
        in_specs=[pl.BlockSpec((tm, Ka), lambda i, j: (i, 0)),
                  pl.BlockSpec((tm, Kr), lambda i, j: (i, 0)),
                  pl.BlockSpec((Ka, tn), lambda i, j: (0, j)),
                  pl.BlockSpec((Kr, tn), lambda i, j: (0, j)),
                  pl.BlockSpec((tm, tn), lambda i, j: (i, cg + j)),
                  pl.BlockSpec((tm, tn), lambda i, j: (i, cg + nn + j)),
                  pl.BlockSpec((1, tn), lambda i, j: (0, j)),
                  pl.BlockSpec((1, tn), lambda i, j: (0, nn + j))],
        out_specs=pl.BlockSpec((tm, tn), lambda i, j: (i, j)),
        out_shape=jax.ShapeDtypeStruct((T, D), BF16),
        compiler_params=_params("parallel", "arbitrary"), name="mix",
    )(attn, rnn, wa, wr, proj, proj, b_gate.reshape(1, 2 * D), b_gate.reshape(1, 2 * D))


def _out_ln_kernel(a_ref, w_ref, h0_ref, g_ref, b_ref, hf_ref, hp_ref, acc_s, *, alpha):
    j = pl.program_id(1)
    n_n, _, tn = acc_s.shape
    acc_s[j] = alpha * h0_ref[...] + jnp.dot(a_ref[...], w_ref[...], preferred_element_type=F32)

    @pl.when(j == n_n - 1)
    def _():
        D = n_n * tn
        mu = sum(jnp.sum(acc_s[c], -1, keepdims=True) for c in range(n_n)) / D
        var = sum(jnp.sum(jnp.square(acc_s[c] - mu), -1, keepdims=True) for c in range(n_n)) / D
        rstd = lax.rsqrt(var + LN_EPS)
        hn = n_n // 2
        for c in range(hn):
            ys = []
            for cc in (c, c + hn):
                cs = slice(cc * tn, (cc + 1) * tn)
                y = (acc_s[cc] - mu) * rstd * g_ref[:, cs] + b_ref[:, cs]
                hf_ref[:, cs] = y
                ys.append(y)
            hp_ref[:, c * tn:(c + 1) * tn] = _pack_halves(jnp.concatenate(ys, axis=1))


def _out_ln(mixed, w, h0f, g, b, *, alpha):
    T, K = mixed.shape
    D = w.shape[1]
    tm = _tile(T, 512)
    tn = _tile(D // 2, 512)
    vec = pl.BlockSpec((1, D), lambda i, j: (0, 0))
    return pl.pallas_call(
        functools.partial(_out_ln_kernel, alpha=alpha), grid=(T // tm, D // tn),
        in_specs=[pl.BlockSpec((tm, K), lambda i, j: (i, 0), pipeline_mode=pl.Buffered(1)),
                  pl.BlockSpec((K, tn), lambda i, j: (0, j)),
                  pl.BlockSpec((tm, tn), lambda i, j: (i, j)), vec, vec],
        out_specs=[pl.BlockSpec((tm, D), lambda i, j: (i, 0)),
                   pl.BlockSpec((tm, D // 2), lambda i, j: (i, 0))],
        out_shape=[jax.ShapeDtypeStruct((T, D), F32), jax.ShapeDtypeStruct((T, D // 2), U32)],
        scratch_shapes=[pltpu.VMEM((D // tn, tm, tn), F32)],
        compiler_params=_params("parallel", "arbitrary"), name="out_ln",
    )(mixed, w, h0f, g.reshape(1, D), b.reshape(1, D))


def _seg_allreduce(x, lane, width, op):
    n = x.shape[1]
    s = 1
    while s < width:
        partner = jnp.where((lane & s) == 0, pltpu.roll(x, n - s, 1), pltpu.roll(x, s, 1))
        x = op(x, partner)
        s *= 2
    return x


def _router_kernel(h_ref, w_ref, bias_ref, w8_ref, idx_ref, pos_ref, cnt_ref, tri_s, run_s, w2_s):
    E = w_ref.shape[1]

    @pl.when(pl.program_id(0) == 0)
    def _():
        tm = tri_s.shape[0]
        tri_s[...] = jnp.where(lax.broadcasted_iota(I32, (tm, tm), 1) < lax.broadcasted_iota(I32, (tm, tm), 0),
                               1.0, 0.0).astype(BF16)
        run_s[...] = jnp.zeros_like(run_s)
        w = w_ref[...]
        w_hi = w.astype(BF16)
        w2_s[:, :E] = w_hi
        w2_s[:, E:] = (w - w_hi.astype(F32)).astype(BF16)

    h = h_ref[...]
    h_hi = h.astype(BF16)
    h_lo = (h - h_hi.astype(F32)).astype(BF16)
    hh = jnp.dot(h_hi, w2_s[...], preferred_element_type=F32)
    logits = hh[:, :E] + (hh[:, E:] + jnp.dot(h_lo, w2_s[:, :E], preferred_element_type=F32))
    scores = jax.nn.sigmoid(logits)
    biased = scores + bias_ref[...]
    gw = E // N_GROUPS
    lane = lax.broadcasted_iota(I32, biased.shape, 1)
    neg = -jnp.inf

    m1 = _seg_allreduce(biased, lane, gw, jnp.maximum)
    is_m1 = biased == m1
    cnt = _seg_allreduce(is_m1.astype(F32), lane, gw, jnp.add)
    m2 = jnp.where(cnt >= 2.0, m1, _seg_allreduce(jnp.where(is_m1, neg, biased), lane, gw, jnp.maximum))
    gs = m1 + m2

    gi = lane // gw
    rank = jnp.zeros(biased.shape, I32)
    for k in range(1, N_GROUPS):
        other = pltpu.roll(gs, k * gw, 1)
        ogi = (gi - k) % N_GROUPS
        beats = (other > gs) | ((other == gs) & (ogi < gi))
        rank = rank + beats.astype(I32)
    cur = jnp.where(rank < TOPK_GROUPS, biased, neg)

    sel = jnp.zeros(biased.shape, jnp.bool_)
    firsts = []
    for k in range(TOP_K):
        mx = jnp.max(cur, -1, keepdims=True)
        first = jnp.min(jnp.where(cur == mx, lane, E), -1, keepdims=True)
        pick = lane == first
        sel = sel | pick
        firsts.append(first)
        cur = jnp.where(pick, neg, cur)

    sel_f = jnp.where(sel, 1.0, 0.0)
    rank_in_e = run_s[...] + jnp.dot(tri_s[...], sel_f.astype(BF16), preferred_element_type=F32)
    run_s[...] = run_s[...] + jnp.sum(sel_f, 0, keepdims=True)
    cnt_ref[...] = run_s[...]

    wsum = jnp.sum(jnp.where(sel, scores, 0.0), -1, keepdims=True)
    w8 = jnp.zeros(biased.shape, F32)
    idx = jnp.zeros(biased.shape, I32)
    pos = jnp.zeros(biased.shape, F32)
    for k in range(TOP_K):
        pick = lane == firsts[k]
        wk = jnp.sum(jnp.where(pick, scores, 0.0), -1, keepdims=True)
        pk = jnp.sum(jnp.where(pick, rank_in_e, 0.0), -1, keepdims=True)
        w8 = jnp.where(lane == k, wk / wsum * ROUTED_SCALE, w8)
        idx = jnp.where(lane == k, firsts[k], idx)
        pos = jnp.where(lane == k, pk, pos)
    w8_ref[...] = w8
    idx_ref[...] = idx
    pos_ref[...] = pos.astype(I32)


def _router(h1f, w_router, router_bias):
    T, D = h1f.shape
    E = w_router.shape[1]
    assert E == LANES and E % N_GROUPS == 0
    tm = _tile(T, 512)
    row = pl.BlockSpec((tm, E), lambda i: (i, 0))
    return pl.pallas_call(
        _router_kernel, grid=(T // tm,),
        in_specs=[pl.BlockSpec((tm, D), lambda i: (i, 0)),
                  pl.BlockSpec((D, E), lambda i: (0, 0)),
                  pl.BlockSpec((1, E), lambda i: (0, 0))],
        out_specs=[row, row, row, pl.BlockSpec((1, E), lambda i: (0, 0))],
        out_shape=[jax.ShapeDtypeStruct((T, E), F32), jax.ShapeDtypeStruct((T, E), I32),
                   jax.ShapeDtypeStruct((T, E), I32), jax.ShapeDtypeStruct((1, E), F32)],
        scratch_shapes=[pltpu.VMEM((tm, tm), BF16), pltpu.VMEM((1, E), F32), pltpu.VMEM((D, 2 * E), BF16)],
        compiler_params=_params("arbitrary"), name="router",
    )(h1f, w_router, router_bias.reshape(1, E))


def _dispatch_kernel(fill_start_ref, fill_n_ref, d8_ref, hp_ref, xs_hbm, zero_s, sem, zsem):
    tm = hp_ref.shape[0]
    E = fill_n_ref.shape[0]

    def row_copy(r, k):
        return pltpu.make_async_copy(hp_ref.at[pl.ds(r, 1), :],
                                     xs_hbm.at[pl.ds(d8_ref[0, 0, r * TOP_K + k], 1), :], sem.at[0])

    def issue(r, c):
        for k in range(TOP_K):
            row_copy(r, k).start()
        return c
    lax.fori_loop(0, tm, issue, 0)

    @pl.when(pl.program_id(0) == 0)
    def _():
        zero_s[...] = jnp.zeros_like(zero_s)
        sizes = [1 << b for b in reversed(range(3, EXPERT_ROWS.bit_length() - 1))]

        def per_expert(e, c):
            s0 = fill_start_ref[e]
            n = fill_n_ref[e]
            head = jnp.minimum((-s0) % SUBLANES, n)
            body = (n - head) // SUBLANES * SUBLANES
            tail = n - head - body
            for start in (True, False):
                def fill(row, size, pred):
                    cp = pltpu.make_async_copy(zero_s.at[pl.ds(0, size), :], xs_hbm.at[pl.ds(row, size), :],
                                               zsem.at[0])

                    @pl.when(pred)
                    def _():
                        cp.start() if start else cp.wait()

                for r in range(SUBLANES - 1):
                    fill(s0 + r, 1, r < head)
                off = s0 + head
                for b in sizes:
                    fill(pl.multiple_of(off, SUBLANES), b, (body & b) != 0)
                    off = off + (body & b)
                for r in range(SUBLANES - 1):
                    fill(off + r, 1, r < tail)
            return c
        lax.fori_loop(0, E, per_expert, 0)

    def wait(r, c):
        for k in range(TOP_K):
            row_copy(r, k).wait()
        return c
    lax.fori_loop(0, tm, wait, 0)


def _dispatch(h1p, dest8, fill_start, fill_n, n_rows):
    T, W = h1p.shape
    tm = _tile(T, 1024)
    nt = T // tm
    d8 = dest8.reshape(nt, 1, tm * TOP_K)
    grid_spec = pltpu.PrefetchScalarGridSpec(
        num_scalar_prefetch=2, grid=(nt,),
        in_specs=[pl.BlockSpec((1, 1, tm * TOP_K), lambda i, fs, fn: (i, 0, 0), memory_space=pltpu.SMEM),
                  pl.BlockSpec((tm, W), lambda i, fs, fn: (i, 0))],
        out_specs=pl.BlockSpec(memory_space=pl.ANY),
        scratch_shapes=[pltpu.VMEM((EXPERT_ROWS // 2, W), U32), pltpu.SemaphoreType.DMA((1,)),
                        pltpu.SemaphoreType.DMA((1,))])
    return pl.pallas_call(
        _dispatch_kernel, grid_spec=grid_spec,
        out_shape=jax.ShapeDtypeStruct((n_rows, W), U32),
        compiler_params=_params("arbitrary"), name="dispatch",
    )(fill_start, fill_n, d8, h1p)


def _expert_switch(plan_ref, i, w_hbm, wbuf, sem, recast):
    def copies(e, slot):
        return [pltpu.make_async_copy(w.at[e], wbuf.at[slot, t], sem.at[slot, t]) for t, w in enumerate(w_hbm)]

    @pl.when(plan_ref[1, i] == 1)
    def _():
        e, slot, nxt = plan_ref[0, i], plan_ref[2, i], plan_ref[3, i]

        @pl.when(i == 0)
        def _():
            for c in copies(e, slot):
                c.start()

        for c in copies(e, slot):
            c.wait()
        recast(slot)

        @pl.when(nxt >= 0)
        def _():
            for c in copies(nxt, 1 - slot):
                c.start()


def _expert_step(plan_ref, n_used_ref, n_rows, switch, compute, clear):
    R = EXPERT_ROWS
    assert n_rows == 2 * R
    i0 = pl.program_id(0) * 2
    merged = jnp.logical_and(i0 + 1 < n_used_ref[0], plan_ref[1, i0 + 1] == 0)

    @pl.when(merged)
    def _():
        switch(i0)
        compute(slice(0, 2 * R))

    @pl.when(jnp.logical_not(merged))
    def _():
        for h in range(2):
            rows = slice(h * R, (h + 1) * R)

            @pl.when(i0 + h < n_used_ref[0])
            def _():
                switch(i0 + h)
                compute(rows)

            @pl.when(i0 + h >= n_used_ref[0])
            def _():
                clear(rows)


def _experts_up_kernel(plan_ref, n_used_ref, xs_ref, wg_hbm, wu_hbm, hb_ref, wbuf, wgu_s, sem):
    F = wg_hbm.shape[2]

    def recast(slot):
        wgu_s[:, :F] = wbuf[slot, 0].astype(BF16)
        wgu_s[:, F:] = wbuf[slot, 1].astype(BF16)

    def compute(rows):
        lo, hi = _unpack_halves(xs_ref[rows, :])
        hw = lo.shape[1]
        gu = (jnp.dot(lo.astype(BF16), wgu_s[:hw, :], preferred_element_type=F32)
              + jnp.dot(hi.astype(BF16), wgu_s[hw:, :], preferred_element_type=F32))
        hb_ref[rows, :] = (_silu(gu[:, :F]) * gu[:, F:]).astype(BF16)

    def clear(rows):
        hb_ref[rows, :] = jnp.zeros((rows.stop - rows.start, F), BF16)

    _expert_step(plan_ref, n_used_ref, xs_ref.shape[0],
                 lambda i: _expert_switch(plan_ref, i, (wg_hbm, wu_hbm), wbuf, sem, recast), compute, clear)


def _experts_down_kernel(plan_ref, n_used_ref, hb_ref, wd_hbm, ys_ref, wbuf, wd_s, sem):
    def recast(slot):
        wd_s[...] = wbuf[slot, 0].astype(BF16)

    def compute(rows):
        ys_ref[rows, :] = _pack_halves(jnp.dot(hb_ref[rows, :], wd_s[...], preferred_element_type=F32))

    def clear(rows):
        ys_ref[rows, :] = jnp.zeros((rows.stop - rows.start, ys_ref.shape[1]), U32)

    _expert_step(plan_ref, n_used_ref, hb_ref.shape[0],
                 lambda i: _expert_switch(plan_ref, i, (wd_hbm,), wbuf, sem, recast), compute, clear)


def _experts(xs, plan, n_used, wg, wu, wd):
    n_rows, W = xs.shape
    E, D, F = wg.shape
    R = EXPERT_ROWS
    per_step = 2
    n_steps = n_rows // (per_step * R)
    assert n_steps * per_step * R == n_rows
    RS = per_step * R
    xrow = lambda i, pn, nu: (jnp.minimum(i, jnp.maximum(nu[0] - 1, 0) // per_step), 0)
    hbm = pl.BlockSpec(memory_space=pl.ANY)
    hb = pl.pallas_call(
        _experts_up_kernel,
        grid_spec=pltpu.PrefetchScalarGridSpec(
            num_scalar_prefetch=2, grid=(n_steps,),
            in_specs=[pl.BlockSpec((RS, W), xrow), hbm, hbm],
            out_specs=pl.BlockSpec((RS, F), lambda i, pn, nu: (i, 0)),
            scratch_shapes=[pltpu.VMEM((2, 2, D, F), F32), pltpu.VMEM((D, 2 * F), BF16),
                            pltpu.SemaphoreType.DMA((2, 2))]),
        out_shape=jax.ShapeDtypeStruct((n_rows, F), BF16),
        compiler_params=_params("arbitrary"), name="experts_up",
    )(plan, n_used, xs, wg, wu)
    return pl.pallas_call(
        _experts_down_kernel,
        grid_spec=pltpu.PrefetchScalarGridSpec(
            num_scalar_prefetch=2, grid=(n_steps,),
            in_specs=[pl.BlockSpec((RS, F), lambda i, pn, nu: (i, 0)), hbm],
            out_specs=pl.BlockSpec((RS, W), lambda i, pn, nu: (i, 0)),
            scratch_shapes=[pltpu.VMEM((2, 1, F, D), F32), pltpu.VMEM((F, D), BF16),
                            pltpu.SemaphoreType.DMA((2, 1))]),
        out_shape=jax.ShapeDtypeStruct((n_rows, W), U32),
        compiler_params=_params("arbitrary"), name="experts_down",
    )(plan, n_used, hb, wd)


def _final_kernel(d8a_ref, d8b_ref, d8n_ref, ys_hbm, hf_ref, hp_ref, w8_ref, wsg_ref, wsu_ref, wsd_ref, g_ref, b_ref,
                  o_ref, buf_a, buf_b, sem, *, alpha):
    i = pl.program_id(0)
    nt = pl.num_programs(0)
    tm = hf_ref.shape[0] // 2
    n = tm * TOP_K

    def issue(d8_ref, rows, buf, s):
        for r in rows:
            pltpu.make_async_copy(ys_hbm.at[pl.ds(d8_ref[0, 0, r], 1), :],
                                  buf.at[r // SUBLANES, pl.ds(r % SUBLANES, 1), :], sem.at[s]).start()

    def wait_all(buf, s):
        def wait(g, c):
            for sub in range(SUBLANES):
                pltpu.make_async_copy(ys_hbm.at[pl.ds(0, 1), :], buf.at[g, pl.ds(sub, 1), :], sem.at[s]).wait()
            return c
        lax.fori_loop(0, n // SUBLANES, wait, 0)

    @pl.when(i == 0)
    def _():
        def first(g, c):
            for sub in range(SUBLANES):
                pltpu.make_async_copy(ys_hbm.at[pl.ds(d8a_ref[0, 0, g * SUBLANES + sub], 1), :],
                                      buf_a.at[g, pl.ds(sub, 1), :], sem.at[0]).start()
            return c
        lax.fori_loop(0, n // SUBLANES, first, 0)

    def shared(rows):
        lo, hi = _unpack_halves(hp_ref[rows, :])
        x = jnp.concatenate([lo.astype(BF16), hi.astype(BF16)], axis=1)
        hs = (_silu(jnp.dot(x, wsg_ref[...], preferred_element_type=F32))
              * jnp.dot(x, wsu_ref[...], preferred_element_type=F32)).astype(BF16)
        return alpha * hf_ref[rows, :] + jnp.dot(hs, wsd_ref[...], preferred_element_type=F32)

    def combine(rows, acc, buf):
        r_lo = r_hi = None
        gk = tm // SUBLANES
        for k in range(TOP_K):
            lo, hi = _unpack_halves(buf[k * gk:(k + 1) * gk].reshape(tm, buf.shape[-1]))
            wk = w8_ref[rows, k:k + 1]
            r_lo = lo * wk if r_lo is None else r_lo + lo * wk
            r_hi = hi * wk if r_hi is None else r_hi + hi * wk
        o_ref[rows, :] = _ln_rows(acc + jnp.concatenate([r_lo, r_hi], axis=1), g_ref[...], b_ref[...])

    rows_a, rows_b = slice(0, tm), slice(tm, 2 * tm)
    issue(d8b_ref, range(0, n // 2), buf_b, 1)
    acc = shared(rows_a)
    wait_all(buf_a, 0)
    issue(d8b_ref, range(n // 2, n), buf_b, 1)
    combine(rows_a, acc, buf_a)
    issue(d8n_ref, range(0, n // 2), buf_a, 0)
    acc = shared(rows_b)
    wait_all(buf_b, 1)
    issue(d8n_ref, range(n // 2, n), buf_a, 0)
    combine(rows_b, acc, buf_b)

    @pl.when(i == nt - 1)
    def _():
        wait_all(buf_a, 0)


def _final(dest8, w8, ys, h1f, h1p, wsg, wsu, wsd, g, b, *, alpha):
    T, D = h1f.shape
    W = ys.shape[1]
    F = wsg.shape[1]
    tm = _tile(T // 2, 128)
    nt = T // (2 * tm)
    d8 = dest8.reshape(2 * nt, tm, TOP_K).transpose(0, 2, 1).reshape(2 * nt, 1, TOP_K * tm)
    row = lambda w: pl.BlockSpec((2 * tm, w), lambda i: (i, 0))
    vec = pl.BlockSpec((1, D), lambda i: (0, 0))
    idx_spec = lambda f: pl.BlockSpec((1, 1, TOP_K * tm), lambda i: (f(i), 0, 0), memory_space=pltpu.SMEM)
    wspec = lambda shape: pl.BlockSpec(shape, lambda i: (0, 0), pipeline_mode=pl.Buffered(1))
    gbuf = pltpu.VMEM((TOP_K * tm // SUBLANES, SUBLANES, W), U32)
    return pl.pallas_call(
        functools.partial(_final_kernel, alpha=alpha), grid=(nt,),
        in_specs=[idx_spec(lambda i: 2 * i), idx_spec(lambda i: 2 * i + 1),
                  idx_spec(lambda i: jnp.minimum(2 * i + 2, 2 * nt - 2)),
                  pl.BlockSpec(memory_space=pl.ANY), row(D), row(W), row(TOP_K),
                  wspec((D, F)), wspec((D, F)), wspec((F, D)), vec, vec],
        out_specs=row(D),
        out_shape=jax.ShapeDtypeStruct((T, D), F32),
        scratch_shapes=[gbuf, gbuf, pltpu.SemaphoreType.DMA((2,))],
        compiler_params=_params("arbitrary"), name="final",
    )(d8, d8, d8, ys, h1f, h1p, w8, wsg, wsu, wsd, g.reshape(1, D), b.reshape(1, D))


def _rope_tables(positions, dh):
    rot = dh // ROT_FRACTION
    half = rot // 2
    inv_freq = jnp.power(ROPE_THETA, -jnp.arange(0, rot, 2, dtype=F32) / rot)
    ang = positions.astype(F32).reshape(-1, 1) * inv_freq
    cos, sin = jnp.cos(ang), jnp.sin(ang)
    T = ang.shape[0]
    cosf = jnp.concatenate([cos, cos, jnp.ones((T, dh - rot), F32)], axis=1)
    sinf = jnp.concatenate([-sin, sin, jnp.zeros((T, dh - rot), F32)], axis=1)
    return cosf, sinf, half


def _plan_kernel(idx_ref, pos_ref, pstart_ref, dest_ref):
    starts = jnp.broadcast_to(pstart_ref[...], idx_ref.shape)
    dest_ref[...] = jnp.take_along_axis(starts, idx_ref[...], axis=1) + pos_ref[...]


def _dispatch_plan(idx, pos, cnt):
    T, E = idx.shape
    R = EXPERT_ROWS
    n_blocks = -(-(T * TOP_K) // R) + E
    counts = cnt.reshape(E).astype(I32)
    padded = (counts + R - 1) // R * R
    pend = jnp.cumsum(padded)
    pstart = pend - padded
    tm = _tile(T, 1024)
    row = pl.BlockSpec((tm, E), lambda i: (i, 0))
    dest = pl.pallas_call(
        _plan_kernel, grid=(T // tm,),
        in_specs=[row, row, pl.BlockSpec((1, E), lambda i: (0, 0))], out_specs=row,
        out_shape=jax.ShapeDtypeStruct((T, E), I32),
        compiler_params=_params("parallel"), name="plan",
    )(idx, pos, pstart.reshape(1, E).astype(I32))
    dest8 = dest[:, :TOP_K]
    blk = jnp.arange(n_blocks, dtype=I32)
    ex = jnp.arange(E, dtype=I32)
    blk_e = jnp.minimum(jnp.sum(pend[None, :] <= (blk * R)[:, None], axis=1), E - 1).astype(I32)
    n_used = (pend[-1:] // R).astype(I32)
    first = (blk < n_used[0]) & ((blk == 0) | (blk_e != jnp.roll(blk_e, 1)))
    slot = (jnp.cumsum(first.astype(I32)) - 1) % 2
    cand = jnp.where(counts > 0, ex, E)
    later = jnp.min(jnp.where(ex[None, :] > ex[:, None], cand[None, :], E), axis=1)
    nxt_e = jnp.where(later < E, later, -1)
    nxt = jnp.sum(jnp.where(blk_e[:, None] == ex[None, :], nxt_e[None, :], 0), axis=1)
    plan = jnp.stack([blk_e, first.astype(I32), slot.astype(I32), nxt.astype(I32)])
    fill_start = (pstart + counts).astype(I32)
    fill_n = (padded - counts).astype(I32)
    return plan, n_used, dest8.astype(I32), fill_start, fill_n, n_blocks * R


def kernel(x, positions, ln_in_g, ln_in_b, w_in, b_gate, lam_q1, lam_k1, lam_q2, lam_k2, subln_g, conv_w, conv_b, w_rg_a, b_rg_a, w_rg_x, b_rg_x, lru_lambda, w_proj_attn, w_proj_rnn, w_out, ln1_g, ln1_b, w_router, router_bias, w_exp_gate, w_exp_up, w_exp_down, w_sh_gate, w_sh_up, w_sh_down, ln2_g, ln2_b):
    B, S, D = x.shape
    T = B * S
    depth = w_in.shape[0]
    alpha = (2 * depth) ** 0.25
    dh = lam_q1.shape[-1]
    att_v = w_proj_attn.shape[1]
    d_rnn = conv_w.shape[-1]
    att_qk = (w_in.shape[2] - att_v - 2 * d_rnn - 2 * D) // 2
    heads = att_v // (2 * dh)
    assert dh == LANES and att_qk == att_v
    cosf, sinf, half = _rope_tables(positions, dh)

    hf, hb = _ln_in(x.reshape(T, D), ln_in_g, ln_in_b)
    for l in range(depth):
        lambda_init = 0.8 - 0.6 * math.exp(-0.3 * l)
        lam = (jnp.exp(jnp.sum(lam_q1[l] * lam_k1[l])) - jnp.exp(jnp.sum(lam_q2[l] * lam_k2[l]))
               + lambda_init).reshape(1).astype(F32)
        proj = _in_proj(hb, w_in[l].astype(BF16), cosf, sinf, att_qk=att_qk,
                        q_scale=dh ** -0.5 * LOG2E, half=half)
        attn = _attention(proj, lam, subln_g[l], B=B, S=S, heads=heads, dh=dh, att_qk=att_qk,
                          lambda_init=lambda_init)
        rnn = _rglru(proj, conv_w[l], conv_b[l], w_rg_a[l], b_rg_a[l], w_rg_x[l], b_rg_x[l], lru_lambda[l],
                     B=B, S=S, off_xr=2 * att_qk + att_v)
        mixed = _mix(attn, rnn, w_proj_attn[l].astype(BF16), w_proj_rnn[l].astype(BF16), proj, b_gate[l],
                     off_g=2 * att_qk + att_v + 2 * d_rnn)
        h1f, h1p = _out_ln(mixed, w_out[l].astype(BF16), hf, ln1_g[l], ln1_b[l], alpha=alpha)
        w8, idx, pos, cnt = _router(h1f, w_router[l], router_bias[l])
        plan, n_used, dest8, fill_start, fill_n, n_rows = _dispatch_plan(idx, pos, cnt)
        xs = _dispatch(h1p, dest8, fill_start, fill_n, n_rows)
        ys = _experts(xs, plan, n_used, w_exp_gate[l], w_exp_up[l], w_exp_down[l])
        hf = _final(dest8, w8[:, :TOP_K], ys, h1f, h1p, w_sh_gate[l].astype(BF16), w_sh_up[l].astype(BF16),
                    w_sh_down[l].astype(BF16), ln2_g[l], ln2_b[l], alpha=alpha)
        if l + 1 < depth:
            hb = hf.astype(BF16)
    return hf.reshape(B, S, D)
```

```python
import functools
import math

import jax
import jax.numpy as jnp
from jax import lax
from jax.experimental import pallas as pl
from jax.experimental.pallas import tpu as pltpu

F32 = jnp.float32
BF16 = jnp.bfloat16
I32 = jnp.int32
U32 = jnp.uint32

CHUNK = 64
ROPE_THETA = 500000.0
ROT_FRACTION = 4
LRU_C = 8.0
CONV_WIDTH = 4
N_GROUPS = 8
TOPK_GROUPS = 4
TOP_K = 8
ROUTED_SCALE = 2.5
LN_EPS = 1e-5
LOG2E = 1.4426950408889634

LANES = 128
SUBLANES = 8
VMEM_LIMIT_BYTES = 56 * 1024 * 1024
EXPERT_ROWS = 256
DMA_UNROLL = 8


def _tile(n, pref):
    t = min(n, pref)
    while n % t:
        t //= 2
    return t


def _params(*sem):
    return pltpu.CompilerParams(dimension_semantics=sem, vmem_limit_bytes=VMEM_LIMIT_BYTES)


def _ln_rows(x, g, b):
    mu = jnp.mean(x, -1, keepdims=True)
    xc = x - mu
    var = jnp.mean(xc * xc, -1, keepdims=True)
    return xc * lax.rsqrt(var + LN_EPS) * g + b


def _silu(x):
    return x * jax.nn.sigmoid(x)


def _pack_halves(y):
    half = y.shape[1] // 2
    bits = lax.bitcast_convert_type(y.astype(BF16).astype(F32), U32)
    return (bits[:, :half] >> 16) | (bits[:, half:] & jnp.uint32(0xFFFF0000))


def _unpack_halves(p):
    lo = lax.bitcast_convert_type(p << 16, F32)
    hi = lax.bitcast_convert_type(p & jnp.uint32(0xFFFF0000), F32)
    return lo, hi


def _ln_in_kernel(x_ref, g_ref, b_ref, yf_ref, yb_ref):
    y = _ln_rows(x_ref[...], g_ref[...], b_ref[...])
    yf_ref[...] = y
    yb_ref[...] = y.astype(BF16)


def _ln_in(x2, g, b):
    T, D = x2.shape
    tm = _tile(T, 256)
    row = pl.BlockSpec((tm, D), lambda i: (i, 0))
    vec = pl.BlockSpec((1, D), lambda i: (0, 0))
    return pl.pallas_call(
        _ln_in_kernel, grid=(T // tm,),
        in_specs=[row, vec, vec], out_specs=[row, row],
        out_shape=[jax.ShapeDtypeStruct((T, D), F32), jax.ShapeDtypeStruct((T, D), BF16)],
        compiler_params=_params("parallel"), name="ln_in",
    )(x2, g.reshape(1, D), b.reshape(1, D))


def _in_proj_kernel(a_ref, w_ref, cos_ref, sin_ref, o_ref, *, n_q, n_qk, q_scale, half, n_chunks):
    j = pl.program_id(1)
    tm, tn = o_ref.shape
    cn = tn // n_chunks
    scale = jnp.where(j < n_q, q_scale, 1.0)
    cosf = jnp.where(j < n_qk, cos_ref[...], 1.0) * scale
    sinf = jnp.where(j < n_qk, sin_ref[...], 0.0) * scale
    low = lax.broadcasted_iota(I32, (tm, LANES), 1) < half
    a = a_ref[...]
    for c in range(n_chunks):
        acc = jnp.dot(a, w_ref[:, c * cn:(c + 1) * cn], preferred_element_type=F32)
        for g in range(cn // LANES):
            x = acc[:, g * LANES:(g + 1) * LANES]
            partner = jnp.where(low, pltpu.roll(x, LANES - half, 1), pltpu.roll(x, half, 1))
            col = c * cn + g * LANES
            o_ref[:, col:col + LANES] = (x * cosf + partner * sinf).astype(BF16)


def _in_proj(h0b, w, cosf, sinf, *, att_qk, q_scale, half):
    T, K = h0b.shape
    N = w.shape[1]
    tm = _tile(T, 1024)
    tn = _tile(att_qk, 1024)
    assert N % tn == 0
    kern = functools.partial(_in_proj_kernel, n_q=att_qk // tn, n_qk=2 * att_qk // tn,
                             q_scale=q_scale, half=half, n_chunks=max(1, tn // 256))
    return pl.pallas_call(
        kern, grid=(T // tm, N // tn),
        in_specs=[pl.BlockSpec((tm, K), lambda i, j: (i, 0)),
                  pl.BlockSpec((K, tn), lambda i, j: (0, j)),
                  pl.BlockSpec((tm, LANES), lambda i, j: (i, 0)),
                  pl.BlockSpec((tm, LANES), lambda i, j: (i, 0))],
        out_specs=pl.BlockSpec((tm, tn), lambda i, j: (i, j)),
        out_shape=jax.ShapeDtypeStruct((T, N), BF16),
        compiler_params=_params("parallel", "arbitrary"), name="in_proj",
    )(h0b, w, cosf, sinf)


def _loop_by_pairs(n, body, carry):
    def pair(c2, carry):
        return body(2 * c2 + 1, body(2 * c2, carry))
    carry = lax.fori_loop(0, n // 2, pair, carry)
    return lax.cond(n % 2 == 1, lambda cr: body(n - 1, cr), lambda cr: cr, carry)


def _attn_kernel(lam_ref, q_ref, k_ref, v_ref, g_ref, o_ref, qt_s, vt_s, st_s, acc_s, *, dh, tk, out_scale):
    i = pl.program_id(2)
    tq = q_ref.shape[0]
    S, wv = v_ref.shape

    @pl.when(i == 0)
    def _():
        for c in range(S // tk):
            vt_s[c, :wv] = v_ref[c * tk:(c + 1) * tk, :].astype(F32).T.astype(BF16)
            vt_s[c, wv:] = jnp.ones((vt_s.shape[1] - wv, tk), BF16)

    qt_s[...] = q_ref[...].astype(F32).T.astype(BF16)

    def scores(c, m):
        r0 = pl.multiple_of(c * tk, tk)
        return jnp.dot(k_ref[pl.ds(r0, tk), m * dh:(m + 1) * dh], qt_s[m * dh:(m + 1) * dh, :],
                       preferred_element_type=F32)

    def phase1(c, mx):
        out = []
        for m in range(2):
            st = scores(c, m)
            st_s[c, m] = st
            out.append(jnp.maximum(mx[m], jnp.max(st, 0, keepdims=True)))
        return tuple(out)

    mx = _loop_by_pairs(i, phase1, tuple(jnp.full((1, tq), -jnp.inf, F32) for _ in range(2)))
    visible = ((lax.broadcasted_iota(I32, (tk, tq), 0) // CHUNK)
               <= (lax.broadcasted_iota(I32, (tk, tq), 1) // CHUNK))
    mxs = []
    for m in range(2):
        st = jnp.where(visible, scores(i, m), -jnp.inf)
        st_s[i, m] = st
        mxs.append(jnp.maximum(mx[m], jnp.max(st, 0, keepdims=True)))

    acc_s[...] = jnp.zeros_like(acc_s)

    def phase2(c, carry):
        vt = vt_s[c]
        for m in range(2):
            p = jnp.exp2(st_s[c, m] - mxs[m])
            acc_s[m] += jnp.dot(vt, p.astype(BF16), preferred_element_type=F32)
        return carry

    _loop_by_pairs(i + 1, phase2, 0)
    o0, o1 = (acc_s[m, :wv] * (1.0 / acc_s[m, wv:wv + 1]) for m in range(2))
    ot = o0 - lam_ref[0] * o1
    ot = ot * lax.rsqrt(jnp.mean(ot * ot, 0, keepdims=True) + LN_EPS)
    o_ref[...] = (ot.T * (g_ref[...] * out_scale)).astype(BF16)


def _attention(proj, lam, subln_g, *, B, S, heads, dh, att_qk, lambda_init):
    T = B * S
    wv = 2 * dh
    tq = _tile(S, 1024)
    tk = tq
    nq = S // tq
    kern = functools.partial(_attn_kernel, dh=dh, tk=tk, out_scale=1.0 - lambda_init)
    ones_rows = 2 * SUBLANES
    scratch = [pltpu.VMEM((wv, tq), BF16), pltpu.VMEM((S // tk, wv + ones_rows, tk), BF16),
               pltpu.VMEM((S // tk, 2, tk, tq), F32), pltpu.VMEM((2, wv + ones_rows, tq), F32)]
    return pl.pallas_call(
        kern, grid=(B, heads, nq),
        in_specs=[pl.BlockSpec(memory_space=pltpu.SMEM),
                  pl.BlockSpec((tq, wv), lambda b, h, i: (b * nq + i, h)),
                  pl.BlockSpec((S, wv), lambda b, h, i: (b, att_qk // wv + h)),
                  pl.BlockSpec((S, wv), lambda b, h, i: (b, 2 * att_qk // wv + h)),
                  pl.BlockSpec((1, wv), lambda b, h, i: (0, 0))],
        out_specs=pl.BlockSpec((tq, wv), lambda b, h, i: (b * nq + i, h)),
        out_shape=jax.ShapeDtypeStruct((T, heads * wv), BF16),
        scratch_shapes=scratch,
        compiler_params=_params("parallel", "parallel", "arbitrary"), name="attention",
    )(lam, proj, proj, proj, subln_g.reshape(1, wv))


def _gelu_tanh(x):
    return 0.5 * x * (1.0 + jnp.tanh(0.7978845608028654 * (x + 0.044715 * (x * x * x))))


def _rglru_kernel(xr_ref, yr_ref, cw_ref, cb_ref, wa_ref, ba_ref, wx_ref, bx_ref, lam_ref, o_ref,
                  a_s, u_s, xprev_s, h_s, *, n_heads, blk, scan_cols):
    @pl.when(pl.program_id(1) == 0)
    def _():
        xprev_s[...] = jnp.zeros_like(xprev_s)
        h_s[...] = jnp.zeros_like(h_s)

    tt, D = xr_ref.shape
    z = -lam_ref[...]
    sp = jnp.maximum(z, 0.0) + jnp.log1p(jnp.exp(-jnp.abs(z)))
    for hh in range(n_heads):
        sl = slice(hh * blk, (hh + 1) * blk)
        xe = jnp.concatenate([xprev_s[:, sl], xr_ref[:, sl].astype(F32)], axis=0)
        xc = cb_ref[:, sl]
        for j in range(CONV_WIDTH):
            o = SUBLANES - (CONV_WIDTH - 1) + j
            xc = xc + cw_ref[j:j + 1, sl] * xe[o:o + tt]
        xb = xc.astype(BF16)
        r = jax.nn.sigmoid(jnp.dot(xb, wa_ref[hh], preferred_element_type=F32) + ba_ref[:, sl])
        ig = jax.nn.sigmoid(jnp.dot(xb, wx_ref[hh], preferred_element_type=F32) + bx_ref[:, sl])
        log_a = (-LRU_C * r) * sp[:, sl]
        a = jnp.exp(log_a)
        a_s[:, sl] = a
        u_s[:, sl] = jnp.sqrt(-jnp.tanh(log_a) * (a * a + 1.0)) * ig * xc
    xprev_s[...] = xr_ref[tt - SUBLANES:tt, :].astype(F32)

    row = lax.broadcasted_iota(I32, (SUBLANES, scan_cols), 0)

    def scan_body(g, h):
        r0 = pl.multiple_of(g * SUBLANES, SUBLANES)
        outs = []
        for c in range(D // scan_cols):
            cs = slice(c * scan_cols, (c + 1) * scan_cols)
            a = a_s[pl.ds(r0, SUBLANES), cs]
            u = u_s[pl.ds(r0, SUBLANES), cs]
            for s in (1, 2, 4):
                valid = row >= s
                u = jnp.where(valid, a * pltpu.roll(u, s, 0) + u, u)
                a = jnp.where(valid, a * pltpu.roll(a, s, 0), a)
            hg = u + a * h[:, cs]
            u_s[pl.ds(r0, SUBLANES), cs] = hg
            outs.append(hg[SUBLANES - 1:SUBLANES, :])
        return jnp.concatenate(outs, axis=1)

    h_s[...] = lax.fori_loop(0, tt // SUBLANES, scan_body, h_s[...])
    o_ref[...] = (u_s[...] * _gelu_tanh(yr_ref[...].astype(F32))).astype(BF16)


def _rglru(proj, conv_w, conv_b, w_ra, b_ra, w_rx, b_rx, lru_lambda, *, B, S, off_xr):
    T = B * S
    n_heads, blk, _ = w_ra.shape
    D = n_heads * blk
    tt = _tile(S, 256)
    nt = S // tt
    assert off_xr % D == 0
    cx = off_xr // D
    vec = pl.BlockSpec((1, D), lambda b, t: (0, 0))
    wspec = pl.BlockSpec((n_heads, blk, blk), lambda b, t: (0, 0, 0))
    kern = functools.partial(_rglru_kernel, n_heads=n_heads, blk=blk, scan_cols=_tile(D, 512))
    return pl.pallas_call(
        kern, grid=(B, nt),
        in_specs=[pl.BlockSpec((tt, D), lambda b, t: (b * nt + t, cx)),
                  pl.BlockSpec((tt, D), lambda b, t: (b * nt + t, cx + 1)),
                  pl.BlockSpec((CONV_WIDTH, D), lambda b, t: (0, 0)), vec,
                  wspec, vec, wspec, vec, vec],
        out_specs=pl.BlockSpec((tt, D), lambda b, t: (b * nt + t, 0)),
        out_shape=jax.ShapeDtypeStruct((T, D), BF16),
        scratch_shapes=[pltpu.VMEM((tt, D), F32), pltpu.VMEM((tt, D), F32),
                        pltpu.VMEM((SUBLANES, D), F32), pltpu.VMEM((1, D), F32)],
        compiler_params=_params("parallel", "arbitrary"), name="rglru",
    )(proj, proj, conv_w, conv_b.reshape(1, D), w_ra.astype(BF16), b_ra.reshape(1, D),
      w_rx.astype(BF16), b_rx.reshape(1, D), lru_lambda.reshape(1, D))


def _mix_kernel(at_ref, rn_ref, wa_ref, wr_ref, g0_ref, g1_ref, b0_ref, b1_ref, o_ref):
    pa = jnp.dot(at_ref[...], wa_ref[...], preferred_element_type=F32)
    pr = jnp.dot(rn_ref[...], wr_ref[...], preferred_element_type=F32)
    s0 = jax.nn.sigmoid(g0_ref[...].astype(F32) + b0_ref[...])
    s1 = jax.nn.sigmoid(g1_ref[...].astype(F32) + b1_ref[...])
    o_ref[...] = (s0 * pa + s1 * pr).astype(BF16)


def _mix(attn, rnn, wa, wr, proj, b_gate, *, off_g):
    T, Ka = attn.shape
    Kr = rnn.shape[1]
    D = wa.shape[1]
    tm = _tile(T, 512)
    tn = _tile(D, 512)
    assert off_g % tn == 0
    cg = off_g // tn
    nn = D // tn
    return pl.pallas_call(
        _mix_kernel, grid=(T // tm, nn),
        in_specs=[pl.BlockSpec((tm, Ka), lambda i, j: (i, 0)),
                  pl.BlockSpec((tm, Kr), lambda i, j: (i, 0)),
                  pl.BlockSpec((Ka, tn), lambda i, j: (0, j)),
                  pl.BlockSpec((Kr, tn), lambda i, j: (0, j)),
                  pl.BlockSpec((tm, tn), lambda i, j: (i, cg + j)),
                  pl.BlockSpec((tm, tn), lambda i, j: (i, cg + nn + j)),
                  pl.BlockSpec((1, tn), lambda i, j: (0, j)),
                  pl.BlockSpec((1, tn), lambda i, j: (0, nn + j))],
        out_specs=pl.BlockSpec((tm, tn), lambda i, j: (i, j)),
        out_shape=jax.ShapeDtypeStruct((T, D), BF16),
        compiler_params=_params("parallel", "arbitrary"), name="mix",
    )(attn, rnn, wa, wr, proj, proj, b_gate.reshape(1, 2 * D), b_gate.reshape(1, 2 * D))


def _out_ln_kernel(a_ref, w_ref, h0_ref, g_ref, b_ref, hf_ref, hp_ref, acc_s, *, alpha):
    j = pl.program_id(1)
    n_n, _, tn = acc_s.shape
    acc_s[j] = alpha * h0_ref[...] + jnp.dot(a_ref[...], w_ref[...], preferred_element_type=F32)

    @pl.when(j == n_n - 1)
    def _():
        D = n_n * tn
        mu = sum(jnp.sum(acc_s[c], -1, keepdims=True) for c in range(n_n)) / D
        var = sum(jnp.sum(jnp.square(acc_s[c] - mu), -1, keepdims=True) for c in range(n_n)) / D
        rstd = lax.rsqrt(var + LN_EPS)
        hn = n_n // 2
        for c in range(hn):
            ys = []
            for cc in (c, c + hn):
                cs = slice(cc * tn, (cc + 1) * tn)
                y = (acc_s[cc] - mu) * rstd * g_ref[:, cs] + b_ref[:, cs]
                hf_ref[:, cs] = y
                ys.append(y)
            hp_ref[:, c * tn:(c + 1) * tn] = _pack_halves(jnp.concatenate(ys, axis=1))


def _out_ln(mixed, w, h0f, g, b, *, alpha):
    T, K = mixed.shape
    D = w.shape[1]
    tm = _tile(T, 512)
    tn = _tile(D // 2, 512)
    vec = pl.BlockSpec((1, D), lambda i, j: (0, 0))
    return pl.pallas_call(
        functools.partial(_out_ln_kernel, alpha=alpha), grid=(T // tm, D // tn),
        in_specs=[pl.BlockSpec((tm, K), lambda i, j: (i, 0), pipeline_mode=pl.Buffered(1)),
                  pl.BlockSpec((K, tn), lambda i, j: (0, j)),
                  pl.BlockSpec((tm, tn), lambda i, j: (i, j)), vec, vec],
        out_specs=[pl.BlockSpec((tm, D), lambda i, j: (i, 0)),
                   pl.BlockSpec((tm, D // 2), lambda i, j: (i, 0))],
        out_shape=[jax.ShapeDtypeStruct((T, D), F32), jax.ShapeDtypeStruct((T, D // 2), U32)],
        scratch_shapes=[pltpu.VMEM((D // tn, tm, tn), F32)],
        compiler_params=_params("parallel", "arbitrary"), name="out_ln",
    )(mixed, w, h0f, g.reshape(1, D), b.reshape(1, D))


def _seg_allreduce(x, lane, width, op):
    n = x.shape[1]
    s = 1
    while s < width:
        partner = jnp.where((lane & s) == 0, pltpu.roll(x, n - s, 1), pltpu.roll(x, s, 1))
        x = op(x, partner)
        s *= 2
    return x


def _router_kernel(h_ref, w_ref, bias_ref, w8_ref, idx_ref, pos_ref, cnt_ref, tri_s, run_s, w2_s):
    E = w_ref.shape[1]

    @pl.when(pl.program_id(0) == 0)
    def _():
        tm = tri_s.shape[0]
        tri_s[...] = jnp.where(lax.broadcasted_iota(I32, (tm, tm), 1) < lax.broadcasted_iota(I32, (tm, tm), 0),
                               1.0, 0.0).astype(BF16)
        run_s[...] = jnp.zeros_like(run_s)
        w = w_ref[...]
        w_hi = w.astype(BF16)
        w2_s[:, :E] = w_hi
        w2_s[:, E:] = (w - w_hi.astype(F32)).astype(BF16)

    h = h_ref[...]
    h_hi = h.astype(BF16)
    h_lo = (h - h_hi.astype(F32)).astype(BF16)
    hh = jnp.dot(h_hi, w2_s[...], preferred_element_type=F32)
    logits = hh[:, :E] + (hh[:, E:] + jnp.dot(h_lo, w2_s[:, :E], preferred_element_type=F32))
    scores = jax.nn.sigmoid(logits)
    biased = scores + bias_ref[...]
    gw = E // N_GROUPS
    lane = lax.broadcasted_iota(I32, biased.shape, 1)
    neg = -jnp.inf

    m1 = _seg_allreduce(biased, lane, gw, jnp.maximum)
    is_m1 = biased == m1
    cnt = _seg_allreduce(is_m1.astype(F32), lane, gw, jnp.add)
    m2 = jnp.where(cnt >= 2.0, m1, _seg_allreduce(jnp.where(is_m1, neg, biased), lane, gw, jnp.maximum))
    gs = m1 + m2

    gi = lane // gw
    rank = jnp.zeros(biased.shape, I32)
    for k in range(1, N_GROUPS):
        other = pltpu.roll(gs, k * gw, 1)
        ogi = (gi - k) % N_GROUPS
        beats = (other > gs) | ((other == gs) & (ogi < gi))
        rank = rank + beats.astype(I32)
    cur = jnp.where(rank < TOPK_GROUPS, biased, neg)

    sel = jnp.zeros(biased.shape, jnp.bool_)
    firsts = []
    for k in range(TOP_K):
        mx = jnp.max(cur, -1, keepdims=True)
        first = jnp.min(jnp.where(cur == mx, lane, E), -1, keepdims=True)
        pick = lane == first
        sel = sel | pick
        firsts.append(first)
        cur = jnp.where(pick, neg, cur)

    sel_f = jnp.where(sel, 1.0, 0.0)
    rank_in_e = run_s[...] + jnp.dot(tri_s[...], sel_f.astype(BF16), preferred_element_type=F32)
    run_s[...] = run_s[...] + jnp.sum(sel_f, 0, keepdims=True)
    cnt_ref[...] = run_s[...]

    wsum = jnp.sum(jnp.where(sel, scores, 0.0), -1, keepdims=True)
    w8 = jnp.zeros(biased.shape, F32)
    idx = jnp.zeros(biased.shape, I32)
    pos = jnp.zeros(biased.shape, F32)
    for k in range(TOP_K):
        pick = lane == firsts[k]
        wk = jnp.sum(jnp.where(pick, scores, 0.0), -1, keepdims=True)
        pk = jnp.sum(jnp.where(pick, rank_in_e, 0.0), -1, keepdims=True)
        w8 = jnp.where(lane == k, wk / wsum * ROUTED_SCALE, w8)
        idx = jnp.where(lane == k, firsts[k], idx)
        pos = jnp.where(lane == k, pk, pos)
    w8_ref[...] = w8
    idx_ref[...] = idx
    pos_ref[...] = pos.astype(I32)


def _router(h1f, w_router, router_bias):
    T, D = h1f.shape
    E = w_router.shape[1]
    assert E == LANES and E % N_GROUPS == 0
    tm = _tile(T, 512)
    row = pl.BlockSpec((tm, E), lambda i: (i, 0))
    return pl.pallas_call(
        _router_kernel, grid=(T // tm,),
        in_specs=[pl.BlockSpec((tm, D), lambda i: (i, 0)),
                  pl.BlockSpec((D, E), lambda i: (0, 0)),
                  pl.BlockSpec((1, E), lambda i: (0, 0))],
        out_specs=[row, row, row, pl.BlockSpec((1, E), lambda i: (0, 0))],
        out_shape=[jax.ShapeDtypeStruct((T, E), F32), jax.ShapeDtypeStruct((T, E), I32),
                   jax.ShapeDtypeStruct((T, E), I32), jax.ShapeDtypeStruct((1, E), F32)],
        scratch_shapes=[pltpu.VMEM((tm, tm), BF16), pltpu.VMEM((1, E), F32), pltpu.VMEM((D, 2 * E), BF16)],
        compiler_params=_params("arbitrary"), name="router",
    )(h1f, w_router, router_bias.reshape(1, E))


def _dispatch_kernel(fill_start_ref, fill_n_ref, d8_ref, hp_ref, xs_hbm, zero_s, sem, zsem):
    tm = hp_ref.shape[0]
    E = fill_n_ref.shape[0]

    def row_copy(r, k):
        return pltpu.make_async_copy(hp_ref.at[pl.ds(r, 1), :],
                                     xs_hbm.at[pl.ds(d8_ref[0, 0, r * TOP_K + k], 1), :], sem.at[0])

    def issue(r, c):
        for k in range(TOP_K):
            row_copy(r, k).start()
        return c
    lax.fori_loop(0, tm, issue, 0)

    @pl.when(pl.program_id(0) == 0)
    def _():
        zero_s[...] = jnp.zeros_like(zero_s)
        sizes = [1 << b for b in reversed(range(3, EXPERT_ROWS.bit_length() - 1))]

        def per_expert(e, c):
            s0 = fill_start_ref[e]
            n = fill_n_ref[e]
            head = jnp.minimum((-s0) % SUBLANES, n)
            body = (n - head) // SUBLANES * SUBLANES
            tail = n - head - body
            for start in (True, False):
                def fill(row, size, pred):
                    cp = pltpu.make_async_copy(zero_s.at[pl.ds(0, size), :], xs_hbm.at[pl.ds(row, size), :],
                                               zsem.at[0])

                    @pl.when(pred)
                    def _():
                        cp.start() if start else cp.wait()

                for r in range(SUBLANES - 1):
                    fill(s0 + r, 1, r < head)
                off = s0 + head
                for b in sizes:
                    fill(pl.multiple_of(off, SUBLANES), b, (body & b) != 0)
                    off = off + (body & b)
                for r in range(SUBLANES - 1):
                    fill(off + r, 1, r < tail)
            return c
        lax.fori_loop(0, E, per_expert, 0)

    def wait(r, c):
        for k in range(TOP_K):
            row_copy(r, k).wait()
        return c
    lax.fori_loop(0, tm, wait, 0)


def _dispatch(h1p, dest8, fill_start, fill_n, n_rows):
    T, W = h1p.shape
    tm = _tile(T, 512)
    nt = T // tm
    d8 = dest8.reshape(nt, 1, tm * TOP_K)
    grid_spec = pltpu.PrefetchScalarGridSpec(
        num_scalar_prefetch=2, grid=(nt,),
        in_specs=[pl.BlockSpec((1, 1, tm * TOP_K), lambda i, fs, fn: (i, 0, 0), memory_space=pltpu.SMEM),
                  pl.BlockSpec((tm, W), lambda i, fs, fn: (i, 0))],
        out_specs=pl.BlockSpec(memory_space=pl.ANY),
        scratch_shapes=[pltpu.VMEM((EXPERT_ROWS // 2, W), U32), pltpu.SemaphoreType.DMA((1,)),
                        pltpu.SemaphoreType.DMA((1,))])
    return pl.pallas_call(
        _dispatch_kernel, grid_spec=grid_spec,
        out_shape=jax.ShapeDtypeStruct((n_rows, W), U32),
        compiler_params=_params("arbitrary"), name="dispatch",
    )(fill_start, fill_n, d8, h1p)


def _expert_switch(plan_ref, i, w_hbm, wbuf, sem, recast):
    def copies(e, slot):
        return [pltpu.make_async_copy(w.at[e], wbuf.at[slot, t], sem.at[slot, t]) for t, w in enumerate(w_hbm)]

    @pl.when(plan_ref[1, i] == 1)
    def _():
        e, slot, nxt = plan_ref[0, i], plan_ref[2, i], plan_ref[3, i]

        @pl.when(i == 0)
        def _():
            for c in copies(e, slot):
                c.start()

        for c in copies(e, slot):
            c.wait()
        recast(slot)

        @pl.when(nxt >= 0)
        def _():
            for c in copies(nxt, 1 - slot):
                c.start()


def _expert_step(plan_ref, n_used_ref, n_rows, switch, compute, clear):
    R = EXPERT_ROWS
    assert n_rows == 2 * R
    i0 = pl.program_id(0) * 2
    merged = jnp.logical_and(i0 + 1 < n_used_ref[0], plan_ref[1, i0 + 1] == 0)

    @pl.when(merged)
    def _():
        switch(i0)
        compute(slice(0, 2 * R))

    @pl.when(jnp.logical_not(merged))
    def _():
        for h in range(2):
            rows = slice(h * R, (h + 1) * R)

            @pl.when(i0 + h < n_used_ref[0])
            def _():
                switch(i0 + h)
                compute(rows)

            @pl.when(i0 + h >= n_used_ref[0])
            def _():
                clear(rows)


def _experts_up_kernel(plan_ref, n_used_ref, xs_ref, wg_hbm, wu_hbm, hb_ref, wbuf, wgu_s, sem):
    F = wg_hbm.shape[2]

    def recast(slot):
        wgu_s[:, :F] = wbuf[slot, 0].astype(BF16)
        wgu_s[:, F:] = wbuf[slot, 1].astype(BF16)

    def compute(rows):
        lo, hi = _unpack_halves(xs_ref[rows, :])
        hw = lo.shape[1]
        gu = (jnp.dot(lo.astype(BF16), wgu_s[:hw, :], preferred_element_type=F32)
              + jnp.dot(hi.astype(BF16), wgu_s[hw:, :], preferred_element_type=F32))
        hb_ref[rows, :] = (_silu(gu[:, :F]) * gu[:, F:]).astype(BF16)

    def clear(rows):
        hb_ref[rows, :] = jnp.zeros((rows.stop - rows.start, F), BF16)

    _expert_step(plan_ref, n_used_ref, xs_ref.shape[0],
                 lambda i: _expert_switch(plan_ref, i, (wg_hbm, wu_hbm), wbuf, sem, recast), compute, clear)


def _experts_down_kernel(plan_ref, n_used_ref, hb_ref, wd_hbm, ys_ref, wbuf, wd_s, sem):
    def recast(slot):
        wd_s[...] = wbuf[slot, 0].astype(BF16)

    def compute(rows):
        ys_ref[rows, :] = _pack_halves(jnp.dot(hb_ref[rows, :], wd_s[...], preferred_element_type=F32))

    def clear(rows):
        ys_ref[rows, :] = jnp.zeros((rows.stop - rows.start, ys_ref.shape[1]), U32)

    _expert_step(plan_ref, n_used_ref, hb_ref.shape[0],
                 lambda i: _expert_switch(plan_ref, i, (wd_hbm,), wbuf, sem, recast), compute, clear)


def _experts(xs, plan, n_used, wg, wu, wd):
    n_rows, W = xs.shape
    E, D, F = wg.shape
    R = EXPERT_ROWS
    per_step = 2
    n_steps = n_rows // (per_step * R)
    assert n_steps * per_step * R == n_rows
    RS = per_step * R
    xrow = lambda i, pn, nu: (jnp.minimum(i, jnp.maximum(nu[0] - 1, 0) // per_step), 0)
    hbm = pl.BlockSpec(memory_space=pl.ANY)
    hb = pl.pallas_call(
        _experts_up_kernel,
        grid_spec=pltpu.PrefetchScalarGridSpec(
            num_scalar_prefetch=2, grid=(n_steps,),
            in_specs=[pl.BlockSpec((RS, W), xrow), hbm, hbm],
            out_specs=pl.BlockSpec((RS, F), lambda i, pn, nu: (i, 0)),
            scratch_shapes=[pltpu.VMEM((2, 2, D, F), F32), pltpu.VMEM((D, 2 * F), BF16),
                            pltpu.SemaphoreType.DMA((2, 2))]),
        out_shape=jax.ShapeDtypeStruct((n_rows, F), BF16),
        compiler_params=_params("arbitrary"), name="experts_up",
    )(plan, n_used, xs, wg, wu)
    return pl.pallas_call(
        _experts_down_kernel,
        grid_spec=pltpu.PrefetchScalarGridSpec(
            num_scalar_prefetch=2, grid=(n_steps,),
            in_specs=[pl.BlockSpec((RS, F), lambda i, pn, nu: (i, 0)), hbm],
            out_specs=pl.BlockSpec((RS, W), lambda i, pn, nu: (i, 0)),
            scratch_shapes=[pltpu.VMEM((2, 1, F, D), F32), pltpu.VMEM((F, D), BF16),
                            pltpu.SemaphoreType.DMA((2, 1))]),
        out_shape=jax.ShapeDtypeStruct((n_rows, W), U32),
        compiler_params=_params("arbitrary"), name="experts_down",
    )(plan, n_used, hb, wd)


def _final_kernel(d8a_ref, d8b_ref, d8n_ref, ys_hbm, hf_ref, hp_ref, w8_ref, wsg_ref, wsu_ref, wsd_ref, g_ref, b_ref,
                  o_ref, buf_a, buf_b, sem, *, alpha):
    i = pl.program_id(0)
    nt = pl.num_programs(0)
    tm = hf_ref.shape[0] // 2
    n = tm * TOP_K

    def issue(d8_ref, rows, buf, s):
        for r in rows:
            pltpu.make_async_copy(ys_hbm.at[pl.ds(d8_ref[0, 0, r], 1), :],
                                  buf.at[r // SUBLANES, pl.ds(r % SUBLANES, 1), :], sem.at[s]).start()

    def wait_all(buf, s):
        def wait(g, c):
            for sub in range(SUBLANES):
                pltpu.make_async_copy(ys_hbm.at[pl.ds(0, 1), :], buf.at[g, pl.ds(sub, 1), :], sem.at[s]).wait()
            return c
        lax.fori_loop(0, n // SUBLANES, wait, 0)

    @pl.when(i == 0)
    def _():
        def first(g, c):
            for sub in range(SUBLANES):
                pltpu.make_async_copy(ys_hbm.at[pl.ds(d8a_ref[0, 0, g * SUBLANES + sub], 1), :],
                                      buf_a.at[g, pl.ds(sub, 1), :], sem.at[0]).start()
            return c
        lax.fori_loop(0, n // SUBLANES, first, 0)

    def shared(rows):
        lo, hi = _unpack_halves(hp_ref[rows, :])
        x = jnp.concatenate([lo.astype(BF16), hi.astype(BF16)], axis=1)
        hs = (_silu(jnp.dot(x, wsg_ref[...], preferred_element_type=F32))
              * jnp.dot(x, wsu_ref[...], preferred_element_type=F32)).astype(BF16)
        return alpha * hf_ref[rows, :] + jnp.dot(hs, wsd_ref[...], preferred_element_type=F32)

    def combine(rows, acc, buf):
        r_lo = r_hi = None
        gk = tm // SUBLANES
        for k in range(TOP_K):
            lo, hi = _unpack_halves(buf[k * gk:(k + 1) * gk].reshape(tm, buf.shape[-1]))
            wk = w8_ref[rows, k:k + 1]
            r_lo = lo * wk if r_lo is None else r_lo + lo * wk
            r_hi = hi * wk if r_hi is None else r_hi + hi * wk
        o_ref[rows, :] = _ln_rows(acc + jnp.concatenate([r_lo, r_hi], axis=1), g_ref[...], b_ref[...])

    rows_a, rows_b = slice(0, tm), slice(tm, 2 * tm)
    issue(d8b_ref, range(0, n // 2), buf_b, 1)
    acc = shared(rows_a)
    wait_all(buf_a, 0)
    issue(d8b_ref, range(n // 2, n), buf_b, 1)
    combine(rows_a, acc, buf_a)
    issue(d8n_ref, range(0, n // 2), buf_a, 0)
    acc = shared(rows_b)
    wait_all(buf_b, 1)
    issue(d8n_ref, range(n // 2, n), buf_a, 0)
    combine(rows_b, acc, buf_b)

    @pl.when(i == nt - 1)
    def _():
        wait_all(buf_a, 0)


def _final(dest8, w8, ys, h1f, h1p, wsg, wsu, wsd, g, b, *, alpha):
    T, D = h1f.shape
    W = ys.shape[1]
    F = wsg.shape[1]
    tm = _tile(T // 2, 128)
    nt = T // (2 * tm)
    d8 = dest8.reshape(2 * nt, tm, TOP_K).transpose(0, 2, 1).reshape(2 * nt, 1, TOP_K * tm)
    row = lambda w: pl.BlockSpec((2 * tm, w), lambda i: (i, 0))
    vec = pl.BlockSpec((1, D), lambda i: (0, 0))
    idx_spec = lambda f: pl.BlockSpec((1, 1, TOP_K * tm), lambda i: (f(i), 0, 0), memory_space=pltpu.SMEM)
    wspec = lambda shape: pl.BlockSpec(shape, lambda i: (0, 0), pipeline_mode=pl.Buffered(1))
    gbuf = pltpu.VMEM((TOP_K * tm // SUBLANES, SUBLANES, W), U32)
    return pl.pallas_call(
        functools.partial(_final_kernel, alpha=alpha), grid=(nt,),
        in_specs=[idx_spec(lambda i: 2 * i), idx_spec(lambda i: 2 * i + 1),
                  idx_spec(lambda i: jnp.minimum(2 * i + 2, 2 * nt - 2)),
                  pl.BlockSpec(memory_space=pl.ANY), row(D), row(W), row(TOP_K),
                  wspec((D, F)), wspec((D, F)), wspec((F, D)), vec, vec],
        out_specs=row(D),
        out_shape=jax.ShapeDtypeStruct((T, D), F32),
        scratch_shapes=[gbuf, gbuf, pltpu.SemaphoreType.DMA((2,))],
        compiler_params=_params("arbitrary"), name="final",
    )(d8, d8, d8, ys, h1f, h1p, w8, wsg, wsu, wsd, g.reshape(1, D), b.reshape(1, D))


def _rope_tables(positions, dh):
    rot = dh // ROT_FRACTION
    half = rot // 2
    inv_freq = jnp.power(ROPE_THETA, -jnp.arange(0, rot, 2, dtype=F32) / rot)
    ang = positions.astype(F32).reshape(-1, 1) * inv_freq
    cos, sin = jnp.cos(ang), jnp.sin(ang)
    T = ang.shape[0]
    cosf = jnp.concatenate([cos, cos, jnp.ones((T, dh - rot), F32)], axis=1)
    sinf = jnp.concatenate([-sin, sin, jnp.zeros((T, dh - rot), F32)], axis=1)
    return cosf, sinf, half


def _plan_kernel(idx_ref, pos_ref, pstart_ref, dest_ref):
    starts = jnp.broadcast_to(pstart_ref[...], idx_ref.shape)
    dest_ref[...] = jnp.take_along_axis(starts, idx_ref[...], axis=1) + pos_ref[...]


def _dispatch_plan(idx, pos, cnt):
    T, E = idx.shape
    R = EXPERT_ROWS
    n_blocks = -(-(T * TOP_K) // R) + E
    counts = cnt.reshape(E).astype(I32)
    padded = (counts + R - 1) // R * R
    pend = jnp.cumsum(padded)
    pstart = pend - padded
    tm = _tile(T, 1024)
    row = pl.BlockSpec((tm, E), lambda i: (i, 0))
    dest = pl.pallas_call(
        _plan_kernel, grid=(T // tm,),
        in_specs=[row, row, pl.BlockSpec((1, E), lambda i: (0, 0))], out_specs=row,
        out_shape=jax.ShapeDtypeStruct((T, E), I32),
        compiler_params=_params("parallel"), name="plan",
    )(idx, pos, pstart.reshape(1, E).astype(I32))
    dest8 = dest[:, :TOP_K]
    blk = jnp.arange(n_blocks, dtype=I32)
    ex = jnp.arange(E, dtype=I32)
    blk_e = jnp.minimum(jnp.sum(pend[None, :] <= (blk * R)[:, None], axis=1), E - 1).astype(I32)
    n_used = (pend[-1:] // R).astype(I32)
    first = (blk < n_used[0]) & ((blk == 0) | (blk_e != jnp.roll(blk_e, 1)))
    slot = (jnp.cumsum(first.astype(I32)) - 1) % 2
    cand = jnp.where(counts > 0, ex, E)
    later = jnp.min(jnp.where(ex[None, :] > ex[:, None], cand[None, :], E), axis=1)
    nxt_e = jnp.where(later < E, later, -1)
    nxt = jnp.sum(jnp.where(blk_e[:, None] == ex[None, :], nxt_e[None, :], 0), axis=1)
    plan = jnp.stack([blk_e, first.astype(I32), slot.astype(I32), nxt.astype(I32)])
    fill_start = (pstart + counts).astype(I32)
    fill_n = (padded - counts).astype(I32)
    return plan, n_used, dest8.astype(I32), fill_start, fill_n, n_blocks * R


def kernel(x, positions, ln_in_g, ln_in_b, w_in, b_gate, lam_q1, lam_k1, lam_q2, lam_k2, subln_g, conv_w, conv_b, w_rg_a, b_rg_a, w_rg_x, b_rg_x, lru_lambda, w_proj_attn, w_proj_rnn, w_out, ln1_g, ln1_b, w_router, router_bias, w_exp_gate, w_exp_up, w_exp_down, w_sh_gate, w_sh_up, w_sh_down, ln2_g, ln2_b):
    B, S, D = x.shape
    T = B * S
    depth = w_in.shape[0]
    alpha = (2 * depth) ** 0.25
    dh = lam_q1.shape[-1]
    att_v = w_proj_attn.shape[1]
    d_rnn = conv_w.shape[-1]
    att_qk = (w_in.shape[2] - att_v - 2 * d_rnn - 2 * D) // 2
    heads = att_v // (2 * dh)
    assert dh == LANES and att_qk == att_v
    cosf, sinf, half = _rope_tables(positions, dh)

    hf, hb = _ln_in(x.reshape(T, D), ln_in_g, ln_in_b)
    for l in range(depth):
        lambda_init = 0.8 - 0.6 * math.exp(-0.3 * l)
        lam = (jnp.exp(jnp.sum(lam_q1[l] * lam_k1[l])) - jnp.exp(jnp.sum(lam_q2[l] * lam_k2[l]))
               + lambda_init).reshape(1).astype(F32)
        proj = _in_proj(hb, w_in[l].astype(BF16), cosf, sinf, att_qk=att_qk,
                        q_scale=dh ** -0.5 * LOG2E, half=half)
        attn = _attention(proj, lam, subln_g[l], B=B, S=S, heads=heads, dh=dh, att_qk=att_qk,
                          lambda_init=lambda_init)
        rnn = _rglru(proj, conv_w[l], conv_b[l], w_rg_a[l], b_rg_a[l], w_rg_x[l], b_rg_x[l], lru_lambda[l],
                     B=B, S=S, off_xr=2 * att_qk + att_v)
        mixed = _mix(attn, rnn, w_proj_attn[l].astype(BF16), w_proj_rnn[l].astype(BF16), proj, b_gate[l],
                     off_g=2 * att_qk + att_v + 2 * d_rnn)
        h1f, h1p = _out_ln(mixed, w_out[l].astype(BF16), hf, ln1_g[l], ln1_b[l], alpha=alpha)
        w8, idx, pos, cnt = _router(h1f, w_router[l], router_bias[l])
        plan, n_used, dest8, fill_start, fill_n, n_rows = _dispatch_plan(idx, pos, cnt)
        xs = _dispatch(h1p, dest8, fill_start, fill_n, n_rows)
        ys = _experts(xs, plan, n_used, w_exp_gate[l], w_exp_up[l], w_exp_down[l])
        hf = _final(dest8, w8[:, :TOP_K], ys, h1f, h1p, w_sh_gate[l].astype(BF16), w_sh_up[l].astype(BF16),
                    w_sh_down[l].astype(BF16), ln2_g[l], ln2_b[l], alpha=alpha)
        if l + 1 < depth:
            hb = hf.astype(BF16)
    return hf.reshape(B, S, D)
```
